```python
import jax, jax.numpy as jnp
from jax import lax
import numpy as np

D_MODEL = 1024
BATCH = 4
SEQ = 4096
DEPTH = 1

N_MEM = 256
HEAD_DIM = 64
ATTN_WIDTH = D_MODEL // 2
CONV_WIDTH = D_MODEL // 4
XATTN_WIDTH = D_MODEL // 4
N_ATTN_HEADS = ATTN_WIDTH // HEAD_DIM
N_XATTN_HEADS = 4
XATTN_HEAD_DIM = XATTN_WIDTH // N_XATTN_HEADS
MIX_WIDTH = ATTN_WIDTH + CONV_WIDTH + XATTN_WIDTH
IN_PROJ_WIDTH = 3 * ATTN_WIDTH + 3 * CONV_WIDTH + XATTN_WIDTH
DILATED_PATTERNS = ((128, 1), (512, 4), (2048, 16))
CONV_K = 3
D_FF = 4 * D_MODEL
ROPE_THETA = 10000.0
EPS = 1e-6
NEG_INF = -1e30

kernel_name = "hybrid_dilated_attn_shortconv_memxattn_block"


def rms_norm(x, g):
    xf = x.astype(jnp.float32)
    y = xf * lax.rsqrt(jnp.mean(xf * xf, axis=-1, keepdims=True) + EPS)
    return (y * g.astype(jnp.float32)).astype(x.dtype)


def apply_rope(t, positions):
    dh = t.shape[-1]
    half = dh // 2
    inv_freq = jnp.float32(ROPE_THETA) ** (-(jnp.arange(half, dtype=jnp.float32) * 2.0 / dh))
    ang = positions.astype(jnp.float32)[..., None] * inv_freq
    cos = jnp.cos(ang)[:, :, None, :]
    sin = jnp.sin(ang)[:, :, None, :]
    tf = t.astype(jnp.float32)
    t1, t2 = tf[..., :half], tf[..., half:]
    out = jnp.concatenate([t1 * cos - t2 * sin, t1 * sin + t2 * cos], axis=-1)
    return out.astype(t.dtype)


def dilated_window_attention(q, k, v, window, dilation):
    B, S, H, Dh = q.shape
    L = S // dilation
    n_back = window // dilation
    blk = n_back
    nb = -(-L // blk)
    Lp = nb * blk

    def to_blocks(t):
        t = t.reshape(B, L, dilation, H, Dh).transpose(0, 2, 1, 3, 4)
        t = jnp.pad(t, ((0, 0), (0, 0), (0, Lp - L), (0, 0), (0, 0)))
        return t.reshape(B, dilation, nb, blk, H, Dh)

    qb, kb, vb = to_blocks(q), to_blocks(k), to_blocks(v)

    def with_prev(t):
        prev = jnp.pad(t, ((0, 0), (0, 0), (1, 0), (0, 0), (0, 0), (0, 0)))[:, :, :-1]
        return jnp.concatenate([prev, t], axis=3)

    kw, vw = with_prev(kb), with_prev(vb)
    scale = Dh ** -0.5
    s = jnp.einsum('bdnqhc,bdnkhc->bdnhqk', qb, kw,
                   preferred_element_type=jnp.float32) * scale
    qi = jnp.arange(blk)[:, None]
    kj = jnp.arange(2 * blk)[None, :]
    band = (kj >= qi) & (kj <= qi + n_back)
    valid = band[None] & ((jnp.arange(nb)[:, None, None] > 0) | (kj[None] >= blk))
    s = jnp.where(valid[None, None, :, None], s, NEG_INF)
    lse = jax.nn.logsumexp(s, axis=-1)
    p = jnp.exp(s - lse[..., None])
    o = jnp.einsum('bdnhqk,bdnkhc->bdnqhc', p.astype(v.dtype), vw,
                   preferred_element_type=jnp.float32)
    o = o.reshape(B, dilation, Lp, H, Dh)[:, :, :L]
    o = o.transpose(0, 2, 1, 3, 4).reshape(B, S, H, Dh)
    lse = lse.transpose(0, 1, 2, 4, 3).reshape(B, dilation, Lp, H)[:, :, :L]
    lse = lse.transpose(0, 2, 1, 3).reshape(B, S, H)
    return o, lse


def dilated_mixture_attention(q, k, v):
    outs, lses = [], []
    for window, dilation in DILATED_PATTERNS:
        o, lse = dilated_window_attention(q, k, v, window, dilation)
        outs.append(o)
        lses.append(lse)
    w = jax.nn.softmax(jnp.stack(lses, axis=0), axis=0)
    o = jnp.sum(w[..., None] * jnp.stack(outs, axis=0), axis=0)
    return o.astype(q.dtype)


def short_gated_conv(b_gate, c_gate, u, conv_w):
    z = c_gate * u
    S = z.shape[1]
    zp = jnp.pad(z, ((0, 0), (CONV_K - 1, 0), (0, 0)))
    y = zp[:, 0:S] * conv_w[0]
    for tap in range(1, CONV_K):
        y = y + zp[:, tap:tap + S] * conv_w[tap]
    return b_gate * y


def memory_cross_attention(qx, mem_kv):
    B, S, _ = qx.shape
    q = qx.reshape(B, S, N_XATTN_HEADS, XATTN_HEAD_DIM)
    km, vm = jnp.split(mem_kv, 2, axis=-1)
    km = km.reshape(B, -1, N_XATTN_HEADS, XATTN_HEAD_DIM)
    vm = vm.reshape(B, -1, N_XATTN_HEADS, XATTN_HEAD_DIM)
    s = jnp.einsum('bshc,bmhc->bhsm', q, km,
                   preferred_element_type=jnp.float32) * (XATTN_HEAD_DIM ** -0.5)
    p = jax.nn.softmax(s, axis=-1)
    o = jnp.einsum('bhsm,bmhc->bshc', p.astype(vm.dtype), vm)
    return o.reshape(B, S, XATTN_WIDTH)


def setup_inputs(seed: int = 0) -> dict:
    key = jax.random.key(seed)
    ks = jax.random.split(key, 20)
    f32 = jnp.float32

    def w(k, shape, fan_in):
        return jax.random.normal(k, shape, f32) * (fan_in ** -0.5)

    def gain(k, width):
        return 1.0 + 0.05 * jax.random.normal(k, (DEPTH, width), f32)

    x = jax.random.normal(ks[0], (BATCH, SEQ, D_MODEL), f32)
    mem = jax.random.normal(ks[1], (BATCH, N_MEM, D_MODEL), f32)
    offset = jax.random.randint(ks[2], (BATCH, 1), 0, 1024, dtype=jnp.int32)
    positions = offset + jnp.arange(SEQ, dtype=jnp.int32)[None, :]
    return {
        "x": x,
        "mem": mem,
        "positions": positions,
        "g_pre_mix": gain(ks[3], D_MODEL),
        "g_mem": gain(ks[4], D_MODEL),
        "w_in": w(ks[5], (DEPTH, D_MODEL, IN_PROJ_WIDTH), D_MODEL),
        "w_mem_kv": w(ks[6], (DEPTH, D_MODEL, 2 * XATTN_WIDTH), D_MODEL),
        "conv_w": w(ks[7], (DEPTH, CONV_K, CONV_WIDTH), CONV_K),
        "g_attn_out": gain(ks[8], ATTN_WIDTH),
        "g_conv_out": gain(ks[9], CONV_WIDTH),
        "g_xattn_out": gain(ks[10], XATTN_WIDTH),
        "w_out": w(ks[11], (DEPTH, MIX_WIDTH, D_MODEL), MIX_WIDTH),
        "g_post_mix": gain(ks[12], D_MODEL),
        "g_pre_mlp": gain(ks[13], D_MODEL),
        "w_up": w(ks[14], (DEPTH, D_MODEL, D_FF), D_MODEL),
        "w_down": w(ks[15], (DEPTH, D_FF, D_MODEL), D_FF),
        "g_post_mlp": gain(ks[16], D_MODEL),
    }


def reference(x, mem, positions, g_pre_mix, g_mem, w_in, w_mem_kv, conv_w,
              g_attn_out, g_conv_out, g_xattn_out, w_out, g_post_mix,
              g_pre_mlp, w_up, w_down, g_post_mlp):
    B, S, _ = x.shape
    a0 = ATTN_WIDTH
    c0 = 3 * ATTN_WIDTH
    x0 = 3 * ATTN_WIDTH + 3 * CONV_WIDTH
    for l in range(DEPTH):
        h = rms_norm(x, g_pre_mix[l])
        proj = jnp.einsum('bsd,de->bse', h, w_in[l])

        q = proj[..., 0:a0].reshape(B, S, N_ATTN_HEADS, HEAD_DIM)
        k = proj[..., a0:2 * a0].reshape(B, S, N_ATTN_HEADS, HEAD_DIM)
        v = proj[..., 2 * a0:3 * a0].reshape(B, S, N_ATTN_HEADS, HEAD_DIM)
        q = apply_rope(q, positions)
        k = apply_rope(k, positions)
        y_attn = dilated_mixture_attention(q, k, v).reshape(B, S, ATTN_WIDTH)

        b_gate = proj[..., c0:c0 + CONV_WIDTH]
        c_gate = proj[..., c0 + CONV_WIDTH:c0 + 2 * CONV_WIDTH]
        u = proj[..., c0 + 2 * CONV_WIDTH:c0 + 3 * CONV_WIDTH]
        y_conv = short_gated_conv(b_gate, c_gate, u, conv_w[l])

        qx = proj[..., x0:x0 + XATTN_WIDTH]
        mem_kv = jnp.einsum('bmd,de->bme', rms_norm(mem, g_mem[l]), w_mem_kv[l])
        y_x = memory_cross_attention(qx, mem_kv)

        y = jnp.concatenate([rms_norm(y_attn, g_attn_out[l]),
                             rms_norm(y_conv, g_conv_out[l]),
                             rms_norm(y_x, g_xattn_out[l])], axis=-1)
        y = jnp.einsum('bse,ed->bsd', y, w_out[l])
        x = x + rms_norm(y, g_post_mix[l])

        h2 = rms_norm(x, g_pre_mlp[l])
        f = jnp.square(jax.nn.relu(jnp.einsum('bsd,df->bsf', h2, w_up[l])))
        f = jnp.einsum('bsf,fd->bsd', f, w_down[l])
        x = x + rms_norm(f, g_post_mlp[l])
    return x
```

```python
import functools

import jax
import jax.numpy as jnp
from jax import lax
from jax.experimental import pallas as pl
from jax.experimental.pallas import tpu as pltpu

F32 = jnp.float32
BF16 = jnp.bfloat16

HEAD_DIM = 64
N_MEM_HEADS = 4
DILATED_PATTERNS = ((128, 1), (512, 4), (2048, 16))
CONV_K = 3
ROPE_THETA = 10000.0
EPS = 1e-6
NEG_INF = -1e30

LANES = 128
BF16_SUBLANES = 16
ATTN_BLK = 128
VMEM_LIMIT_BYTES = 56 * 1024 * 1024


def _rms(x, g):
    return x * lax.rsqrt(jnp.mean(x * x, axis=-1, keepdims=True) + EPS) * g


def _in_proj_kernel(x_ref, pos_ref, g_ref, w_ref, o_ref, *, attn_width, chunk):
    h = _rms(x_ref[0], g_ref[...]).astype(BF16)

    lane = lax.broadcasted_iota(jnp.int32, (1, LANES), 1)
    half = HEAD_DIM // 2
    first_half = (lane % HEAD_DIM) < half
    freq = (lane % half).astype(F32)
    inv_freq = jnp.float32(ROPE_THETA) ** (-(freq * 2.0 / HEAD_DIM))
    ang = pos_ref[0].astype(F32) * inv_freq
    cos = jnp.cos(ang)
    sin = jnp.where(first_half, -jnp.sin(ang), jnp.sin(ang))
    q_scale = HEAD_DIM ** -0.5

    n_out = o_ref.shape[-1]
    for c0 in range(0, n_out, chunk):
        p = jnp.dot(h, w_ref[:, c0:c0 + chunk], preferred_element_type=F32)
        if c0 < 2 * attn_width:
            scale = q_scale if c0 < attn_width else 1.0
            for g0 in range(0, chunk, LANES):
                t = p[:, g0:g0 + LANES]
                rot = jnp.where(first_half,
                                pltpu.roll(t, LANES - half, 1),
                                pltpu.roll(t, half, 1))
                r = (t * cos + rot * sin) * scale
                o_ref[0, :, c0 + g0:c0 + g0 + LANES] = r.astype(BF16)
        else:
            o_ref[0, :, c0:c0 + chunk] = p.astype(BF16)


def _in_proj(x, pos3, g, w_bf16, *, attn_width, tm=512, chunk=512):
    B, S, D = x.shape
    n_out = w_bf16.shape[1]
    return pl.pallas_call(
        functools.partial(_in_proj_kernel, attn_width=attn_width, chunk=chunk),
        grid=(B, S // tm),
        in_specs=[
            pl.BlockSpec((1, tm, D), lambda b, t: (b, t, 0)),
            pl.BlockSpec((1, tm, 1), lambda b, t: (b, t, 0)),
            pl.BlockSpec((1, D), lambda b, t: (0, 0)),
            pl.BlockSpec((D, n_out), lambda b, t: (0, 0)),
        ],
        out_specs=pl.BlockSpec((1, tm, n_out), lambda b, t: (b, t, 0)),
        out_shape=jax.ShapeDtypeStruct((B, S, n_out), BF16),
        compiler_params=pltpu.CompilerParams(
            dimension_semantics=("parallel", "parallel"),
            vmem_limit_bytes=VMEM_LIMIT_BYTES),
        name="in_proj",
    )(x, pos3, g, w_bf16)


def _band_block(qb, kb, vb, bias, lane_lo):
    ms, ls, accs = [], [], []
    for head_lo in (True, False):
        sel = lane_lo if head_lo else jnp.logical_not(lane_lo)
        qh = jnp.where(sel, qb, jnp.zeros_like(qb))
        s = lax.dot_general(qh, kb, (((1,), (1,)), ((), ())),
                            preferred_element_type=F32) + bias
        m = jnp.max(s, axis=-1, keepdims=True)
        p = jnp.exp(s - m)
        ls.append(jnp.sum(p, axis=-1, keepdims=True))
        ms.append(m)
        accs.append(jnp.dot(p.astype(BF16), vb, preferred_element_type=F32))
    shape = accs[0].shape
    m_b = jnp.where(lane_lo, jnp.broadcast_to(ms[0], shape), jnp.broadcast_to(ms[1], shape))
    l_b = jnp.where(lane_lo, jnp.broadcast_to(ls[0], shape), jnp.broadcast_to(ls[1], shape))
    acc = jnp.where(lane_lo, accs[0], accs[1])
    return m_b, l_b, acc


def _dil_attn_kernel(q_ref, k_ref, v_ref, o_ref,
                     qf, kf, vf, qp, kp, vp, m_run, l_run, a_run,
                     m_pat, l_pat, a_pat, bias_ref, *, seq, patterns):
    blk = ATTN_BLK
    n_blocks = seq // blk
    lane_lo = lax.broadcasted_iota(jnp.int32, (1, LANES), 1) < HEAD_DIM

    qi = lax.broadcasted_iota(jnp.int32, (blk, 2 * blk), 0)
    kj = lax.broadcasted_iota(jnp.int32, (blk, 2 * blk), 1)
    bias_ref[0] = jnp.where(kj <= qi, 0.0, NEG_INF).astype(F32)
    bias_ref[1] = jnp.where((kj >= qi) & (kj <= qi + blk), 0.0, NEG_INF).astype(F32)

    qf[...] = q_ref[0].astype(F32)
    kf[...] = k_ref[0].astype(F32)
    vf[...] = v_ref[0].astype(F32)

    gather_rows = 256
    for window, dil in patterns:
        assert window // dil == blk
        sub_len = seq // dil
        nb = sub_len // blk
        assert nb >= 2 and sub_len % gather_rows == 0

        if dil == 1:
            q_src, k_src, v_src = q_ref.at[0], k_ref.at[0], v_ref.at[0]
            m_dst, l_dst, a_dst = m_run, l_run, a_run
        else:
            for r in range(dil):
                for c in range(sub_len // gather_rows):
                    src = pl.ds(r + dil * gather_rows * c, gather_rows, stride=dil)
                    dst = pl.ds(r * sub_len + gather_rows * c, gather_rows)
                    qp[dst, :] = qf[src, :].astype(BF16)
                    kp[dst, :] = kf[src, :].astype(BF16)
                    vp[dst, :] = vf[src, :].astype(BF16)
            q_src, k_src, v_src = qp, kp, vp
            m_dst, l_dst, a_dst = m_pat, l_pat, a_pat

        def block_body(g, carry, nb=nb, q_src=q_src, k_src=k_src, v_src=v_src,
                       m_dst=m_dst, l_dst=l_dst, a_dst=a_dst):
            n = g % nb
            q0 = pl.multiple_of(g * blk, blk)
            k0 = pl.multiple_of(jnp.where(n > 0, q0 - blk, q0), blk)
            bias = bias_ref[jnp.minimum(n, 1)]
            m_b, l_b, acc = _band_block(q_src[pl.ds(q0, blk), :],
                                        k_src[pl.ds(k0, 2 * blk), :],
                                        v_src[pl.ds(k0, 2 * blk), :],
                                        bias, lane_lo)
            m_dst[pl.ds(q0, blk), :] = m_b
            l_dst[pl.ds(q0, blk), :] = l_b
            a_dst[pl.ds(q0, blk), :] = acc
            return carry

        lax.fori_loop(0, n_blocks, block_body, 0)

        if dil != 1:
            for r in range(dil):
                for c in range(sub_len // gather_rows):
                    nat = pl.ds(r + dil * gather_rows * c, gather_rows, stride=dil)
                    per = pl.ds(r * sub_len + gather_rows * c, gather_rows)
                    m_old, m_new = m_run[nat, :], m_pat[per, :]
                    m_max = jnp.maximum(m_old, m_new)
                    w_old = jnp.exp(m_old - m_max)
                    w_new = jnp.exp(m_new - m_max)
                    m_run[nat, :] = m_max
                    l_run[nat, :] = l_run[nat, :] * w_old + l_pat[per, :] * w_new
                    a_run[nat, :] = a_run[nat, :] * w_old + a_pat[per, :] * w_new

    o_ref[0] = (a_run[...] / l_run[...]).astype(o_ref.dtype)


def _dil_attn(proj, *, attn_width):
    B, S, _ = proj.shape
    n_pairs = attn_width // LANES
    kern = functools.partial(_dil_attn_kernel, seq=S, patterns=DILATED_PATTERNS)
    col = lambda off: (lambda b, hp: (b, 0, off + hp))
    f32_buf = pltpu.VMEM((S, LANES), F32)
    bf16_buf = pltpu.VMEM((S, LANES), BF16)
    return pl.pallas_call(
        kern,
        grid=(B, n_pairs),
        in_specs=[pl.BlockSpec((1, S, LANES), col(0)),
                  pl.BlockSpec((1, S, LANES), col(n_pairs)),
                  pl.BlockSpec((1, S, LANES), col(2 * n_pairs))],
        out_specs=pl.BlockSpec((1, S, LANES), lambda b, hp: (b, 0, hp)),
        out_shape=jax.ShapeDtypeStruct((B, S, attn_width), BF16),
        scratch_shapes=[f32_buf, f32_buf, f32_buf,
                        bf16_buf, bf16_buf, bf16_buf,
                        f32_buf, f32_buf, f32_buf,
                        f32_buf, f32_buf, f32_buf,
                        pltpu.VMEM((2, ATTN_BLK, 2 * ATTN_BLK), F32)],
        compiler_params=pltpu.CompilerParams(
            dimension_semantics=("parallel", "parallel"),
            vmem_limit_bytes=VMEM_LIMIT_BYTES),
        name="dil_attn",
    )(proj, proj, proj)


def _mix_mlp_kernel(x_ref, ya_ref, bg_ref, cg_ref, u_ref, qx_ref, hc_ref, hu_ref,
                    mem_ref, g_mem_ref, w_mem_ref, conv_w_ref,
                    g_attn_ref, g_conv_ref, g_xattn_ref, w_out_ref, g_post_mix_ref,
                    g_pre_mlp_ref, w_up_ref, w_down_ref, g_post_mlp_ref,
                    o_ref, km_ref, vm_ref, *, ff_chunk):
    t = pl.program_id(1)
    xw = km_ref.shape[1]

    @pl.when(t == 0)
    def _():
        hm = _rms(mem_ref[0], g_mem_ref[...]).astype(BF16)
        kv = jnp.dot(hm, w_mem_ref[...], preferred_element_type=F32)
        km_ref[...] = kv[:, :xw].astype(BF16)
        vm_ref[...] = kv[:, xw:].astype(BF16)

    tm = x_ref.shape[1]

    z = cg_ref[0].astype(F32) * u_ref[0].astype(F32)
    hz = hc_ref[0].astype(F32) * hu_ref[0].astype(F32)
    hz = jnp.where(t > 0, hz, jnp.zeros_like(hz))
    z_ext = jnp.concatenate([hz, z], axis=0)
    cw = conv_w_ref[...]
    y_conv = z * cw[CONV_K - 1:CONV_K, :]
    for back in range(1, CONV_K):
        lo = BF16_SUBLANES - back
        y_conv = y_conv + z_ext[lo:lo + tm, :] * cw[CONV_K - 1 - back:CONV_K - back, :]
    y_conv = bg_ref[0].astype(F32) * y_conv

    qx = qx_ref[0]
    km = km_ref[...]
    vm = vm_ref[...]
    lane = lax.broadcasted_iota(jnp.int32, (1, xw), 1)
    xhd = xw // N_MEM_HEADS
    y_x = jnp.zeros((tm, xw), F32)
    for hd in range(N_MEM_HEADS):
        sel = (lane >= hd * xhd) & (lane < (hd + 1) * xhd)
        qh = jnp.where(sel, qx, jnp.zeros_like(qx))
        s = lax.dot_general(qh, km, (((1,), (1,)), ((), ())),
                            preferred_element_type=F32) * (xhd ** -0.5)
        m = jnp.max(s, axis=-1, keepdims=True)
        p = jnp.exp(s - m)
        p = p / jnp.sum(p, axis=-1, keepdims=True)
        o = jnp.dot(p.astype(BF16), vm, preferred_element_type=F32)
        y_x = jnp.where(sel, o, y_x)

    y = jnp.concatenate([
        _rms(ya_ref[0].astype(F32), g_attn_ref[...]).astype(BF16),
        _rms(y_conv, g_conv_ref[...]).astype(BF16),
        _rms(y_x, g_xattn_ref[...]).astype(BF16)], axis=-1)
    y = jnp.dot(y, w_out_ref[...], preferred_element_type=F32)
    x1 = x_ref[0] + _rms(y, g_post_mix_ref[...])

    h2 = _rms(x1, g_pre_mlp_ref[...]).astype(BF16)
    d_ff = w_up_ref.shape[1]
    acc = jnp.zeros_like(x1)
    for f0 in range(0, d_ff, ff_chunk):
        up = jnp.dot(h2, w_up_ref[:, f0:f0 + ff_chunk], preferred_element_type=F32)
        act = jnp.square(jnp.maximum(up, 0.0)).astype(BF16)
        acc = acc + jnp.dot(act, w_down_ref[f0:f0 + ff_chunk, :],
                            preferred_element_type=F32)
    o_ref[0] = x1 + _rms(acc, g_post_mlp_ref[...])


def _mix_mlp(x, y_attn, proj, mem, g_mem, w_mem, conv_w, g_attn, g_conv, g_xattn,
             w_out, g_post_mix, g_pre_mlp, w_up, w_down, g_post_mlp,
             *, attn_width, conv_width, xattn_width, tm=512, ff_chunk=1024):
    B, S, D = x.shape
    n_mem = mem.shape[1]
    d_ff = w_up.shape[1]
    assert conv_width == xattn_width and (3 * attn_width) % conv_width == 0
    cb0 = 3 * attn_width // conv_width
    halo = BF16_SUBLANES
    const = lambda shape: pl.BlockSpec(shape, lambda b, t: (0,) * len(shape),
                                       pipeline_mode=pl.Buffered(1))
    pcol = lambda cb: pl.BlockSpec((1, tm, conv_width), lambda b, t: (b, t, cb))
    phalo = lambda cb: pl.BlockSpec(
        (1, halo, conv_width),
        lambda b, t: (b, jnp.maximum(t * (tm // halo) - 1, 0), cb))
    return pl.pallas_call(
        functools.partial(_mix_mlp_kernel, ff_chunk=ff_chunk),
        grid=(B, S // tm),
        in_specs=[
            pl.BlockSpec((1, tm, D), lambda b, t: (b, t, 0)),
            pl.BlockSpec((1, tm, attn_width), lambda b, t: (b, t, 0)),
            pcol(cb0), pcol(cb0 + 1), pcol(cb0 + 2), pcol(cb0 + 3),
            phalo(cb0 + 1), phalo(cb0 + 2),
            pl.BlockSpec((1, n_mem, D), lambda b, t: (b, 0, 0)),
            const((1, D)), const((D, 2 * xattn_width)), const((CONV_K, conv_width)),
            const((1, attn_width)), const((1, conv_width)), const((1, xattn_width)),
            const((D, D)), const((1, D)),
            const((1, D)), const((D, d_ff)), const((d_ff, D)), const((1, D)),
        ],
        out_specs=pl.BlockSpec((1, tm, D), lambda b, t: (b, t, 0)),
        out_shape=jax.ShapeDtypeStruct((B, S, D), x.dtype),
        scratch_shapes=[pltpu.VMEM((n_mem, xattn_width), BF16),
                        pltpu.VMEM((n_mem, xattn_width), BF16)],
        compiler_params=pltpu.CompilerParams(
            dimension_semantics=("parallel", "arbitrary"),
            vmem_limit_bytes=VMEM_LIMIT_BYTES),
        name="mix_mlp",
    )(x, y_attn, proj, proj, proj, proj, proj, proj, mem, g_mem, w_mem, conv_w,
      g_attn, g_conv, g_xattn, w_out, g_post_mix, g_pre_mlp, w_up, w_down, g_post_mlp)


def kernel(x, mem, positions, g_pre_mix, g_mem, w_in, w_mem_kv, conv_w, g_attn_out,
           g_conv_out, g_xattn_out, w_out, g_post_mix, g_pre_mlp, w_up, w_down,
           g_post_mlp):
    depth = w_in.shape[0]
    attn_width = g_attn_out.shape[1]
    conv_width = g_conv_out.shape[1]
    xattn_width = g_xattn_out.shape[1]
    pos3 = positions[:, :, None]
    row = lambda g: g[None, :]
    for l in range(depth):
        proj = _in_proj(x, pos3, row(g_pre_mix[l]), w_in[l].astype(BF16),
                        attn_width=attn_width)
        y_attn = _dil_attn(proj, attn_width=attn_width)
        x = _mix_mlp(x, y_attn, proj, mem, row(g_mem[l]), w_mem_kv[l].astype(BF16),
                     conv_w[l], row(g_attn_out[l]), row(g_conv_out[l]),
                     row(g_xattn_out[l]), w_out[l].astype(BF16), row(g_post_mix[l]),
                     row(g_pre_mlp[l]), w_up[l].astype(BF16), w_down[l].astype(BF16),
                     row(g_post_mlp[l]),
                     attn_width=attn_width, conv_width=conv_width,
                     xattn_width=xattn_width)
    return x
```

```python
import functools

import jax
import jax.numpy as jnp
from jax import lax
from jax.experimental import pallas as pl
from jax.experimental.pallas import tpu as pltpu

F32 = jnp.float32
BF16 = jnp.bfloat16

HEAD_DIM = 64
N_MEM_HEADS = 4
DILATED_PATTERNS = ((128, 1), (512, 4), (2048, 16))
CONV_K = 3
ROPE_THETA = 10000.0
EPS = 1e-6
NEG_INF = -1e30
LOG2_E = 1.4426950408889634

LANES = 128
BF16_SUBLANES = 16
ATTN_BLK = 128
VMEM_LIMIT_BYTES = 56 * 1024 * 1024


def _rms(x, g):
    return x * lax.rsqrt(jnp.mean(x * x, axis=-1, keepdims=True) + EPS) * g


def _in_proj_kernel(x_ref, pos_ref, g_ref, w_ref, o_ref, *, attn_width, chunk):
    h = _rms(x_ref[0], g_ref[...]).astype(BF16)

    lane = lax.broadcasted_iota(jnp.int32, (1, LANES), 1)
    half = HEAD_DIM // 2
    first_half = (lane % HEAD_DIM) < half
    freq = (lane % half).astype(F32)
    inv_freq = jnp.float32(ROPE_THETA) ** (-(freq * 2.0 / HEAD_DIM))
    ang = pos_ref[0].astype(F32) * inv_freq
    cos = jnp.cos(ang)
    sin = jnp.where(first_half, -jnp.sin(ang), jnp.sin(ang))
    q_scale = HEAD_DIM ** -0.5 * LOG2_E

    n_out = o_ref.shape[-1]
    for c0 in range(0, n_out, chunk):
        p = jnp.dot(h, w_ref[:, c0:c0 + chunk], preferred_element_type=F32)
        if c0 < 2 * attn_width:
            scale = q_scale if c0 < attn_width else 1.0
            for g0 in range(0, chunk, LANES):
                t = p[:, g0:g0 + LANES]
                rot = jnp.where(first_half,
                                pltpu.roll(t, LANES - half, 1),
                                pltpu.roll(t, half, 1))
                r = (t * cos + rot * sin) * scale
                o_ref[0, :, c0 + g0:c0 + g0 + LANES] = r.astype(BF16)
        else:
            o_ref[0, :, c0:c0 + chunk] = p.astype(BF16)


def _in_proj(x, pos3, g, w_bf16, *, attn_width, tm=512, chunk=512):
    B, S, D = x.shape
    n_out = w_bf16.shape[1]
    return pl.pallas_call(
        functools.partial(_in_proj_kernel, attn_width=attn_width, chunk=chunk),
        grid=(B, S // tm),
        in_specs=[
            pl.BlockSpec((1, tm, D), lambda b, t: (b, t, 0)),
            pl.BlockSpec((1, tm, 1), lambda b, t: (b, t, 0)),
            pl.BlockSpec((1, D), lambda b, t: (0, 0)),
            pl.BlockSpec((D, n_out), lambda b, t: (0, 0)),
        ],
        out_specs=pl.BlockSpec((1, tm, n_out), lambda b, t: (b, t, 0)),
        out_shape=jax.ShapeDtypeStruct((B, S, n_out), BF16),
        compiler_params=pltpu.CompilerParams(
            dimension_semantics=("parallel", "parallel"),
            vmem_limit_bytes=VMEM_LIMIT_BYTES),
        name="in_proj",
    )(x, pos3, g, w_bf16)


def _band_block(qb, kb, vb1, bias, lane_lo):
    ms, ls, accs = [], [], []
    for head_lo in (True, False):
        sel = lane_lo if head_lo else jnp.logical_not(lane_lo)
        qh = jnp.where(sel, qb, jnp.zeros_like(qb))
        s = lax.dot_general(qh, kb, (((1,), (1,)), ((), ())),
                            preferred_element_type=F32) + bias
        m = jnp.max(s, axis=-1, keepdims=True)
        p = jnp.exp2(s - m).astype(BF16)
        r = jnp.dot(p, vb1, preferred_element_type=F32)
        accs.append(r[:, :LANES])
        ls.append(r[:, LANES:])
        ms.append(jnp.broadcast_to(m, (m.shape[0], LANES)))
    return (jnp.where(lane_lo, ms[0], ms[1]), jnp.where(lane_lo, ls[0], ls[1]),
            jnp.where(lane_lo, accs[0], accs[1]))


COPY_ROWS = 64
DEINTERLEAVE = 4


def _for_each_split(seq, region, body):
    run = region // DEINTERLEAVE
    span = DEINTERLEAVE * COPY_ROWS
    steps_per_region = region // span

    def step(t, carry):
        strided0 = pl.multiple_of(t * span, span)
        dense0 = pl.multiple_of((t // steps_per_region) * region
                                + (t % steps_per_region) * COPY_ROWS, COPY_ROWS)
        for j in range(DEINTERLEAVE):
            body(pl.ds(strided0 + j, COPY_ROWS, stride=DEINTERLEAVE),
                 pl.ds(dense0 + j * run, COPY_ROWS))
        return carry

    lax.fori_loop(0, seq // span, step, 0)


def _merge_softmax(dst, src, dst_rows, src_rows):
    (m_d, l_d, a_d), (m_s, l_s, a_s) = dst, src
    m_old, m_new = m_d[dst_rows, :], m_s[src_rows, :]
    m_max = jnp.maximum(m_old, m_new)
    w_old = jnp.exp2(m_old - m_max)
    w_new = jnp.exp2(m_new - m_max)
    m_d[dst_rows, :] = m_max
    l_d[dst_rows, :] = l_d[dst_rows, :] * w_old + l_s[src_rows, :] * w_new
    a_d[dst_rows, :] = a_d[dst_rows, :] * w_old + a_s[src_rows, :] * w_new


def _dil_attn_kernel(q_ref, k_ref, v_ref, o_ref,
                     qf, kf, vf, q4f, k4f, v4f, qp, kp, vp1,
                     m_run, l_run, a_run, m_p4, l_p4, a_p4, bias_ref,
                     *, seq, patterns, unroll):
    blk = ATTN_BLK
    n_blocks = seq // blk
    lane_lo = lax.broadcasted_iota(jnp.int32, (1, LANES), 1) < HEAD_DIM
    assert [d for _, d in patterns] == [1, DEINTERLEAVE, DEINTERLEAVE ** 2]
    assert all(w // d == blk for w, d in patterns)

    qi = lax.broadcasted_iota(jnp.int32, (blk, 2 * blk), 0)
    kj = lax.broadcasted_iota(jnp.int32, (blk, 2 * blk), 1)
    bias_ref[0] = jnp.where(kj <= qi, 0.0, NEG_INF).astype(F32)
    bias_ref[1] = jnp.where((kj >= qi) & (kj <= qi + blk), 0.0, NEG_INF).astype(F32)
    vp1[:, LANES:] = jnp.ones((seq, LANES), BF16)

    def attend(sub_len, q_src, k_src, dst):
        nb = sub_len // blk
        assert nb >= 2
        m_dst, l_dst, a_dst = dst

        def block_body(g, carry):
            n = g % nb
            q0 = pl.multiple_of(g * blk, blk)
            k0 = pl.multiple_of(jnp.where(n > 0, q0 - blk, q0), blk)
            m_b, l_b, acc = _band_block(q_src[pl.ds(q0, blk), :],
                                        k_src[pl.ds(k0, 2 * blk), :],
                                        vp1[pl.ds(k0, 2 * blk), :],
                                        bias_ref[jnp.minimum(n, 1)], lane_lo)
            m_dst[pl.ds(q0, blk), :] = m_b
            l_dst[pl.ds(q0, blk), :] = l_b
            a_dst[pl.ds(q0, blk), :] = acc
            return carry

        lax.fori_loop(0, n_blocks, block_body, 0, unroll=unroll)

    run = (m_run, l_run, a_run)
    pat4 = (m_p4, l_p4, a_p4)
    pat16 = (qf, kf, vf)

    vp1[:, :LANES] = v_ref[0]
    attend(seq, q_ref.at[0], k_ref.at[0], run)

    qf[...] = q_ref[0].astype(F32)
    kf[...] = k_ref[0].astype(F32)
    vf[...] = v_ref[0].astype(F32)
    def gather4(strided, dense):
        for src, dst_f, dst_b in ((qf, q4f, qp), (kf, k4f, kp)):
            rows = src[strided, :]
            dst_f[dense, :] = rows
            dst_b[dense, :] = rows.astype(BF16)
        rows = vf[strided, :]
        v4f[dense, :] = rows
        vp1[dense, :LANES] = rows.astype(BF16)

    _for_each_split(seq, seq, gather4)
    run4 = seq // DEINTERLEAVE
    attend(run4, qp, kp, pat4)

    def gather16(strided, dense):
        qp[dense, :] = q4f[strided, :].astype(BF16)
        kp[dense, :] = k4f[strided, :].astype(BF16)
        vp1[dense, :LANES] = v4f[strided, :].astype(BF16)

    _for_each_split(seq, run4, gather16)
    attend(run4 // DEINTERLEAVE, qp, kp, pat16)

    _for_each_split(seq, run4, functools.partial(_merge_softmax, pat4, pat16))
    _for_each_split(seq, seq, functools.partial(_merge_softmax, run, pat4))

    o_ref[0] = (a_run[...] / l_run[...]).astype(o_ref.dtype)


def _dil_attn(proj, *, attn_width, unroll=4):
    B, S, _ = proj.shape
    n_pairs = attn_width // LANES
    kern = functools.partial(_dil_attn_kernel, seq=S, patterns=DILATED_PATTERNS,
                             unroll=unroll)
    col = lambda off: (lambda b, hp: (b, 0, off + hp))
    f32_buf = pltpu.VMEM((S, LANES), F32)
    bf16_buf = pltpu.VMEM((S, LANES), BF16)
    return pl.pallas_call(
        kern,
        grid=(B, n_pairs),
        in_specs=[pl.BlockSpec((1, S, LANES), col(0)),
                  pl.BlockSpec((1, S, LANES), col(n_pairs)),
                  pl.BlockSpec((1, S, LANES), col(2 * n_pairs))],
        out_specs=pl.BlockSpec((1, S, LANES), lambda b, hp: (b, 0, hp)),
        out_shape=jax.ShapeDtypeStruct((B, S, attn_width), BF16),
        scratch_shapes=[f32_buf, f32_buf, f32_buf,
                        f32_buf, f32_buf, f32_buf,
                        bf16_buf, bf16_buf,
                        pltpu.VMEM((S, 2 * LANES), BF16),
                        f32_buf, f32_buf, f32_buf,
                        f32_buf, f32_buf, f32_buf,
                        pltpu.VMEM((2, ATTN_BLK, 2 * ATTN_BLK), F32)],
        compiler_params=pltpu.CompilerParams(
            dimension_semantics=("parallel", "parallel"),
            vmem_limit_bytes=VMEM_LIMIT_BYTES),
        name="dil_attn",
    )(proj, proj, proj)


def _mix_mlp_kernel(x_ref, ya_ref, bg_ref, cg_ref, u_ref, qx_ref, hc_ref, hu_ref,
                    mem_ref, g_mem_ref, w_mem_ref, conv_w_ref,
                    g_attn_ref, g_conv_ref, g_xattn_ref, w_out_ref, g_post_mix_ref,
                    g_pre_mlp_ref, w_up_ref, w_down_ref, g_post_mlp_ref,
                    o_ref, km_ref, vm_ref, *, ff_chunk):
    t = pl.program_id(1)
    xw = km_ref.shape[1]

    @pl.when(t == 0)
    def _():
        hm = _rms(mem_ref[0], g_mem_ref[...]).astype(BF16)
        kv = jnp.dot(hm, w_mem_ref[...], preferred_element_type=F32)
        km_ref[...] = kv[:, :xw].astype(BF16)
        vm_ref[...] = kv[:, xw:].astype(BF16)

    tm = x_ref.shape[1]

    z = cg_ref[0].astype(F32) * u_ref[0].astype(F32)
    hz = hc_ref[0].astype(F32) * hu_ref[0].astype(F32)
    hz = jnp.where(t > 0, hz, jnp.zeros_like(hz))
    z_ext = jnp.concatenate([hz, z], axis=0)
    cw = conv_w_ref[...]
    y_conv = z * cw[CONV_K - 1:CONV_K, :]
    for back in range(1, CONV_K):
        lo = BF16_SUBLANES - back
        y_conv = y_conv + z_ext[lo:lo + tm, :] * cw[CONV_K - 1 - back:CONV_K - back, :]
    y_conv = bg_ref[0].astype(F32) * y_conv

    qx = qx_ref[0]
    km = km_ref[...]
    vm = vm_ref[...]
    lane = lax.broadcasted_iota(jnp.int32, (1, xw), 1)
    xhd = xw // N_MEM_HEADS
    y_x = jnp.zeros((tm, xw), F32)
    for hd in range(N_MEM_HEADS):
        sel = (lane >= hd * xhd) & (lane < (hd + 1) * xhd)
        qh = jnp.where(sel, qx, jnp.zeros_like(qx))
        s = lax.dot_general(qh, km, (((1,), (1,)), ((), ())),
                            preferred_element_type=F32) * (xhd ** -0.5)
        m = jnp.max(s, axis=-1, keepdims=True)
        p = jnp.exp(s - m)
        p = p / jnp.sum(p, axis=-1, keepdims=True)
        o = jnp.dot(p.astype(BF16), vm, preferred_element_type=F32)
        y_x = jnp.where(sel, o, y_x)

    y = jnp.concatenate([
        _rms(ya_ref[0].astype(F32), g_attn_ref[...]).astype(BF16),
        _rms(y_conv, g_conv_ref[...]).astype(BF16),
        _rms(y_x, g_xattn_ref[...]).astype(BF16)], axis=-1)
    y = jnp.dot(y, w_out_ref[...], preferred_element_type=F32)
    x1 = x_ref[0] + _rms(y, g_post_mix_ref[...])

    h2 = _rms(x1, g_pre_mlp_ref[...]).astype(BF16)
    d_ff = w_up_ref.shape[1]
    acc = jnp.zeros_like(x1)
    for f0 in range(0, d_ff, ff_chunk):
        up = jnp.dot(h2, w_up_ref[:, f0:f0 + ff_chunk], preferred_element_type=F32)
        act = jnp.square(jnp.maximum(up, 0.0)).astype(BF16)
        acc = acc + jnp.dot(act, w_down_ref[f0:f0 + ff_chunk, :],
                            preferred_element_type=F32)
    o_ref[0] = x1 + _rms(acc, g_post_mlp_ref[...])


def _mix_mlp(x, y_attn, proj, mem, g_mem, w_mem, conv_w, g_attn, g_conv, g_xattn,
             w_out, g_post_mix, g_pre_mlp, w_up, w_down, g_post_mlp,
             *, attn_width, conv_width, xattn_width, tm=512, ff_chunk=1024):
    B, S, D = x.shape
    n_mem = mem.shape[1]
    d_ff = w_up.shape[1]
    assert conv_width == xattn_width and (3 * attn_width) % conv_width == 0
    cb0 = 3 * attn_width // conv_width
    halo = BF16_SUBLANES
    const = lambda shape: pl.BlockSpec(shape, lambda b, t: (0,) * len(shape),
                                       pipeline_mode=pl.Buffered(1))
    pcol = lambda cb: pl.BlockSpec((1, tm, conv_width), lambda b, t: (b, t, cb))
    phalo = lambda cb: pl.BlockSpec(
        (1, halo, conv_width),
        lambda b, t: (b, jnp.maximum(t * (tm // halo) - 1, 0), cb))
    return pl.pallas_call(
        functools.partial(_mix_mlp_kernel, ff_chunk=ff_chunk),
        grid=(B, S // tm),
        in_specs=[
            pl.BlockSpec((1, tm, D), lambda b, t: (b, t, 0)),
            pl.BlockSpec((1, tm, attn_width), lambda b, t: (b, t, 0)),
            pcol(cb0), pcol(cb0 + 1), pcol(cb0 + 2), pcol(cb0 + 3),
            phalo(cb0 + 1), phalo(cb0 + 2),
            pl.BlockSpec((1, n_mem, D), lambda b, t: (b, 0, 0)),
            const((1, D)), const((D, 2 * xattn_width)), const((CONV_K, conv_width)),
            const((1, attn_width)), const((1, conv_width)), const((1, xattn_width)),
            const((D, D)), const((1, D)),
            const((1, D)), const((D, d_ff)), const((d_ff, D)), const((1, D)),
        ],
        out_specs=pl.BlockSpec((1, tm, D), lambda b, t: (b, t, 0)),
        out_shape=jax.ShapeDtypeStruct((B, S, D), x.dtype),
        scratch_shapes=[pltpu.VMEM((n_mem, xattn_width), BF16),
                        pltpu.VMEM((n_mem, xattn_width), BF16)],
        compiler_params=pltpu.CompilerParams(
            dimension_semantics=("parallel", "arbitrary"),
            vmem_limit_bytes=VMEM_LIMIT_BYTES),
        name="mix_mlp",
    )(x, y_attn, proj, proj, proj, proj, proj, proj, mem, g_mem, w_mem, conv_w,
      g_attn, g_conv, g_xattn, w_out, g_post_mix, g_pre_mlp, w_up, w_down, g_post_mlp)


def kernel(x, mem, positions, g_pre_mix, g_mem, w_in, w_mem_kv, conv_w, g_attn_out,
           g_conv_out, g_xattn_out, w_out, g_post_mix, g_pre_mlp, w_up, w_down,
           g_post_mlp):
    depth = w_in.shape[0]
    attn_width = g_attn_out.shape[1]
    conv_width = g_conv_out.shape[1]
    xattn_width = g_xattn_out.shape[1]
    pos3 = positions[:, :, None]
    row = lambda g: g[None, :]
    for l in range(depth):
        proj = _in_proj(x, pos3, row(g_pre_mix[l]), w_in[l].astype(BF16),
                        attn_width=attn_width)
        y_attn = _dil_attn(proj, attn_width=attn_width)
        x = _mix_mlp(x, y_attn, proj, mem, row(g_mem[l]), w_mem_kv[l].astype(BF16),
                     conv_w[l], row(g_attn_out[l]), row(g_conv_out[l]),
                     row(g_xattn_out[l]), w_out[l].astype(BF16), row(g_post_mix[l]),
                     row(g_pre_mlp[l]), w_up[l].astype(BF16), w_down[l].astype(BF16),
                     row(g_post_mlp[l]),
                     attn_width=attn_width, conv_width=conv_width,
                     xattn_width=xattn_width)
    return x
```

```python
import functools

import jax
import jax.numpy as jnp
from jax import lax
from jax.experimental import pallas as pl
from jax.experimental.pallas import tpu as pltpu

F32 = jnp.float32
BF16 = jnp.bfloat16

HEAD_DIM = 64
N_MEM_HEADS = 4
DILATED_PATTERNS = ((128, 1), (512, 4), (2048, 16))
CONV_K = 3
ROPE_THETA = 10000.0
EPS = 1e-6
NEG_INF = -1e30
LOG2_E = 1.4426950408889634

LANES = 128
BF16_SUBLANES = 16
ATTN_BLK = 128
VMEM_LIMIT_BYTES = 56 * 1024 * 1024


def _rms(x, g):
    return x * lax.rsqrt(jnp.mean(x * x, axis=-1, keepdims=True) + EPS) * g


def _in_proj_kernel(x_ref, pos_ref, g_ref, w_ref, o_ref, *, attn_width, chunk):
    h = _rms(x_ref[0], g_ref[...]).astype(BF16)

    lane = lax.broadcasted_iota(jnp.int32, (1, LANES), 1)
    half = HEAD_DIM // 2
    first_half = (lane % HEAD_DIM) < half
    freq = (lane % half).astype(F32)
    inv_freq = jnp.float32(ROPE_THETA) ** (-(freq * 2.0 / HEAD_DIM))
    ang = pos_ref[0].astype(F32) * inv_freq
    cos = jnp.cos(ang)
    sin = jnp.where(first_half, -jnp.sin(ang), jnp.sin(ang))
    q_scale = HEAD_DIM ** -0.5 * LOG2_E

    n_out = o_ref.shape[-1]
    for c0 in range(0, n_out, chunk):
        p = jnp.dot(h, w_ref[:, c0:c0 + chunk], preferred_element_type=F32)
        if c0 < 2 * attn_width:
            scale = q_scale if c0 < attn_width else 1.0
            for g0 in range(0, chunk, LANES):
                t = p[:, g0:g0 + LANES]
                rot = jnp.where(first_half,
                                pltpu.roll(t, LANES - half, 1),
                                pltpu.roll(t, half, 1))
                r = (t * cos + rot * sin) * scale
                o_ref[0, :, c0 + g0:c0 + g0 + LANES] = r.astype(BF16)
        else:
            o_ref[0, :, c0:c0 + chunk] = p.astype(BF16)


def _in_proj(x, pos3, g, w_bf16, *, attn_width, tm=512, chunk=512):
    B, S, D = x.shape
    n_out = w_bf16.shape[1]
    return pl.pallas_call(
        functools.partial(_in_proj_kernel, attn_width=attn_width, chunk=chunk),
        grid=(B, S // tm),
        in_specs=[
            pl.BlockSpec((1, tm, D), lambda b, t: (b, t, 0)),
            pl.BlockSpec((1, tm, 1), lambda b, t: (b, t, 0)),
            pl.BlockSpec((1, D), lambda b, t: (0, 0)),
            pl.BlockSpec((D, n_out), lambda b, t: (0, 0)),
        ],
        out_specs=pl.BlockSpec((1, tm, n_out), lambda b, t: (b, t, 0)),
        out_shape=jax.ShapeDtypeStruct((B, S, n_out), BF16),
        compiler_params=pltpu.CompilerParams(
            dimension_semantics=("parallel", "parallel"),
            vmem_limit_bytes=VMEM_LIMIT_BYTES),
        name="in_proj",
    )(x, pos3, g, w_bf16)


def _band_block(qb, kb, vb1, bias, lane_lo):
    ms, ls, accs = [], [], []
    for head_lo in (True, False):
        sel = lane_lo if head_lo else jnp.logical_not(lane_lo)
        qh = jnp.where(sel, qb, jnp.zeros_like(qb))
        s = lax.dot_general(qh, kb, (((1,), (1,)), ((), ())),
                            preferred_element_type=F32) + bias
        m = jnp.max(s, axis=-1, keepdims=True)
        p = jnp.exp2(s - m).astype(BF16)
        r = jnp.dot(p, vb1, preferred_element_type=F32)
        accs.append(r[:, :LANES])
        ls.append(r[:, LANES:])
        ms.append(jnp.broadcast_to(m, (m.shape[0], LANES)))
    return (jnp.where(lane_lo, ms[0], ms[1]), jnp.where(lane_lo, ls[0], ls[1]),
            jnp.where(lane_lo, accs[0], accs[1]))


COPY_ROWS = 64
DEINTERLEAVE = 4


def _for_each_split(seq, region, body):
    run = region // DEINTERLEAVE
    span = DEINTERLEAVE * COPY_ROWS
    steps_per_region = region // span

    def step(t, carry):
        strided0 = pl.multiple_of(t * span, span)
        dense0 = pl.multiple_of((t // steps_per_region) * region
                                + (t % steps_per_region) * COPY_ROWS, COPY_ROWS)
        for j in range(DEINTERLEAVE):
            body(pl.ds(strided0 + j, COPY_ROWS, stride=DEINTERLEAVE),
                 pl.ds(dense0 + j * run, COPY_ROWS))
        return carry

    lax.fori_loop(0, seq // span, step, 0)


def _merge_softmax(dst, src, dst_rows, src_rows):
    (m_d, l_d, a_d), (m_s, l_s, a_s) = dst, src
    m_old, m_new = m_d[dst_rows, :], m_s[src_rows, :]
    m_max = jnp.maximum(m_old, m_new)
    w_old = jnp.exp2(m_old - m_max)
    w_new = jnp.exp2(m_new - m_max)
    m_d[dst_rows, :] = m_max
    l_d[dst_rows, :] = l_d[dst_rows, :] * w_old + l_s[src_rows, :] * w_new
    a_d[dst_rows, :] = a_d[dst_rows, :] * w_old + a_s[src_rows, :] * w_new


def _dil_attn_kernel(q_ref, k_ref, v_ref, o_ref,
                     qf, kf, vf, q4f, k4f, v4f, qp, kp, vp1,
                     m_run, l_run, a_run, m_p4, l_p4, a_p4, bias_ref,
                     *, seq, patterns, unroll):
    blk = ATTN_BLK
    n_blocks = seq // blk
    lane_lo = lax.broadcasted_iota(jnp.int32, (1, LANES), 1) < HEAD_DIM
    assert [d for _, d in patterns] == [1, DEINTERLEAVE, DEINTERLEAVE ** 2]
    assert all(w // d == blk for w, d in patterns)

    qi = lax.broadcasted_iota(jnp.int32, (blk, 2 * blk), 0)
    kj = lax.broadcasted_iota(jnp.int32, (blk, 2 * blk), 1)
    bias_ref[0] = jnp.where(kj <= qi, 0.0, NEG_INF).astype(F32)
    bias_ref[1] = jnp.where((kj >= qi) & (kj <= qi + blk), 0.0, NEG_INF).astype(F32)
    vp1[:, LANES:] = jnp.ones((seq, LANES), BF16)

    def attend(sub_len, q_src, k_src, dst):
        nb = sub_len // blk
        assert nb >= 2
        m_dst, l_dst, a_dst = dst

        def block_body(g, carry):
            n = g % nb
            q0 = pl.multiple_of(g * blk, blk)
            k0 = pl.multiple_of(jnp.where(n > 0, q0 - blk, q0), blk)
            m_b, l_b, acc = _band_block(q_src[pl.ds(q0, blk), :],
                                        k_src[pl.ds(k0, 2 * blk), :],
                                        vp1[pl.ds(k0, 2 * blk), :],
                                        bias_ref[jnp.minimum(n, 1)], lane_lo)
            m_dst[pl.ds(q0, blk), :] = m_b
            l_dst[pl.ds(q0, blk), :] = l_b
            a_dst[pl.ds(q0, blk), :] = acc
            return carry

        lax.fori_loop(0, n_blocks, block_body, 0, unroll=unroll)

    run = (m_run, l_run, a_run)
    pat4 = (m_p4, l_p4, a_p4)
    pat16 = (qf, kf, vf)

    vp1[:, :LANES] = v_ref[0]
    attend(seq, q_ref.at[0], k_ref.at[0], run)

    qf[...] = q_ref[0].astype(F32)
    kf[...] = k_ref[0].astype(F32)
    vf[...] = v_ref[0].astype(F32)
    def gather4(strided, dense):
        for src, dst_f, dst_b in ((qf, q4f, qp), (kf, k4f, kp)):
            rows = src[strided, :]
            dst_f[dense, :] = rows
            dst_b[dense, :] = rows.astype(BF16)
        rows = vf[strided, :]
        v4f[dense, :] = rows
        vp1[dense, :LANES] = rows.astype(BF16)

    _for_each_split(seq, seq, gather4)
    run4 = seq // DEINTERLEAVE
    attend(run4, qp, kp, pat4)

    def gather16(strided, dense):
        qp[dense, :] = q4f[strided, :].astype(BF16)
        kp[dense, :] = k4f[strided, :].astype(BF16)
        vp1[dense, :LANES] = v4f[strided, :].astype(BF16)

    _for_each_split(seq, run4, gather16)
    attend(run4 // DEINTERLEAVE, qp, kp, pat16)

    _for_each_split(seq, run4, functools.partial(_merge_softmax, pat4, pat16))
    _for_each_split(seq, seq, functools.partial(_merge_softmax, run, pat4))

    o_ref[0] = (a_run[...] / l_run[...]).astype(o_ref.dtype)


def _dil_attn(proj, *, attn_width, unroll=16):
    B, S, _ = proj.shape
    n_pairs = attn_width // LANES
    kern = functools.partial(_dil_attn_kernel, seq=S, patterns=DILATED_PATTERNS,
                             unroll=unroll)
    col = lambda off: (lambda b, hp: (b, 0, off + hp))
    f32_buf = pltpu.VMEM((S, LANES), F32)
    bf16_buf = pltpu.VMEM((S, LANES), BF16)
    return pl.pallas_call(
        kern,
        grid=(B, n_pairs),
        in_specs=[pl.BlockSpec((1, S, LANES), col(0)),
                  pl.BlockSpec((1, S, LANES), col(n_pairs)),
                  pl.BlockSpec((1, S, LANES), col(2 * n_pairs))],
        out_specs=pl.BlockSpec((1, S, LANES), lambda b, hp: (b, 0, hp)),
        out_shape=jax.ShapeDtypeStruct((B, S, attn_width), BF16),
        scratch_shapes=[f32_buf, f32_buf, f32_buf,
                        f32_buf, f32_buf, f32_buf,
                        bf16_buf, bf16_buf,
                        pltpu.VMEM((S, 2 * LANES), BF16),
                        f32_buf, f32_buf, f32_buf,
                        f32_buf, f32_buf, f32_buf,
                        pltpu.VMEM((2, ATTN_BLK, 2 * ATTN_BLK), F32)],
        compiler_params=pltpu.CompilerParams(
            dimension_semantics=("parallel", "parallel"),
            vmem_limit_bytes=VMEM_LIMIT_BYTES),
        name="dil_attn",
    )(proj, proj, proj)


def _mix_mlp_kernel(x_ref, ya_ref, bg_ref, cg_ref, u_ref, qx_ref, hc_ref, hu_ref,
                    mem_ref, g_mem_ref, w_mem_ref, conv_w_ref,
                    g_attn_ref, g_conv_ref, g_xattn_ref, w_out_ref, g_post_mix_ref,
                    g_pre_mlp_ref, w_up_ref, w_down_ref, g_post_mlp_ref,
                    o_ref, km_ref, vm_ref, *, ff_chunk):
    t = pl.program_id(1)
    xw = km_ref.shape[1]

    @pl.when(t == 0)
    def _():
        hm = _rms(mem_ref[0], g_mem_ref[...]).astype(BF16)
        kv = jnp.dot(hm, w_mem_ref[...], preferred_element_type=F32)
        km_ref[...] = kv[:, :xw].astype(BF16)
        vm_ref[...] = kv[:, xw:].astype(BF16)

    tm = x_ref.shape[1]

    z = cg_ref[0].astype(F32) * u_ref[0].astype(F32)
    hz = hc_ref[0].astype(F32) * hu_ref[0].astype(F32)
    hz = jnp.where(t > 0, hz, jnp.zeros_like(hz))
    z_ext = jnp.concatenate([hz, z], axis=0)
    cw = conv_w_ref[...]
    y_conv = z * cw[CONV_K - 1:CONV_K, :]
    for back in range(1, CONV_K):
        lo = BF16_SUBLANES - back
        y_conv = y_conv + z_ext[lo:lo + tm, :] * cw[CONV_K - 1 - back:CONV_K - back, :]
    y_conv = bg_ref[0].astype(F32) * y_conv

    qx = qx_ref[0]
    km = km_ref[...]
    vm = vm_ref[...]
    lane = lax.broadcasted_iota(jnp.int32, (1, xw), 1)
    xhd = xw // N_MEM_HEADS
    y_x = jnp.zeros((tm, xw), F32)
    for hd in range(N_MEM_HEADS):
        sel = (lane >= hd * xhd) & (lane < (hd + 1) * xhd)
        qh = jnp.where(sel, qx, jnp.zeros_like(qx))
        s = lax.dot_general(qh, km, (((1,), (1,)), ((), ())),
                            preferred_element_type=F32) * (xhd ** -0.5)
        m = jnp.max(s, axis=-1, keepdims=True)
        p = jnp.exp(s - m)
        p = p / jnp.sum(p, axis=-1, keepdims=True)
        o = jnp.dot(p.astype(BF16), vm, preferred_element_type=F32)
        y_x = jnp.where(sel, o, y_x)

    y = jnp.concatenate([
        _rms(ya_ref[0].astype(F32), g_attn_ref[...]).astype(BF16),
        _rms(y_conv, g_conv_ref[...]).astype(BF16),
        _rms(y_x, g_xattn_ref[...]).astype(BF16)], axis=-1)
    y = jnp.dot(y, w_out_ref[...], preferred_element_type=F32)
    x1 = x_ref[0] + _rms(y, g_post_mix_ref[...])

    h2 = _rms(x1, g_pre_mlp_ref[...]).astype(BF16)
    d_ff = w_up_ref.shape[1]
    acc = jnp.zeros_like(x1)
    for f0 in range(0, d_ff, ff_chunk):
        up = jnp.dot(h2, w_up_ref[:, f0:f0 + ff_chunk], preferred_element_type=F32)
        act = jnp.square(jnp.maximum(up, 0.0)).astype(BF16)
        acc = acc + jnp.dot(act, w_down_ref[f0:f0 + ff_chunk, :],
                            preferred_element_type=F32)
    o_ref[0] = x1 + _rms(acc, g_post_mlp_ref[...])


def _mix_mlp(x, y_attn, proj, mem, g_mem, w_mem, conv_w, g_attn, g_conv, g_xattn,
             w_out, g_post_mix, g_pre_mlp, w_up, w_down, g_post_mlp,
             *, attn_width, conv_width, xattn_width, tm=512, ff_chunk=1024):
    B, S, D = x.shape
    n_mem = mem.shape[1]
    d_ff = w_up.shape[1]
    assert conv_width == xattn_width and (3 * attn_width) % conv_width == 0
    cb0 = 3 * attn_width // conv_width
    halo = BF16_SUBLANES
    const = lambda shape: pl.BlockSpec(shape, lambda b, t: (0,) * len(shape),
                                       pipeline_mode=pl.Buffered(1))
    pcol = lambda cb: pl.BlockSpec((1, tm, conv_width), lambda b, t: (b, t, cb))
    phalo = lambda cb: pl.BlockSpec(
        (1, halo, conv_width),
        lambda b, t: (b, jnp.maximum(t * (tm // halo) - 1, 0), cb))
    return pl.pallas_call(
        functools.partial(_mix_mlp_kernel, ff_chunk=ff_chunk),
        grid=(B, S // tm),
        in_specs=[
            pl.BlockSpec((1, tm, D), lambda b, t: (b, t, 0)),
            pl.BlockSpec((1, tm, attn_width), lambda b, t: (b, t, 0)),
            pcol(cb0), pcol(cb0 + 1), pcol(cb0 + 2), pcol(cb0 + 3),
            phalo(cb0 + 1), phalo(cb0 + 2),
            pl.BlockSpec((1, n_mem, D), lambda b, t: (b, 0, 0)),
            const((1, D)), const((D, 2 * xattn_width)), const((CONV_K, conv_width)),
            const((1, attn_width)), const((1, conv_width)), const((1, xattn_width)),
            const((D, D)), const((1, D)),
            const((1, D)), const((D, d_ff)), const((d_ff, D)), const((1, D)),
        ],
        out_specs=pl.BlockSpec((1, tm, D), lambda b, t: (b, t, 0)),
        out_shape=jax.ShapeDtypeStruct((B, S, D), x.dtype),
        scratch_shapes=[pltpu.VMEM((n_mem, xattn_width), BF16),
                        pltpu.VMEM((n_mem, xattn_width), BF16)],
        compiler_params=pltpu.CompilerParams(
            dimension_semantics=("parallel", "arbitrary"),
            vmem_limit_bytes=VMEM_LIMIT_BYTES),
        name="mix_mlp",
    )(x, y_attn, proj, proj, proj, proj, proj, proj, mem, g_mem, w_mem, conv_w,
      g_attn, g_conv, g_xattn, w_out, g_post_mix, g_pre_mlp, w_up, w_down, g_post_mlp)


def kernel(x, mem, positions, g_pre_mix, g_mem, w_in, w_mem_kv, conv_w, g_attn_out,
           g_conv_out, g_xattn_out, w_out, g_post_mix, g_pre_mlp, w_up, w_down,
           g_post_mlp):
    depth = w_in.shape[0]
    attn_width = g_attn_out.shape[1]
    conv_width = g_conv_out.shape[1]
    xattn_width = g_xattn_out.shape[1]
    pos3 = positions[:, :, None]
    row = lambda g: g[None, :]
    for l in range(depth):
        proj = _in_proj(x, pos3, row(g_pre_mix[l]), w_in[l].astype(BF16),
                        attn_width=attn_width)
        y_attn = _dil_attn(proj, attn_width=attn_width)
        x = _mix_mlp(x, y_attn, proj, mem, row(g_mem[l]), w_mem_kv[l].astype(BF16),
                     conv_w[l], row(g_attn_out[l]), row(g_conv_out[l]),
                     row(g_xattn_out[l]), w_out[l].astype(BF16), row(g_post_mix[l]),
                     row(g_pre_mlp[l]), w_up[l].astype(BF16), w_down[l].astype(BF16),
                     row(g_post_mlp[l]),
                     attn_width=attn_width, conv_width=conv_width,
                     xattn_width=xattn_width)
    return x
```

```python
import functools

import jax
import jax.numpy as jnp
from jax import lax
from jax.experimental import pallas as pl
from jax.experimental.pallas import tpu as pltpu

F32 = jnp.float32
BF16 = jnp.bfloat16

HEAD_DIM = 64
N_MEM_HEADS = 4
DILATED_PATTERNS = ((128, 1), (512, 4), (2048, 16))
CONV_K = 3
ROPE_THETA = 10000.0
EPS = 1e-6
NEG_INF = -1e30
LOG2_E = 1.4426950408889634

LANES = 128
BF16_SUBLANES = 16
ATTN_BLK = 128
VMEM_LIMIT_BYTES = 56 * 1024 * 1024


def _rms(x, g):
    return x * lax.rsqrt(jnp.mean(x * x, axis=-1, keepdims=True) + EPS) * g


def _in_proj_kernel(x_ref, pos_ref, g_ref, w_ref, o_ref, *, attn_width, chunk):
    h = _rms(x_ref[0], g_ref[...]).astype(BF16)

    half = HEAD_DIM // 2
    freq = lax.broadcasted_iota(jnp.int32, (half, 1), 0).astype(F32)
    inv_freq = jnp.float32(ROPE_THETA) ** (-(freq * 2.0 / HEAD_DIM))
    ang = inv_freq * pos_ref[0].astype(F32)
    cos_t, sin_t = jnp.cos(ang), jnp.sin(ang)
    reps = LANES // HEAD_DIM
    cos_k = jnp.concatenate([cos_t, cos_t] * reps, axis=0).T
    sin_k = jnp.concatenate([-sin_t, sin_t] * reps, axis=0).T
    q_scale = HEAD_DIM ** -0.5 * LOG2_E
    cos_q, sin_q = cos_k * q_scale, sin_k * q_scale
    lane = lax.broadcasted_iota(jnp.int32, (1, LANES), 1)
    first_half = (lane % HEAD_DIM) < half

    n_out = o_ref.shape[-1]
    for c0 in range(0, n_out, chunk):
        p = jnp.dot(h, w_ref[:, c0:c0 + chunk], preferred_element_type=F32)
        if c0 < 2 * attn_width:
            cos, sin = (cos_q, sin_q) if c0 < attn_width else (cos_k, sin_k)
            for g0 in range(0, chunk, LANES):
                t = p[:, g0:g0 + LANES]
                rot = jnp.where(first_half,
                                pltpu.roll(t, LANES - half, 1),
                                pltpu.roll(t, half, 1))
                r = t * cos + rot * sin
                o_ref[0, :, c0 + g0:c0 + g0 + LANES] = r.astype(BF16)
        else:
            o_ref[0, :, c0:c0 + chunk] = p.astype(BF16)


def _in_proj(x, pos3, g, w_bf16, *, attn_width, tm=512, chunk=512):
    B, S, D = x.shape
    n_out = w_bf16.shape[1]
    return pl.pallas_call(
        functools.partial(_in_proj_kernel, attn_width=attn_width, chunk=chunk),
        grid=(B, S // tm),
        in_specs=[
            pl.BlockSpec((1, tm, D), lambda b, t: (b, t, 0)),
            pl.BlockSpec((1, 1, tm), lambda b, t: (b, 0, t)),
            pl.BlockSpec((1, D), lambda b, t: (0, 0)),
            pl.BlockSpec((D, n_out), lambda b, t: (0, 0)),
        ],
        out_specs=pl.BlockSpec((1, tm, n_out), lambda b, t: (b, t, 0)),
        out_shape=jax.ShapeDtypeStruct((B, S, n_out), BF16),
        compiler_params=pltpu.CompilerParams(
            dimension_semantics=("parallel", "parallel"),
            vmem_limit_bytes=VMEM_LIMIT_BYTES),
        name="in_proj",
    )(x, pos3, g, w_bf16)


def _band_block(qb, kb, vb1, bias, lane_lo):
    ms, ls, accs = [], [], []
    for head_lo in (True, False):
        sel = lane_lo if head_lo else jnp.logical_not(lane_lo)
        qh = jnp.where(sel, qb, jnp.zeros_like(qb))
        s = lax.dot_general(qh, kb, (((1,), (1,)), ((), ())),
                            preferred_element_type=F32) + bias
        m = jnp.max(s, axis=-1, keepdims=True)
        p = jnp.exp2(s - m).astype(BF16)
        r = jnp.dot(p, vb1, preferred_element_type=F32)
        accs.append(r[:, :LANES])
        ls.append(r[:, LANES:])
        ms.append(jnp.broadcast_to(m, (m.shape[0], LANES)))
    return (jnp.where(lane_lo, ms[0], ms[1]), jnp.where(lane_lo, ls[0], ls[1]),
            jnp.where(lane_lo, accs[0], accs[1]))


COPY_ROWS = 64
DEINTERLEAVE = 4


def _for_each_split(seq, region, body):
    run = region // DEINTERLEAVE
    span = DEINTERLEAVE * COPY_ROWS
    steps_per_region = region // span

    def step(t, carry):
        strided0 = pl.multiple_of(t * span, span)
        dense0 = pl.multiple_of((t // steps_per_region) * region
                                + (t % steps_per_region) * COPY_ROWS, COPY_ROWS)
        for j in range(DEINTERLEAVE):
            body(pl.ds(strided0 + j, COPY_ROWS, stride=DEINTERLEAVE),
                 pl.ds(dense0 + j * run, COPY_ROWS))
        return carry

    lax.fori_loop(0, seq // span, step, 0)


def _merge_softmax(a, b):
    (m_a, l_a, a_a), (m_b, l_b, a_b) = a, b
    m = jnp.maximum(m_a, m_b)
    w_a = jnp.exp2(m_a - m)
    w_b = jnp.exp2(m_b - m)
    return m, l_a * w_a + l_b * w_b, a_a * w_a + a_b * w_b


def _dil_attn_kernel(q_ref, k_ref, v_ref, o_ref,
                     qf, kf, vf, q4f, k4f, v4f, qp, kp, vp1,
                     m_p4, l_p4, a_p4, m_tmp, l_tmp, a_tmp, bias_ref,
                     *, seq, patterns, unroll):
    blk = ATTN_BLK
    n_blocks = seq // blk
    lane_lo = lax.broadcasted_iota(jnp.int32, (1, LANES), 1) < HEAD_DIM
    assert [d for _, d in patterns] == [1, DEINTERLEAVE, DEINTERLEAVE ** 2]
    assert all(w // d == blk for w, d in patterns)
    run4 = seq // DEINTERLEAVE
    run16 = run4 // DEINTERLEAVE
    blocks_per_run4 = run4 // blk
    blocks_per_run16 = run16 // blk
    assert n_blocks % unroll == 0 and unroll % blocks_per_run4 == 0
    assert n_blocks // unroll >= 2

    qi = lax.broadcasted_iota(jnp.int32, (blk, 2 * blk), 0)
    kj = lax.broadcasted_iota(jnp.int32, (blk, 2 * blk), 1)
    bias_ref[0] = jnp.where(kj <= qi, 0.0, NEG_INF).astype(F32)
    bias_ref[1] = jnp.where((kj >= qi) & (kj <= qi + blk), 0.0, NEG_INF).astype(F32)
    vp1[:, LANES:] = jnp.ones((seq, LANES), BF16)

    pat4 = (m_p4, l_p4, a_p4)
    tmp = (m_tmp, l_tmp, a_tmp)

    def attend(sub_len, q_src, k_src, finish):
        nb = sub_len // blk
        assert nb >= 2

        def trip_body(trip, carry):
            for u in range(unroll):
                g = trip * unroll + u
                n = g % nb
                q0 = pl.multiple_of(g * blk, blk)
                k0 = pl.multiple_of(jnp.where(n > 0, q0 - blk, q0), blk)
                triple = _band_block(q_src[pl.ds(q0, blk), :],
                                     k_src[pl.ds(k0, 2 * blk), :],
                                     vp1[pl.ds(k0, 2 * blk), :],
                                     bias_ref[jnp.minimum(n, 1)], lane_lo)
                finish(trip, u, q0, triple)
            return carry

        lax.fori_loop(0, n_blocks // unroll, trip_body, 0)

    qf[...] = q_ref[0].astype(F32)
    kf[...] = k_ref[0].astype(F32)
    vf[...] = v_ref[0].astype(F32)

    def gather4(strided, dense):
        for src, dst_f, dst_b in ((qf, q4f, qp), (kf, k4f, kp)):
            rows = src[strided, :]
            dst_f[dense, :] = rows
            dst_b[dense, :] = rows.astype(BF16)
        rows = vf[strided, :]
        v4f[dense, :] = rows
        vp1[dense, :LANES] = rows.astype(BF16)

    def store_pat4(trip, u, q0, triple):
        for ref, val in zip(pat4, triple):
            ref[pl.ds(q0, blk), :] = val

    _for_each_split(seq, seq, gather4)
    attend(run4, qp, kp, store_pat4)

    def gather16(strided, dense):
        qp[dense, :] = q4f[strided, :].astype(BF16)
        kp[dense, :] = k4f[strided, :].astype(BF16)
        vp1[dense, :LANES] = v4f[strided, :].astype(BF16)

    def fold_into_pat4(trip, u, q0, triple):
        r_static, idx = divmod(u, blocks_per_run4)
        j, a0 = idx // blocks_per_run16, (idx % blocks_per_run16) * blk
        r_dyn = pl.multiple_of(trip * (unroll // blocks_per_run4) * run4, run4)
        rows = pl.ds(r_dyn + r_static * run4 + j + DEINTERLEAVE * a0, blk,
                     stride=DEINTERLEAVE)
        merged = _merge_softmax(tuple(ref[rows, :] for ref in pat4), triple)
        for ref, val in zip(pat4, merged):
            ref[rows, :] = val

    _for_each_split(seq, run4, gather16)
    attend(run16, qp, kp, fold_into_pat4)

    def finish_natural(trip, u, q0, triple):
        sub = blk // DEINTERLEAVE
        i_dyn = pl.multiple_of(trip * (unroll * sub), unroll * sub)
        for j in range(DEINTERLEAVE):
            src = pl.ds(i_dyn + j * run4 + u * sub, sub)
            for t_ref, p_ref in zip(tmp, pat4):
                t_ref[u, pl.ds(j, sub, stride=DEINTERLEAVE), :] = p_ref[src, :]
        _, l, acc = _merge_softmax(tuple(t_ref[u] for t_ref in tmp), triple)
        o_ref[0, pl.ds(q0, blk), :] = (acc / l).astype(o_ref.dtype)

    vp1[:, :LANES] = v_ref[0]
    attend(seq, q_ref.at[0], k_ref.at[0], finish_natural)


def _dil_attn(proj, *, attn_width, unroll=16):
    B, S, _ = proj.shape
    n_pairs = attn_width // LANES
    kern = functools.partial(_dil_attn_kernel, seq=S, patterns=DILATED_PATTERNS,
                             unroll=unroll)
    col = lambda off: (lambda b, hp: (b, 0, off + hp))
    f32_buf = pltpu.VMEM((S, LANES), F32)
    bf16_buf = pltpu.VMEM((S, LANES), BF16)
    tmp_buf = pltpu.VMEM((unroll, ATTN_BLK, LANES), F32)
    return pl.pallas_call(
        kern,
        grid=(B, n_pairs),
        in_specs=[pl.BlockSpec((1, S, LANES), col(0)),
                  pl.BlockSpec((1, S, LANES), col(n_pairs)),
                  pl.BlockSpec((1, S, LANES), col(2 * n_pairs))],
        out_specs=pl.BlockSpec((1, S, LANES), lambda b, hp: (b, 0, hp)),
        out_shape=jax.ShapeDtypeStruct((B, S, attn_width), BF16),
        scratch_shapes=[f32_buf, f32_buf, f32_buf,
                        f32_buf, f32_buf, f32_buf,
                        bf16_buf, bf16_buf,
                        pltpu.VMEM((S, 2 * LANES), BF16),
                        f32_buf, f32_buf, f32_buf,
                        tmp_buf, tmp_buf, tmp_buf,
                        pltpu.VMEM((2, ATTN_BLK, 2 * ATTN_BLK), F32)],
        compiler_params=pltpu.CompilerParams(
            dimension_semantics=("parallel", "parallel"),
            vmem_limit_bytes=VMEM_LIMIT_BYTES),
        name="dil_attn",
    )(proj, proj, proj)


def _mix_mlp_kernel(x_ref, ya_ref, bg_ref, cg_ref, u_ref, qx_ref, hc_ref, hu_ref,
                    mem_ref, g_mem_ref, w_mem_ref, conv_w_ref,
                    g_attn_ref, g_conv_ref, g_xattn_ref, w_out_ref, g_post_mix_ref,
                    g_pre_mlp_ref, w_up_ref, w_down_ref, g_post_mlp_ref,
                    o_ref, km_ref, vm_ref, *, ff_chunk):
    t = pl.program_id(1)
    xw = km_ref.shape[1]

    @pl.when(t == 0)
    def _():
        hm = _rms(mem_ref[0], g_mem_ref[...]).astype(BF16)
        kv = jnp.dot(hm, w_mem_ref[...], preferred_element_type=F32)
        km_ref[...] = kv[:, :xw].astype(BF16)
        vm_ref[...] = kv[:, xw:].astype(BF16)

    tm = x_ref.shape[1]

    z = cg_ref[0].astype(F32) * u_ref[0].astype(F32)
    hz = hc_ref[0].astype(F32) * hu_ref[0].astype(F32)
    hz = jnp.where(t > 0, hz, jnp.zeros_like(hz))
    z_ext = jnp.concatenate([hz, z], axis=0)
    cw = conv_w_ref[...]
    y_conv = z * cw[CONV_K - 1:CONV_K, :]
    for back in range(1, CONV_K):
        lo = BF16_SUBLANES - back
        y_conv = y_conv + z_ext[lo:lo + tm, :] * cw[CONV_K - 1 - back:CONV_K - back, :]
    y_conv = bg_ref[0].astype(F32) * y_conv

    qx = qx_ref[0]
    km = km_ref[...]
    vm = vm_ref[...]
    lane = lax.broadcasted_iota(jnp.int32, (1, xw), 1)
    xhd = xw // N_MEM_HEADS
    y_x = jnp.zeros((tm, xw), F32)
    for hd in range(N_MEM_HEADS):
        sel = (lane >= hd * xhd) & (lane < (hd + 1) * xhd)
        qh = jnp.where(sel, qx, jnp.zeros_like(qx))
        sc = lax.dot_general(qh, km, (((1,), (1,)), ((), ())),
                             preferred_element_type=F32) * (xhd ** -0.5)
        m = jnp.max(sc, axis=-1, keepdims=True)
        p = jnp.exp(sc - m)
        p = p / jnp.sum(p, axis=-1, keepdims=True)
        o = jnp.dot(p.astype(BF16), vm, preferred_element_type=F32)
        y_x = jnp.where(sel, o, y_x)

    y = jnp.concatenate([
        _rms(ya_ref[0].astype(F32), g_attn_ref[...]).astype(BF16),
        _rms(y_conv, g_conv_ref[...]).astype(BF16),
        _rms(y_x, g_xattn_ref[...]).astype(BF16)], axis=-1)
    y = jnp.dot(y, w_out_ref[...], preferred_element_type=F32)
    x1 = x_ref[0] + _rms(y, g_post_mix_ref[...])

    h2 = _rms(x1, g_pre_mlp_ref[...]).astype(BF16)
    d_ff = w_up_ref.shape[1]
    acc = jnp.zeros_like(x1)
    for f0 in range(0, d_ff, ff_chunk):
        up = jnp.dot(h2, w_up_ref[:, f0:f0 + ff_chunk], preferred_element_type=F32)
        act = jnp.square(jnp.maximum(up, 0.0)).astype(BF16)
        acc = acc + jnp.dot(act, w_down_ref[f0:f0 + ff_chunk, :],
                            preferred_element_type=F32)
    o_ref[0] = x1 + _rms(acc, g_post_mlp_ref[...])


def _mix_mlp(x, y_attn, proj, mem, g_mem, w_mem, conv_w, g_attn, g_conv, g_xattn,
             w_out, g_post_mix, g_pre_mlp, w_up, w_down, g_post_mlp,
             *, attn_width, conv_width, xattn_width, tm=512, ff_chunk=1024):
    B, S, D = x.shape
    n_mem = mem.shape[1]
    d_ff = w_up.shape[1]
    assert conv_width == xattn_width and (3 * attn_width) % conv_width == 0
    cb0 = 3 * attn_width // conv_width
    halo = BF16_SUBLANES
    const = lambda shape: pl.BlockSpec(shape, lambda b, t: (0,) * len(shape),
                                       pipeline_mode=pl.Buffered(1))
    pcol = lambda cb: pl.BlockSpec((1, tm, conv_width), lambda b, t: (b, t, cb))
    phalo = lambda cb: pl.BlockSpec(
        (1, halo, conv_width),
        lambda b, t: (b, jnp.maximum(t * (tm // halo) - 1, 0), cb))
    return pl.pallas_call(
        functools.partial(_mix_mlp_kernel, ff_chunk=ff_chunk),
        grid=(B, S // tm),
        in_specs=[
            pl.BlockSpec((1, tm, D), lambda b, t: (b, t, 0)),
            pl.BlockSpec((1, tm, attn_width), lambda b, t: (b, t, 0)),
            pcol(cb0), pcol(cb0 + 1), pcol(cb0 + 2), pcol(cb0 + 3),
            phalo(cb0 + 1), phalo(cb0 + 2),
            pl.BlockSpec((1, n_mem, D), lambda b, t: (b, 0, 0)),
            const((1, D)), const((D, 2 * xattn_width)), const((CONV_K, conv_width)),
            const((1, attn_width)), const((1, conv_width)), const((1, xattn_width)),
            const((D, D)), const((1, D)),
            const((1, D)), const((D, d_ff)), const((d_ff, D)), const((1, D)),
        ],
        out_specs=pl.BlockSpec((1, tm, D), lambda b, t: (b, t, 0)),
        out_shape=jax.ShapeDtypeStruct((B, S, D), x.dtype),
        scratch_shapes=[pltpu.VMEM((n_mem, xattn_width), BF16),
                        pltpu.VMEM((n_mem, xattn_width), BF16)],
        compiler_params=pltpu.CompilerParams(
            dimension_semantics=("parallel", "arbitrary"),
            vmem_limit_bytes=VMEM_LIMIT_BYTES),
        name="mix_mlp",
    )(x, y_attn, proj, proj, proj, proj, proj, proj, mem, g_mem, w_mem, conv_w,
      g_attn, g_conv, g_xattn, w_out, g_post_mix, g_pre_mlp, w_up, w_down, g_post_mlp)


def kernel(x, mem, positions, g_pre_mix, g_mem, w_in, w_mem_kv, conv_w, g_attn_out,
           g_conv_out, g_xattn_out, w_out, g_post_mix, g_pre_mlp, w_up, w_down,
           g_post_mlp):
    depth = w_in.shape[0]
    attn_width = g_attn_out.shape[1]
    conv_width = g_conv_out.shape[1]
    xattn_width = g_xattn_out.shape[1]
    pos3 = positions[:, None, :]
    row = lambda g: g[None, :]
    for l in range(depth):
        proj = _in_proj(x, pos3, row(g_pre_mix[l]), w_in[l].astype(BF16),
                        attn_width=attn_width)
        y_attn = _dil_attn(proj, attn_width=attn_width)
        x = _mix_mlp(x, y_attn, proj, mem, row(g_mem[l]), w_mem_kv[l].astype(BF16),
                     conv_w[l], row(g_attn_out[l]), row(g_conv_out[l]),
                     row(g_xattn_out[l]), w_out[l].astype(BF16), row(g_post_mix[l]),
                     row(g_pre_mlp[l]), w_up[l].astype(BF16), w_down[l].astype(BF16),
                     row(g_post_mlp[l]),
                     attn_width=attn_width, conv_width=conv_width,
                     xattn_width=xattn_width)
    return x
```

```python
import functools

import jax
import jax.numpy as jnp
from jax import lax
from jax.experimental import pallas as pl
from jax.experimental.pallas import tpu as pltpu

F32 = jnp.float32
BF16 = jnp.bfloat16

HEAD_DIM = 64
N_MEM_HEADS = 4
DILATED_PATTERNS = ((128, 1), (512, 4), (2048, 16))
CONV_K = 3
ROPE_THETA = 10000.0
EPS = 1e-6
NEG_INF = -1e30
LOG2_E = 1.4426950408889634

LANES = 128
BF16_SUBLANES = 16
ATTN_BLK = 128
VMEM_LIMIT_BYTES = 56 * 1024 * 1024


def _rms(x, g):
    return x * lax.rsqrt(jnp.mean(x * x, axis=-1, keepdims=True) + EPS) * g


def _in_proj_kernel(x_ref, pos_ref, g_ref, w_ref, o_ref, *, attn_width, chunk):
    h = _rms(x_ref[0], g_ref[...]).astype(BF16)

    half = HEAD_DIM // 2
    freq = lax.broadcasted_iota(jnp.int32, (half, 1), 0).astype(F32)
    inv_freq = jnp.float32(ROPE_THETA) ** (-(freq * 2.0 / HEAD_DIM))
    ang = inv_freq * pos_ref[0].astype(F32)
    cos_t, sin_t = jnp.cos(ang), jnp.sin(ang)
    reps = LANES // HEAD_DIM
    cos_k = jnp.concatenate([cos_t, cos_t] * reps, axis=0).T
    sin_k = jnp.concatenate([-sin_t, sin_t] * reps, axis=0).T
    q_scale = HEAD_DIM ** -0.5 * LOG2_E
    cos_q, sin_q = cos_k * q_scale, sin_k * q_scale
    lane = lax.broadcasted_iota(jnp.int32, (1, LANES), 1)
    first_half = (lane % HEAD_DIM) < half

    n_out = o_ref.shape[-1]
    for c0 in range(0, n_out, chunk):
        p = jnp.dot(h, w_ref[:, c0:c0 + chunk], preferred_element_type=F32)
        if c0 < 2 * attn_width:
            cos, sin = (cos_q, sin_q) if c0 < attn_width else (cos_k, sin_k)
            for g0 in range(0, chunk, LANES):
                t = p[:, g0:g0 + LANES]
                rot = jnp.where(first_half,
                                pltpu.roll(t, LANES - half, 1),
                                pltpu.roll(t, half, 1))
                r = t * cos + rot * sin
                o_ref[0, :, c0 + g0:c0 + g0 + LANES] = r.astype(BF16)
        else:
            o_ref[0, :, c0:c0 + chunk] = p.astype(BF16)


def _in_proj(x, pos3, g, w_bf16, *, attn_width, tm=512, chunk=512):
    B, S, D = x.shape
    n_out = w_bf16.shape[1]
    return pl.pallas_call(
        functools.partial(_in_proj_kernel, attn_width=attn_width, chunk=chunk),
        grid=(B, S // tm),
        in_specs=[
            pl.BlockSpec((1, tm, D), lambda b, t: (b, t, 0)),
            pl.BlockSpec((1, 1, tm), lambda b, t: (b, 0, t)),
            pl.BlockSpec((1, D), lambda b, t: (0, 0)),
            pl.BlockSpec((D, n_out), lambda b, t: (0, 0)),
        ],
        out_specs=pl.BlockSpec((1, tm, n_out), lambda b, t: (b, t, 0)),
        out_shape=jax.ShapeDtypeStruct((B, S, n_out), BF16),
        compiler_params=pltpu.CompilerParams(
            dimension_semantics=("parallel", "parallel"),
            vmem_limit_bytes=VMEM_LIMIT_BYTES),
        name="in_proj",
    )(x, pos3, g, w_bf16)


def _band_block(qb, kb, vb1, bias, lane_lo):
    ms, ls, accs = [], [], []
    for head_lo in (True, False):
        sel = lane_lo if head_lo else jnp.logical_not(lane_lo)
        qh = jnp.where(sel, qb, jnp.zeros_like(qb))
        s = lax.dot_general(qh, kb, (((1,), (1,)), ((), ())),
                            preferred_element_type=F32) + bias
        m = jnp.max(s, axis=-1, keepdims=True)
        p = jnp.exp2(s - m).astype(BF16)
        r = jnp.dot(p, vb1, preferred_element_type=F32)
        accs.append(r[:, :LANES])
        ls.append(r[:, LANES:])
        ms.append(jnp.broadcast_to(m, (m.shape[0], LANES)))
    return (jnp.where(lane_lo, ms[0], ms[1]), jnp.where(lane_lo, ls[0], ls[1]),
            jnp.where(lane_lo, accs[0], accs[1]))


COPY_ROWS = 64
DEINTERLEAVE = 4


def _for_each_split(seq, region, body):
    run = region // DEINTERLEAVE
    span = DEINTERLEAVE * COPY_ROWS
    steps_per_region = region // span

    def step(t, carry):
        strided0 = pl.multiple_of(t * span, span)
        dense0 = pl.multiple_of((t // steps_per_region) * region
                                + (t % steps_per_region) * COPY_ROWS, COPY_ROWS)
        for j in range(DEINTERLEAVE):
            body(pl.ds(strided0 + j, COPY_ROWS, stride=DEINTERLEAVE),
                 pl.ds(dense0 + j * run, COPY_ROWS))
        return carry

    lax.fori_loop(0, seq // span, step, 0)


def _merge_softmax(a, b):
    (m_a, l_a, a_a), (m_b, l_b, a_b) = a, b
    m = jnp.maximum(m_a, m_b)
    w_a = jnp.exp2(m_a - m)
    w_b = jnp.exp2(m_b - m)
    return m, l_a * w_a + l_b * w_b, a_a * w_a + a_b * w_b


def _dil_attn_kernel(trips_ref, q_ref, k_ref, v_ref, o_ref,
                     qf, kf, vf, q4f, k4f, v4f, qp, kp, vp1,
                     m_p4, l_p4, a_p4, m_tmp, l_tmp, a_tmp, bias_ref,
                     *, seq, patterns, unroll):
    blk = ATTN_BLK
    n_blocks = seq // blk
    lane_lo = lax.broadcasted_iota(jnp.int32, (1, LANES), 1) < HEAD_DIM
    assert [d for _, d in patterns] == [1, DEINTERLEAVE, DEINTERLEAVE ** 2]
    assert all(w // d == blk for w, d in patterns)
    run4 = seq // DEINTERLEAVE
    run16 = run4 // DEINTERLEAVE
    blocks_per_run4 = run4 // blk
    blocks_per_run16 = run16 // blk
    assert n_blocks % unroll == 0 and unroll % blocks_per_run4 == 0
    n_trips = trips_ref[0]

    qi = lax.broadcasted_iota(jnp.int32, (blk, 2 * blk), 0)
    kj = lax.broadcasted_iota(jnp.int32, (blk, 2 * blk), 1)
    bias_ref[0] = jnp.where(kj <= qi, 0.0, NEG_INF).astype(F32)
    bias_ref[1] = jnp.where((kj >= qi) & (kj <= qi + blk), 0.0, NEG_INF).astype(F32)
    vp1[:, LANES:] = jnp.ones((seq, LANES), BF16)

    pat4 = (m_p4, l_p4, a_p4)
    tmp = (m_tmp, l_tmp, a_tmp)

    def attend(sub_len, q_src, k_src, finish):
        nb = sub_len // blk
        assert nb >= 2

        def trip_body(trip, carry):
            for u in range(unroll):
                g = trip * unroll + u
                n = g % nb
                q0 = pl.multiple_of(g * blk, blk)
                k0 = pl.multiple_of(jnp.where(n > 0, q0 - blk, q0), blk)
                triple = _band_block(q_src[pl.ds(q0, blk), :],
                                     k_src[pl.ds(k0, 2 * blk), :],
                                     vp1[pl.ds(k0, 2 * blk), :],
                                     bias_ref[jnp.minimum(n, 1)], lane_lo)
                finish(trip, u, q0, triple)
            return carry

        lax.fori_loop(0, n_trips, trip_body, 0)

    qf[...] = q_ref[0].astype(F32)
    kf[...] = k_ref[0].astype(F32)
    vf[...] = v_ref[0].astype(F32)

    def gather4(strided, dense):
        for src, dst_f, dst_b in ((qf, q4f, qp), (kf, k4f, kp)):
            rows = src[strided, :]
            dst_f[dense, :] = rows
            dst_b[dense, :] = rows.astype(BF16)
        rows = vf[strided, :]
        v4f[dense, :] = rows
        vp1[dense, :LANES] = rows.astype(BF16)

    def store_pat4(trip, u, q0, triple):
        for ref, val in zip(pat4, triple):
            ref[pl.ds(q0, blk), :] = val

    _for_each_split(seq, seq, gather4)
    attend(run4, qp, kp, store_pat4)

    def gather16(strided, dense):
        qp[dense, :] = q4f[strided, :].astype(BF16)
        kp[dense, :] = k4f[strided, :].astype(BF16)
        vp1[dense, :LANES] = v4f[strided, :].astype(BF16)

    def fold_into_pat4(trip, u, q0, triple):
        r_static, idx = divmod(u, blocks_per_run4)
        j, a0 = idx // blocks_per_run16, (idx % blocks_per_run16) * blk
        r_dyn = pl.multiple_of(trip * (unroll // blocks_per_run4) * run4, run4)
        rows = pl.ds(r_dyn + r_static * run4 + j + DEINTERLEAVE * a0, blk,
                     stride=DEINTERLEAVE)
        merged = _merge_softmax(tuple(ref[rows, :] for ref in pat4), triple)
        for ref, val in zip(pat4, merged):
            ref[rows, :] = val

    _for_each_split(seq, run4, gather16)
    attend(run16, qp, kp, fold_into_pat4)

    def finish_natural(trip, u, q0, triple):
        sub = blk // DEINTERLEAVE
        i_dyn = pl.multiple_of(trip * (unroll * sub), unroll * sub)
        for j in range(DEINTERLEAVE):
            src = pl.ds(i_dyn + j * run4 + u * sub, sub)
            for t_ref, p_ref in zip(tmp, pat4):
                t_ref[u, pl.ds(j, sub, stride=DEINTERLEAVE), :] = p_ref[src, :]
        _, l, acc = _merge_softmax(tuple(t_ref[u] for t_ref in tmp), triple)
        o_ref[0, pl.ds(q0, blk), :] = (acc / l).astype(o_ref.dtype)

    vp1[:, :LANES] = v_ref[0]
    attend(seq, q_ref.at[0], k_ref.at[0], finish_natural)


def _dil_attn(proj, *, attn_width, unroll=32):
    B, S, _ = proj.shape
    n_pairs = attn_width // LANES
    trips = jnp.full((1,), S // ATTN_BLK // unroll, jnp.int32)
    kern = functools.partial(_dil_attn_kernel, seq=S, patterns=DILATED_PATTERNS,
                             unroll=unroll)
    col = lambda off: (lambda b, hp: (b, 0, off + hp))
    f32_buf = pltpu.VMEM((S, LANES), F32)
    bf16_buf = pltpu.VMEM((S, LANES), BF16)
    tmp_buf = pltpu.VMEM((unroll, ATTN_BLK, LANES), F32)
    return pl.pallas_call(
        kern,
        grid=(B, n_pairs),
        in_specs=[pl.BlockSpec(memory_space=pltpu.SMEM),
                  pl.BlockSpec((1, S, LANES), col(0)),
                  pl.BlockSpec((1, S, LANES), col(n_pairs)),
                  pl.BlockSpec((1, S, LANES), col(2 * n_pairs))],
        out_specs=pl.BlockSpec((1, S, LANES), lambda b, hp: (b, 0, hp)),
        out_shape=jax.ShapeDtypeStruct((B, S, attn_width), BF16),
        scratch_shapes=[f32_buf, f32_buf, f32_buf,
                        f32_buf, f32_buf, f32_buf,
                        bf16_buf, bf16_buf,
                        pltpu.VMEM((S, 2 * LANES), BF16),
                        f32_buf, f32_buf, f32_buf,
                        tmp_buf, tmp_buf, tmp_buf,
                        pltpu.VMEM((2, ATTN_BLK, 2 * ATTN_BLK), F32)],
        compiler_params=pltpu.CompilerParams(
            dimension_semantics=("parallel", "parallel"),
            vmem_limit_bytes=VMEM_LIMIT_BYTES),
        name="dil_attn",
    )(trips, proj, proj, proj)


def _mix_mlp_kernel(x_ref, ya_ref, bg_ref, cg_ref, u_ref, qx_ref, hc_ref, hu_ref,
                    mem_ref, g_mem_ref, w_mem_ref, conv_w_ref,
                    g_attn_ref, g_conv_ref, g_xattn_ref, w_out_ref, g_post_mix_ref,
                    g_pre_mlp_ref, w_up_ref, w_down_ref, g_post_mlp_ref,
                    o_ref, km_ref, vm_ref, *, ff_chunk):
    t = pl.program_id(1)
    xw = km_ref.shape[1]

    @pl.when(t == 0)
    def _():
        hm = _rms(mem_ref[0], g_mem_ref[...]).astype(BF16)
        kv = jnp.dot(hm, w_mem_ref[...], preferred_element_type=F32)
        km_ref[...] = kv[:, :xw].astype(BF16)
        vm_ref[...] = kv[:, xw:].astype(BF16)

    tm = x_ref.shape[1]

    z = cg_ref[0].astype(F32) * u_ref[0].astype(F32)
    hz = hc_ref[0].astype(F32) * hu_ref[0].astype(F32)
    hz = jnp.where(t > 0, hz, jnp.zeros_like(hz))
    z_ext = jnp.concatenate([hz, z], axis=0)
    cw = conv_w_ref[...]
    y_conv = z * cw[CONV_K - 1:CONV_K, :]
    for back in range(1, CONV_K):
        lo = BF16_SUBLANES - back
        y_conv = y_conv + z_ext[lo:lo + tm, :] * cw[CONV_K - 1 - back:CONV_K - back, :]
    y_conv = bg_ref[0].astype(F32) * y_conv

    qx = qx_ref[0]
    km = km_ref[...]
    vm = vm_ref[...]
    lane = lax.broadcasted_iota(jnp.int32, (1, xw), 1)
    xhd = xw // N_MEM_HEADS
    y_x = jnp.zeros((tm, xw), F32)
    for hd in range(N_MEM_HEADS):
        sel = (lane >= hd * xhd) & (lane < (hd + 1) * xhd)
        qh = jnp.where(sel, qx, jnp.zeros_like(qx))
        sc = lax.dot_general(qh, km, (((1,), (1,)), ((), ())),
                             preferred_element_type=F32) * (xhd ** -0.5)
        m = jnp.max(sc, axis=-1, keepdims=True)
        p = jnp.exp(sc - m)
        p = p / jnp.sum(p, axis=-1, keepdims=True)
        o = jnp.dot(p.astype(BF16), vm, preferred_element_type=F32)
        y_x = jnp.where(sel, o, y_x)

    y = jnp.concatenate([
        _rms(ya_ref[0].astype(F32), g_attn_ref[...]).astype(BF16),
        _rms(y_conv, g_conv_ref[...]).astype(BF16),
        _rms(y_x, g_xattn_ref[...]).astype(BF16)], axis=-1)
    y = jnp.dot(y, w_out_ref[...], preferred_element_type=F32)
    x1 = x_ref[0] + _rms(y, g_post_mix_ref[...])

    h2 = _rms(x1, g_pre_mlp_ref[...]).astype(BF16)
    d_ff = w_up_ref.shape[1]
    acc = jnp.zeros_like(x1)
    for f0 in range(0, d_ff, ff_chunk):
        up = jnp.dot(h2, w_up_ref[:, f0:f0 + ff_chunk], preferred_element_type=F32)
        act = jnp.square(jnp.maximum(up, 0.0)).astype(BF16)
        acc = acc + jnp.dot(act, w_down_ref[f0:f0 + ff_chunk, :],
                            preferred_element_type=F32)
    o_ref[0] = x1 + _rms(acc, g_post_mlp_ref[...])


def _mix_mlp(x, y_attn, proj, mem, g_mem, w_mem, conv_w, g_attn, g_conv, g_xattn,
             w_out, g_post_mix, g_pre_mlp, w_up, w_down, g_post_mlp,
             *, attn_width, conv_width, xattn_width, tm=512, ff_chunk=1024):
    B, S, D = x.shape
    n_mem = mem.shape[1]
    d_ff = w_up.shape[1]
    assert conv_width == xattn_width and (3 * attn_width) % conv_width == 0
    cb0 = 3 * attn_width // conv_width
    halo = BF16_SUBLANES
    const = lambda shape: pl.BlockSpec(shape, lambda b, t: (0,) * len(shape),
                                       pipeline_mode=pl.Buffered(1))
    pcol = lambda cb: pl.BlockSpec((1, tm, conv_width), lambda b, t: (b, t, cb))
    phalo = lambda cb: pl.BlockSpec(
        (1, halo, conv_width),
        lambda b, t: (b, jnp.maximum(t * (tm // halo) - 1, 0), cb))
    return pl.pallas_call(
        functools.partial(_mix_mlp_kernel, ff_chunk=ff_chunk),
        grid=(B, S // tm),
        in_specs=[
            pl.BlockSpec((1, tm, D), lambda b, t: (b, t, 0)),
            pl.BlockSpec((1, tm, attn_width), lambda b, t: (b, t, 0)),
            pcol(cb0), pcol(cb0 + 1), pcol(cb0 + 2), pcol(cb0 + 3),
            phalo(cb0 + 1), phalo(cb0 + 2),
            pl.BlockSpec((1, n_mem, D), lambda b, t: (b, 0, 0)),
            const((1, D)), const((D, 2 * xattn_width)), const((CONV_K, conv_width)),
            const((1, attn_width)), const((1, conv_width)), const((1, xattn_width)),
            const((D, D)), const((1, D)),
            const((1, D)), const((D, d_ff)), const((d_ff, D)), const((1, D)),
        ],
        out_specs=pl.BlockSpec((1, tm, D), lambda b, t: (b, t, 0)),
        out_shape=jax.ShapeDtypeStruct((B, S, D), x.dtype),
        scratch_shapes=[pltpu.VMEM((n_mem, xattn_width), BF16),
                        pltpu.VMEM((n_mem, xattn_width), BF16)],
        compiler_params=pltpu.CompilerParams(
            dimension_semantics=("parallel", "arbitrary"),
            vmem_limit_bytes=VMEM_LIMIT_BYTES),
        name="mix_mlp",
    )(x, y_attn, proj, proj, proj, proj, proj, proj, mem, g_mem, w_mem, conv_w,
      g_attn, g_conv, g_xattn, w_out, g_post_mix, g_pre_mlp, w_up, w_down, g_post_mlp)


def kernel(x, mem, positions, g_pre_mix, g_mem, w_in, w_mem_kv, conv_w, g_attn_out,
           g_conv_out, g_xattn_out, w_out, g_post_mix, g_pre_mlp, w_up, w_down,
           g_post_mlp):
    depth = w_in.shape[0]
    attn_width = g_attn_out.shape[1]
    conv_width = g_conv_out.shape[1]
    xattn_width = g_xattn_out.shape[1]
    pos3 = positions[:, None, :]
    row = lambda g: g[None, :]
    for l in range(depth):
        proj = _in_proj(x, pos3, row(g_pre_mix[l]), w_in[l].astype(BF16),
                        attn_width=attn_width)
        y_attn = _dil_attn(proj, attn_width=attn_width)
        x = _mix_mlp(x, y_attn, proj, mem, row(g_mem[l]), w_mem_kv[l].astype(BF16),
                     conv_w[l], row(g_attn_out[l]), row(g_conv_out[l]),
                     row(g_xattn_out[l]), w_out[l].astype(BF16), row(g_post_mix[l]),
                     row(g_pre_mlp[l]), w_up[l].astype(BF16), w_down[l].astype(BF16),
                     row(g_post_mlp[l]),
                     attn_width=attn_width, conv_width=conv_width,
                     xattn_width=xattn_width)
    return x
```

```python
import functools

import jax
import jax.numpy as jnp
from jax import lax
from jax.experimental import pallas as pl
from jax.experimental.pallas import tpu as pltpu

F32 = jnp.float32
BF16 = jnp.bfloat16

HEAD_DIM = 64
N_MEM_HEADS = 4
DILATED_PATTERNS = ((128, 1), (512, 4), (2048, 16))
CONV_K = 3
ROPE_THETA = 10000.0
EPS = 1e-6
NEG_INF = -1e30
LOG2_E = 1.4426950408889634

LANES = 128
BF16_SUBLANES = 16
ATTN_BLK = 128
VMEM_LIMIT_BYTES = 56 * 1024 * 1024


def _rms(x, g):
    return x * lax.rsqrt(jnp.mean(x * x, axis=-1, keepdims=True) + EPS) * g


def _in_proj_kernel(x_ref, pos_ref, g_ref, w_ref, o_ref, *, attn_width, xattn_width, chunk):
    h = _rms(x_ref[0], g_ref[...]).astype(BF16)

    half = HEAD_DIM // 2
    freq = lax.broadcasted_iota(jnp.int32, (half, 1), 0).astype(F32)
    inv_freq = jnp.float32(ROPE_THETA) ** (-(freq * 2.0 / HEAD_DIM))
    ang = inv_freq * pos_ref[0].astype(F32)
    cos_t, sin_t = jnp.cos(ang), jnp.sin(ang)
    reps = LANES // HEAD_DIM
    cos_k = jnp.concatenate([cos_t, cos_t] * reps, axis=0).T
    sin_k = jnp.concatenate([-sin_t, sin_t] * reps, axis=0).T
    q_scale = HEAD_DIM ** -0.5 * LOG2_E
    cos_q, sin_q = cos_k * q_scale, sin_k * q_scale
    lane = lax.broadcasted_iota(jnp.int32, (1, LANES), 1)
    first_half = (lane % HEAD_DIM) < half

    n_out = o_ref.shape[-1]
    for c0 in range(0, n_out, chunk):
        p = jnp.dot(h, w_ref[:, c0:c0 + chunk], preferred_element_type=F32)
        if c0 < 2 * attn_width:
            cos, sin = (cos_q, sin_q) if c0 < attn_width else (cos_k, sin_k)
            for g0 in range(0, chunk, LANES):
                t = p[:, g0:g0 + LANES]
                rot = jnp.where(first_half,
                                pltpu.roll(t, LANES - half, 1),
                                pltpu.roll(t, half, 1))
                r = t * cos + rot * sin
                o_ref[0, :, c0 + g0:c0 + g0 + LANES] = r.astype(BF16)
        elif c0 + chunk == n_out:
            lo = chunk - xattn_width
            o_ref[0, :, c0:c0 + lo] = p[:, :lo].astype(BF16)
            o_ref[0, :, c0 + lo:c0 + chunk] = (p[:, lo:] * q_scale).astype(BF16)
        else:
            o_ref[0, :, c0:c0 + chunk] = p.astype(BF16)


def _in_proj(x, pos3, g, w_bf16, *, attn_width, xattn_width, tm=512, chunk=512):
    B, S, D = x.shape
    n_out = w_bf16.shape[1]
    return pl.pallas_call(
        functools.partial(_in_proj_kernel, attn_width=attn_width,
                          xattn_width=xattn_width, chunk=chunk),
        grid=(B, S // tm),
        in_specs=[
            pl.BlockSpec((1, tm, D), lambda b, t: (b, t, 0)),
            pl.BlockSpec((1, 1, tm), lambda b, t: (b, 0, t)),
            pl.BlockSpec((1, D), lambda b, t: (0, 0)),
            pl.BlockSpec((D, n_out), lambda b, t: (0, 0)),
        ],
        out_specs=pl.BlockSpec((1, tm, n_out), lambda b, t: (b, t, 0)),
        out_shape=jax.ShapeDtypeStruct((B, S, n_out), BF16),
        compiler_params=pltpu.CompilerParams(
            dimension_semantics=("parallel", "parallel"),
            vmem_limit_bytes=VMEM_LIMIT_BYTES),
        name="in_proj",
    )(x, pos3, g, w_bf16)


def _band_block(qb, kb, vb1, bias, lane_lo):
    ms, ls, accs = [], [], []
    for head_lo in (True, False):
        sel = lane_lo if head_lo else jnp.logical_not(lane_lo)
        qh = jnp.where(sel, qb, jnp.zeros_like(qb))
        s = lax.dot_general(qh, kb, (((1,), (1,)), ((), ())),
                            preferred_element_type=F32) + bias
        m = jnp.max(s, axis=-1, keepdims=True)
        p = jnp.exp2(s - m).astype(BF16)
        r = jnp.dot(p, vb1, preferred_element_type=F32)
        accs.append(r[:, :LANES])
        ls.append(r[:, LANES:])
        ms.append(jnp.broadcast_to(m, (m.shape[0], LANES)))
    return (jnp.where(lane_lo, ms[0], ms[1]), jnp.where(lane_lo, ls[0], ls[1]),
            jnp.where(lane_lo, accs[0], accs[1]))


COPY_ROWS = 64
DEINTERLEAVE = 4


def _for_each_split(seq, region, body):
    run = region // DEINTERLEAVE
    span = DEINTERLEAVE * COPY_ROWS
    steps_per_region = region // span

    def step(t, carry):
        strided0 = pl.multiple_of(t * span, span)
        dense0 = pl.multiple_of((t // steps_per_region) * region
                                + (t % steps_per_region) * COPY_ROWS, COPY_ROWS)
        for j in range(DEINTERLEAVE):
            body(pl.ds(strided0 + j, COPY_ROWS, stride=DEINTERLEAVE),
                 pl.ds(dense0 + j * run, COPY_ROWS))
        return carry

    lax.fori_loop(0, seq // span, step, 0)


def _merge_softmax(a, b):
    (m_a, l_a, a_a), (m_b, l_b, a_b) = a, b
    m = jnp.maximum(m_a, m_b)
    w_a = jnp.exp2(m_a - m)
    w_b = jnp.exp2(m_b - m)
    return m, l_a * w_a + l_b * w_b, a_a * w_a + a_b * w_b


def _dil_attn_kernel(trips_ref, q_ref, k_ref, v_ref, o_ref,
                     qf, kf, vf, q4f, k4f, v4f, qp, kp, vp1,
                     m_p4, l_p4, a_p4, m_tmp, l_tmp, a_tmp, bias_ref,
                     *, seq, patterns, unroll):
    blk = ATTN_BLK
    n_blocks = seq // blk
    lane_lo = lax.broadcasted_iota(jnp.int32, (1, LANES), 1) < HEAD_DIM
    assert [d for _, d in patterns] == [1, DEINTERLEAVE, DEINTERLEAVE ** 2]
    assert all(w // d == blk for w, d in patterns)
    run4 = seq // DEINTERLEAVE
    run16 = run4 // DEINTERLEAVE
    blocks_per_run4 = run4 // blk
    blocks_per_run16 = run16 // blk
    assert n_blocks % unroll == 0 and unroll % blocks_per_run4 == 0
    n_trips = trips_ref[0]

    qi = lax.broadcasted_iota(jnp.int32, (blk, 2 * blk), 0)
    kj = lax.broadcasted_iota(jnp.int32, (blk, 2 * blk), 1)
    bias_ref[0] = jnp.where(kj <= qi, 0.0, NEG_INF).astype(F32)
    bias_ref[1] = jnp.where((kj >= qi) & (kj <= qi + blk), 0.0, NEG_INF).astype(F32)
    vp1[:, LANES:] = jnp.ones((seq, LANES), BF16)

    pat4 = (m_p4, l_p4, a_p4)
    tmp = (m_tmp, l_tmp, a_tmp)

    def attend(sub_len, q_src, k_src, finish):
        nb = sub_len // blk
        assert nb >= 2

        def trip_body(trip, carry):
            for u in range(unroll):
                g = trip * unroll + u
                n = g % nb
                q0 = pl.multiple_of(g * blk, blk)
                k0 = pl.multiple_of(jnp.where(n > 0, q0 - blk, q0), blk)
                triple = _band_block(q_src[pl.ds(q0, blk), :],
                                     k_src[pl.ds(k0, 2 * blk), :],
                                     vp1[pl.ds(k0, 2 * blk), :],
                                     bias_ref[jnp.minimum(n, 1)], lane_lo)
                finish(trip, u, q0, triple)
            return carry

        lax.fori_loop(0, n_trips, trip_body, 0)

    qf[...] = q_ref[0].astype(F32)
    kf[...] = k_ref[0].astype(F32)
    vf[...] = v_ref[0].astype(F32)

    def gather4(strided, dense):
        for src, dst_f, dst_b in ((qf, q4f, qp), (kf, k4f, kp)):
            rows = src[strided, :]
            dst_f[dense, :] = rows
            dst_b[dense, :] = rows.astype(BF16)
        rows = vf[strided, :]
        v4f[dense, :] = rows
        vp1[dense, :LANES] = rows.astype(BF16)

    def store_pat4(trip, u, q0, triple):
        for ref, val in zip(pat4, triple):
            ref[pl.ds(q0, blk), :] = val

    _for_each_split(seq, seq, gather4)
    attend(run4, qp, kp, store_pat4)

    def gather16(strided, dense):
        qp[dense, :] = q4f[strided, :].astype(BF16)
        kp[dense, :] = k4f[strided, :].astype(BF16)
        vp1[dense, :LANES] = v4f[strided, :].astype(BF16)

    def fold_into_pat4(trip, u, q0, triple):
        r_static, idx = divmod(u, blocks_per_run4)
        j, a0 = idx // blocks_per_run16, (idx % blocks_per_run16) * blk
        r_dyn = pl.multiple_of(trip * (unroll // blocks_per_run4) * run4, run4)
        rows = pl.ds(r_dyn + r_static * run4 + j + DEINTERLEAVE * a0, blk,
                     stride=DEINTERLEAVE)
        merged = _merge_softmax(tuple(ref[rows, :] for ref in pat4), triple)
        for ref, val in zip(pat4, merged):
            ref[rows, :] = val

    _for_each_split(seq, run4, gather16)
    attend(run16, qp, kp, fold_into_pat4)

    def finish_natural(trip, u, q0, triple):
        sub = blk // DEINTERLEAVE
        i_dyn = pl.multiple_of(trip * (unroll * sub), unroll * sub)
        for j in range(DEINTERLEAVE):
            src = pl.ds(i_dyn + j * run4 + u * sub, sub)
            for t_ref, p_ref in zip(tmp, pat4):
                t_ref[u, pl.ds(j, sub, stride=DEINTERLEAVE), :] = p_ref[src, :]
        _, l, acc = _merge_softmax(tuple(t_ref[u] for t_ref in tmp), triple)
        o_ref[0, pl.ds(q0, blk), :] = (acc / l).astype(o_ref.dtype)

    vp1[:, :LANES] = v_ref[0]
    attend(seq, q_ref.at[0], k_ref.at[0], finish_natural)


def _dil_attn(proj, *, attn_width, unroll=32):
    B, S, _ = proj.shape
    n_pairs = attn_width // LANES
    trips = jnp.full((1,), S // ATTN_BLK // unroll, jnp.int32)
    kern = functools.partial(_dil_attn_kernel, seq=S, patterns=DILATED_PATTERNS,
                             unroll=unroll)
    col = lambda off: (lambda b, hp: (b, 0, off + hp))
    f32_buf = pltpu.VMEM((S, LANES), F32)
    bf16_buf = pltpu.VMEM((S, LANES), BF16)
    tmp_buf = pltpu.VMEM((unroll, ATTN_BLK, LANES), F32)
    return pl.pallas_call(
        kern,
        grid=(B, n_pairs),
        in_specs=[pl.BlockSpec(memory_space=pltpu.SMEM),
                  pl.BlockSpec((1, S, LANES), col(0)),
                  pl.BlockSpec((1, S, LANES), col(n_pairs)),
                  pl.BlockSpec((1, S, LANES), col(2 * n_pairs))],
        out_specs=pl.BlockSpec((1, S, LANES), lambda b, hp: (b, 0, hp)),
        out_shape=jax.ShapeDtypeStruct((B, S, attn_width), BF16),
        scratch_shapes=[f32_buf, f32_buf, f32_buf,
                        f32_buf, f32_buf, f32_buf,
                        bf16_buf, bf16_buf,
                        pltpu.VMEM((S, 2 * LANES), BF16),
                        f32_buf, f32_buf, f32_buf,
                        tmp_buf, tmp_buf, tmp_buf,
                        pltpu.VMEM((2, ATTN_BLK, 2 * ATTN_BLK), F32)],
        compiler_params=pltpu.CompilerParams(
            dimension_semantics=("parallel", "parallel"),
            vmem_limit_bytes=VMEM_LIMIT_BYTES),
        name="dil_attn",
    )(trips, proj, proj, proj)


STAGE_ROWS, STAGE_COLS = 512, 1024


def _load_weights_as_bf16(pairs, stage, sems):
    tiles = []
    for w_hbm, w_vmem in pairs:
        n_rows, n_cols = w_hbm.shape
        rows, cols = min(n_rows, STAGE_ROWS), min(n_cols, STAGE_COLS)
        assert n_rows % rows == 0 and n_cols % cols == 0
        tiles += [(w_hbm, w_vmem, r0, c0, rows, cols)
                  for r0 in range(0, n_rows, rows) for c0 in range(0, n_cols, cols)]

    def copy(i):
        w_hbm, _, r0, c0, rows, cols = tiles[i]
        slot = i % 2
        return pltpu.make_async_copy(
            w_hbm.at[pl.ds(r0, rows), pl.ds(c0, cols)],
            stage.at[slot, pl.ds(0, rows), pl.ds(0, cols)], sems.at[slot])

    copy(0).start()
    for i, (_, w_vmem, r0, c0, rows, cols) in enumerate(tiles):
        if i + 1 < len(tiles):
            copy(i + 1).start()
        copy(i).wait()
        w_vmem[pl.ds(r0, rows), pl.ds(c0, cols)] = stage[i % 2, :rows, :cols].astype(BF16)


def _mix_mlp_kernel(x_ref, ya_ref, bg_ref, cg_ref, u_ref, qx_ref, hc_ref, hu_ref,
                    mem_ref, g_mem_ref, w_mem_hbm, conv_w_ref,
                    g_attn_ref, g_conv_ref, g_xattn_ref, w_out_hbm, g_post_mix_ref,
                    g_pre_mlp_ref, w_up_hbm, w_down_hbm, g_post_mlp_ref,
                    o_ref, km_ref, vm_ref, w_mem_ref, w_out_ref, w_up_ref, w_down_ref,
                    stage_ref, sem_ref, *, ff_chunk):
    t = pl.program_id(1)
    xw = km_ref.shape[1]

    @pl.when((pl.program_id(0) == 0) & (t == 0))
    def _():
        _load_weights_as_bf16([(w_mem_hbm, w_mem_ref), (w_out_hbm, w_out_ref),
                               (w_up_hbm, w_up_ref), (w_down_hbm, w_down_ref)],
                              stage_ref, sem_ref)

    @pl.when(t == 0)
    def _():
        hm = _rms(mem_ref[0], g_mem_ref[...]).astype(BF16)
        kv = jnp.dot(hm, w_mem_ref[...], preferred_element_type=F32)
        km_ref[...] = kv[:, :xw].astype(BF16)
        vm_ref[...] = kv[:, xw:].astype(BF16)

    tm = x_ref.shape[1]

    z = cg_ref[0].astype(F32) * u_ref[0].astype(F32)
    hz = hc_ref[0].astype(F32) * hu_ref[0].astype(F32)
    hz = jnp.where(t > 0, hz, jnp.zeros_like(hz))
    z_ext = jnp.concatenate([hz, z], axis=0)
    cw = conv_w_ref[...]
    y_conv = z * cw[CONV_K - 1:CONV_K, :]
    for back in range(1, CONV_K):
        lo = BF16_SUBLANES - back
        y_conv = y_conv + z_ext[lo:lo + tm, :] * cw[CONV_K - 1 - back:CONV_K - back, :]
    y_conv = bg_ref[0].astype(F32) * y_conv

    qx = qx_ref[0]
    km = km_ref[...]
    vm = vm_ref[...]
    lane = lax.broadcasted_iota(jnp.int32, (1, xw), 1)
    xhd = xw // N_MEM_HEADS
    assert xhd == HEAD_DIM
    y_x = jnp.zeros((tm, xw), F32)
    for hd in range(N_MEM_HEADS):
        sel = (lane >= hd * xhd) & (lane < (hd + 1) * xhd)
        qh = jnp.where(sel, qx, jnp.zeros_like(qx))
        sc = lax.dot_general(qh, km, (((1,), (1,)), ((), ())),
                             preferred_element_type=F32)
        p = jnp.exp2(sc - jnp.max(sc, axis=-1, keepdims=True))
        p = p / jnp.sum(p, axis=-1, keepdims=True)
        o = jnp.dot(p.astype(BF16), vm, preferred_element_type=F32)
        y_x = jnp.where(sel, o, y_x)

    y = jnp.concatenate([
        _rms(ya_ref[0].astype(F32), g_attn_ref[...]).astype(BF16),
        _rms(y_conv, g_conv_ref[...]).astype(BF16),
        _rms(y_x, g_xattn_ref[...]).astype(BF16)], axis=-1)
    y = jnp.dot(y, w_out_ref[...], preferred_element_type=F32)
    x1 = x_ref[0] + _rms(y, g_post_mix_ref[...])

    h2 = _rms(x1, g_pre_mlp_ref[...]).astype(BF16)
    d_ff = w_up_ref.shape[1]
    acc = jnp.zeros_like(x1)
    for f0 in range(0, d_ff, ff_chunk):
        up = jnp.dot(h2, w_up_ref[:, f0:f0 + ff_chunk], preferred_element_type=F32)
        act = jnp.square(jnp.maximum(up, 0.0)).astype(BF16)
        acc = acc + jnp.dot(act, w_down_ref[f0:f0 + ff_chunk, :],
                            preferred_element_type=F32)
    o_ref[0] = x1 + _rms(acc, g_post_mlp_ref[...])


def _mix_mlp(x, y_attn, proj, mem, g_mem, w_mem, conv_w, g_attn, g_conv, g_xattn,
             w_out, g_post_mix, g_pre_mlp, w_up, w_down, g_post_mlp,
             *, attn_width, conv_width, xattn_width, tm=512, ff_chunk=1024):
    B, S, D = x.shape
    n_mem = mem.shape[1]
    d_ff = w_up.shape[1]
    assert conv_width == xattn_width and (3 * attn_width) % conv_width == 0
    cb0 = 3 * attn_width // conv_width
    halo = BF16_SUBLANES
    const = lambda shape: pl.BlockSpec(shape, lambda b, t: (0,) * len(shape),
                                       pipeline_mode=pl.Buffered(1))
    hbm = pl.BlockSpec(memory_space=pl.ANY)
    pcol = lambda cb: pl.BlockSpec((1, tm, conv_width), lambda b, t: (b, t, cb))
    phalo = lambda cb: pl.BlockSpec(
        (1, halo, conv_width),
        lambda b, t: (b, jnp.maximum(t * (tm // halo) - 1, 0), cb))
    return pl.pallas_call(
        functools.partial(_mix_mlp_kernel, ff_chunk=ff_chunk),
        grid=(B, S // tm),
        in_specs=[
            pl.BlockSpec((1, tm, D), lambda b, t: (b, t, 0)),
            pl.BlockSpec((1, tm, attn_width), lambda b, t: (b, t, 0)),
            pcol(cb0), pcol(cb0 + 1), pcol(cb0 + 2), pcol(cb0 + 3),
            phalo(cb0 + 1), phalo(cb0 + 2),
            pl.BlockSpec((1, n_mem, D), lambda b, t: (b, 0, 0)),
            const((1, D)), hbm, const((CONV_K, conv_width)),
            const((1, attn_width)), const((1, conv_width)), const((1, xattn_width)),
            hbm, const((1, D)),
            const((1, D)), hbm, hbm, const((1, D)),
        ],
        out_specs=pl.BlockSpec((1, tm, D), lambda b, t: (b, t, 0)),
        out_shape=jax.ShapeDtypeStruct((B, S, D), x.dtype),
        scratch_shapes=[pltpu.VMEM((n_mem, xattn_width), BF16),
                        pltpu.VMEM((n_mem, xattn_width), BF16),
                        pltpu.VMEM(w_mem.shape, BF16),
                        pltpu.VMEM(w_out.shape, BF16),
                        pltpu.VMEM(w_up.shape, BF16),
                        pltpu.VMEM(w_down.shape, BF16),
                        pltpu.VMEM((2, STAGE_ROWS, STAGE_COLS), F32),
                        pltpu.SemaphoreType.DMA((2,))],
        compiler_params=pltpu.CompilerParams(
            dimension_semantics=("arbitrary", "arbitrary"),
            vmem_limit_bytes=VMEM_LIMIT_BYTES),
        name="mix_mlp",
    )(x, y_attn, proj, proj, proj, proj, proj, proj, mem, g_mem, w_mem, conv_w,
      g_attn, g_conv, g_xattn, w_out, g_post_mix, g_pre_mlp, w_up, w_down, g_post_mlp)


def kernel(x, mem, positions, g_pre_mix, g_mem, w_in, w_mem_kv, conv_w, g_attn_out,
           g_conv_out, g_xattn_out, w_out, g_post_mix, g_pre_mlp, w_up, w_down,
           g_post_mlp):
    depth = w_in.shape[0]
    attn_width = g_attn_out.shape[1]
    conv_width = g_conv_out.shape[1]
    xattn_width = g_xattn_out.shape[1]
    pos3 = positions[:, None, :]
    row = lambda g: g[None, :]
    for l in range(depth):
        proj = _in_proj(x, pos3, row(g_pre_mix[l]), w_in[l].astype(BF16),
                        attn_width=attn_width, xattn_width=xattn_width)
        y_attn = _dil_attn(proj, attn_width=attn_width)
        x = _mix_mlp(x, y_attn, proj, mem, row(g_mem[l]), w_mem_kv[l], conv_w[l],
                     row(g_attn_out[l]), row(g_conv_out[l]), row(g_xattn_out[l]),
                     w_out[l], row(g_post_mix[l]), row(g_pre_mlp[l]), w_up[l], w_down[l],
                     row(g_post_mlp[l]),
                     attn_width=attn_width, conv_width=conv_width,
                     xattn_width=xattn_width)
    return x
```

```python
import functools

import jax
import jax.numpy as jnp
from jax import lax
from jax.experimental import pallas as pl
from jax.experimental.pallas import tpu as pltpu

F32 = jnp.float32
BF16 = jnp.bfloat16

HEAD_DIM = 64
N_MEM_HEADS = 4
DILATED_PATTERNS = ((128, 1), (512, 4), (2048, 16))
CONV_K = 3
ROPE_THETA = 10000.0
EPS = 1e-6
NEG_INF = -1e30
LOG2_E = 1.4426950408889634

LANES = 128
BF16_SUBLANES = 16
ATTN_BLK = 128
VMEM_LIMIT_BYTES = 56 * 1024 * 1024


def _rms(x, g):
    return x * lax.rsqrt(jnp.mean(x * x, axis=-1, keepdims=True) + EPS) * g


def _in_proj_kernel(x_ref, pos_ref, g_ref, w_ref, o_ref, *, attn_width, xattn_width, chunk):
    h = _rms(x_ref[0], g_ref[...]).astype(BF16)

    half = HEAD_DIM // 2
    freq = lax.broadcasted_iota(jnp.int32, (half, 1), 0).astype(F32)
    inv_freq = jnp.float32(ROPE_THETA) ** (-(freq * 2.0 / HEAD_DIM))
    ang = inv_freq * pos_ref[0].astype(F32)
    cos_t, sin_t = jnp.cos(ang), jnp.sin(ang)
    reps = LANES // HEAD_DIM
    cos_k = jnp.concatenate([cos_t, cos_t] * reps, axis=0).T
    sin_k = jnp.concatenate([-sin_t, sin_t] * reps, axis=0).T
    q_scale = HEAD_DIM ** -0.5 * LOG2_E
    cos_q, sin_q = cos_k * q_scale, sin_k * q_scale
    lane = lax.broadcasted_iota(jnp.int32, (1, LANES), 1)
    first_half = (lane % HEAD_DIM) < half

    n_out = o_ref.shape[-1]
    for c0 in range(0, n_out, chunk):
        p = jnp.dot(h, w_ref[:, c0:c0 + chunk], preferred_element_type=F32)
        if c0 < 2 * attn_width:
            cos, sin = (cos_q, sin_q) if c0 < attn_width else (cos_k, sin_k)
            for g0 in range(0, chunk, LANES):
                t = p[:, g0:g0 + LANES]
                rot = jnp.where(first_half,
                                pltpu.roll(t, LANES - half, 1),
                                pltpu.roll(t, half, 1))
                r = t * cos + rot * sin
                o_ref[0, :, c0 + g0:c0 + g0 + LANES] = r.astype(BF16)
        elif c0 + chunk == n_out:
            lo = chunk - xattn_width
            o_ref[0, :, c0:c0 + lo] = p[:, :lo].astype(BF16)
            o_ref[0, :, c0 + lo:c0 + chunk] = (p[:, lo:] * q_scale).astype(BF16)
        else:
            o_ref[0, :, c0:c0 + chunk] = p.astype(BF16)


def _in_proj(x, pos3, g, w_bf16, *, attn_width, xattn_width, tm=512, chunk=512):
    B, S, D = x.shape
    n_out = w_bf16.shape[1]
    return pl.pallas_call(
        functools.partial(_in_proj_kernel, attn_width=attn_width,
                          xattn_width=xattn_width, chunk=chunk),
        grid=(B, S // tm),
        in_specs=[
            pl.BlockSpec((1, tm, D), lambda b, t: (b, t, 0)),
            pl.BlockSpec((1, 1, tm), lambda b, t: (b, 0, t)),
            pl.BlockSpec((1, D), lambda b, t: (0, 0)),
            pl.BlockSpec((D, n_out), lambda b, t: (0, 0)),
        ],
        out_specs=pl.BlockSpec((1, tm, n_out), lambda b, t: (b, t, 0)),
        out_shape=jax.ShapeDtypeStruct((B, S, n_out), BF16),
        compiler_params=pltpu.CompilerParams(
            dimension_semantics=("parallel", "parallel"),
            vmem_limit_bytes=VMEM_LIMIT_BYTES),
        name="in_proj",
    )(x, pos3, g, w_bf16)


def _band_block(qb, kb, vb1, bias, lane_lo):
    ms, ls, accs = [], [], []
    for head_lo in (True, False):
        sel = lane_lo if head_lo else jnp.logical_not(lane_lo)
        qh = jnp.where(sel, qb, jnp.zeros_like(qb))
        s = lax.dot_general(qh, kb, (((1,), (1,)), ((), ())),
                            preferred_element_type=F32) + bias
        m = jnp.max(s, axis=-1, keepdims=True)
        p = jnp.exp2(s - m).astype(BF16)
        r = jnp.dot(p, vb1, preferred_element_type=F32)
        accs.append(r[:, :LANES])
        ls.append(r[:, LANES:])
        ms.append(jnp.broadcast_to(m, (m.shape[0], LANES)))
    return (jnp.where(lane_lo, ms[0], ms[1]), jnp.where(lane_lo, ls[0], ls[1]),
            jnp.where(lane_lo, accs[0], accs[1]))


COPY_ROWS = 64
DEINTERLEAVE = 4


def _for_each_split(seq, region, body):
    run = region // DEINTERLEAVE
    span = DEINTERLEAVE * COPY_ROWS
    steps_per_region = region // span

    def step(t, carry):
        strided0 = pl.multiple_of(t * span, span)
        dense0 = pl.multiple_of((t // steps_per_region) * region
                                + (t % steps_per_region) * COPY_ROWS, COPY_ROWS)
        for j in range(DEINTERLEAVE):
            body(pl.ds(strided0 + j, COPY_ROWS, stride=DEINTERLEAVE),
                 pl.ds(dense0 + j * run, COPY_ROWS))
        return carry

    lax.fori_loop(0, seq // span, step, 0)


def _merge_softmax(a, b):
    (m_a, l_a, a_a), (m_b, l_b, a_b) = a, b
    m = jnp.maximum(m_a, m_b)
    w_a = jnp.exp2(m_a - m)
    w_b = jnp.exp2(m_b - m)
    return m, l_a * w_a + l_b * w_b, a_a * w_a + a_b * w_b


def _dil_attn_kernel(trips_ref, q_ref, k_ref, v_ref, o_ref,
                     qf, kf, vf, q4f, k4f, v4f, qp, kp, vp1,
                     m_p4, l_p4, a_p4, m_tmp, l_tmp, a_tmp, bias_ref,
                     *, seq, patterns, unroll):
    blk = ATTN_BLK
    n_blocks = seq // blk
    lane_lo = lax.broadcasted_iota(jnp.int32, (1, LANES), 1) < HEAD_DIM
    assert [d for _, d in patterns] == [1, DEINTERLEAVE, DEINTERLEAVE ** 2]
    assert all(w // d == blk for w, d in patterns)
    run4 = seq // DEINTERLEAVE
    run16 = run4 // DEINTERLEAVE
    blocks_per_run4 = run4 // blk
    blocks_per_run16 = run16 // blk
    assert n_blocks % unroll == 0 and unroll % blocks_per_run4 == 0
    n_trips = trips_ref[0]

    qi = lax.broadcasted_iota(jnp.int32, (blk, 2 * blk), 0)
    kj = lax.broadcasted_iota(jnp.int32, (blk, 2 * blk), 1)
    bias_ref[0] = jnp.where(kj <= qi, 0.0, NEG_INF).astype(F32)
    bias_ref[1] = jnp.where((kj >= qi) & (kj <= qi + blk), 0.0, NEG_INF).astype(F32)
    vp1[:, LANES:] = jnp.ones((seq, LANES), BF16)

    pat4 = (m_p4, l_p4, a_p4)
    tmp = (m_tmp, l_tmp, a_tmp)

    def attend(sub_len, q_src, k_src, finish):
        nb = sub_len // blk
        assert nb >= 2

        def trip_body(trip, carry):
            for u in range(unroll):
                g = trip * unroll + u
                n = g % nb
                q0 = pl.multiple_of(g * blk, blk)
                k0 = pl.multiple_of(jnp.where(n > 0, q0 - blk, q0), blk)
                triple = _band_block(q_src[pl.ds(q0, blk), :],
                                     k_src[pl.ds(k0, 2 * blk), :],
                                     vp1[pl.ds(k0, 2 * blk), :],
                                     bias_ref[jnp.minimum(n, 1)], lane_lo)
                finish(trip, u, q0, triple)
            return carry

        lax.fori_loop(0, n_trips, trip_body, 0)

    qf[...] = q_ref[0].astype(F32)
    kf[...] = k_ref[0].astype(F32)
    vf[...] = v_ref[0].astype(F32)

    def gather4(strided, dense):
        for src, dst_f, dst_b in ((qf, q4f, qp), (kf, k4f, kp)):
            rows = src[strided, :]
            dst_f[dense, :] = rows
            dst_b[dense, :] = rows.astype(BF16)
        rows = vf[strided, :]
        v4f[dense, :] = rows
        vp1[dense, :LANES] = rows.astype(BF16)

    def store_pat4(trip, u, q0, triple):
        for ref, val in zip(pat4, triple):
            ref[pl.ds(q0, blk), :] = val

    _for_each_split(seq, seq, gather4)
    attend(run4, qp, kp, store_pat4)

    def gather16(strided, dense):
        qp[dense, :] = q4f[strided, :].astype(BF16)
        kp[dense, :] = k4f[strided, :].astype(BF16)
        vp1[dense, :LANES] = v4f[strided, :].astype(BF16)

    def fold_into_pat4(trip, u, q0, triple):
        r_static, idx = divmod(u, blocks_per_run4)
        j, a0 = idx // blocks_per_run16, (idx % blocks_per_run16) * blk
        r_dyn = pl.multiple_of(trip * (unroll // blocks_per_run4) * run4, run4)
        rows = pl.ds(r_dyn + r_static * run4 + j + DEINTERLEAVE * a0, blk,
                     stride=DEINTERLEAVE)
        merged = _merge_softmax(tuple(ref[rows, :] for ref in pat4), triple)
        for ref, val in zip(pat4, merged):
            ref[rows, :] = val

    _for_each_split(seq, run4, gather16)
    attend(run16, qp, kp, fold_into_pat4)

    def finish_natural(trip, u, q0, triple):
        sub = blk // DEINTERLEAVE
        i_dyn = pl.multiple_of(trip * (unroll * sub), unroll * sub)
        for j in range(DEINTERLEAVE):
            src = pl.ds(i_dyn + j * run4 + u * sub, sub)
            for t_ref, p_ref in zip(tmp, pat4):
                t_ref[u, pl.ds(j, sub, stride=DEINTERLEAVE), :] = p_ref[src, :]
        _, l, acc = _merge_softmax(tuple(t_ref[u] for t_ref in tmp), triple)
        o_ref[0, pl.ds(q0, blk), :] = (acc / l).astype(o_ref.dtype)

    vp1[:, :LANES] = v_ref[0]
    attend(seq, q_ref.at[0], k_ref.at[0], finish_natural)


def _dil_attn(proj, *, attn_width, unroll=32):
    B, S, _ = proj.shape
    n_pairs = attn_width // LANES
    trips = jnp.full((1,), S // ATTN_BLK // unroll, jnp.int32)
    kern = functools.partial(_dil_attn_kernel, seq=S, patterns=DILATED_PATTERNS,
                             unroll=unroll)
    col = lambda off: (lambda b, hp: (b, 0, off + hp))
    f32_buf = pltpu.VMEM((S, LANES), F32)
    bf16_buf = pltpu.VMEM((S, LANES), BF16)
    tmp_buf = pltpu.VMEM((unroll, ATTN_BLK, LANES), F32)
    return pl.pallas_call(
        kern,
        grid=(B, n_pairs),
        in_specs=[pl.BlockSpec(memory_space=pltpu.SMEM),
                  pl.BlockSpec((1, S, LANES), col(0)),
                  pl.BlockSpec((1, S, LANES), col(n_pairs)),
                  pl.BlockSpec((1, S, LANES), col(2 * n_pairs))],
        out_specs=pl.BlockSpec((1, S, LANES), lambda b, hp: (b, 0, hp)),
        out_shape=jax.ShapeDtypeStruct((B, S, attn_width), BF16),
        scratch_shapes=[f32_buf, f32_buf, f32_buf,
                        f32_buf, f32_buf, f32_buf,
                        bf16_buf, bf16_buf,
                        pltpu.VMEM((S, 2 * LANES), BF16),
                        f32_buf, f32_buf, f32_buf,
                        tmp_buf, tmp_buf, tmp_buf,
                        pltpu.VMEM((2, ATTN_BLK, 2 * ATTN_BLK), F32)],
        compiler_params=pltpu.CompilerParams(
            dimension_semantics=("parallel", "parallel"),
            vmem_limit_bytes=VMEM_LIMIT_BYTES),
        name="dil_attn",
    )(trips, proj, proj, proj)


STAGE_ROWS, STAGE_COLS = 512, 1024


def _load_weights_as_bf16(pairs, stage, sems):
    tiles = []
    for w_hbm, w_vmem in pairs:
        n_rows, n_cols = w_hbm.shape
        rows, cols = min(n_rows, STAGE_ROWS), min(n_cols, STAGE_COLS)
        assert n_rows % rows == 0 and n_cols % cols == 0
        tiles += [(w_hbm, w_vmem, r0, c0, rows, cols)
                  for r0 in range(0, n_rows, rows) for c0 in range(0, n_cols, cols)]

    def copy(i):
        w_hbm, _, r0, c0, rows, cols = tiles[i]
        slot = i % 2
        return pltpu.make_async_copy(
            w_hbm.at[pl.ds(r0, rows), pl.ds(c0, cols)],
            stage.at[slot, pl.ds(0, rows), pl.ds(0, cols)], sems.at[slot])

    copy(0).start()
    for i, (_, w_vmem, r0, c0, rows, cols) in enumerate(tiles):
        if i + 1 < len(tiles):
            copy(i + 1).start()
        copy(i).wait()
        w_vmem[pl.ds(r0, rows), pl.ds(c0, cols)] = stage[i % 2, :rows, :cols].astype(BF16)


def _mix_mlp_kernel(x_ref, ya_ref, bg_ref, cg_ref, u_ref, qx_ref, hc_ref, hu_ref,
                    mem_ref, g_mem_ref, w_mem_hbm, conv_w_ref,
                    g_attn_ref, g_conv_ref, g_xattn_ref, w_out_hbm, g_post_mix_ref,
                    g_pre_mlp_ref, w_up_hbm, w_down_hbm, g_post_mlp_ref,
                    o_ref, km_ref, vm_ref, w_mem_ref, w_out_ref, w_up_ref, w_down_ref,
                    stage_ref, sem_ref, *, ff_chunk):
    t = pl.program_id(1)
    xw = km_ref.shape[1]

    @pl.when((pl.program_id(0) == 0) & (t == 0))
    def _():
        _load_weights_as_bf16([(w_mem_hbm, w_mem_ref), (w_out_hbm, w_out_ref),
                               (w_up_hbm, w_up_ref), (w_down_hbm, w_down_ref)],
                              stage_ref, sem_ref)

    @pl.when(t == 0)
    def _():
        hm = _rms(mem_ref[0], g_mem_ref[...]).astype(BF16)
        kv = jnp.dot(hm, w_mem_ref[...], preferred_element_type=F32)
        km_ref[...] = kv[:, :xw].astype(BF16)
        vm_ref[...] = kv[:, xw:].astype(BF16)

    tm = x_ref.shape[1]

    z = cg_ref[0].astype(F32) * u_ref[0].astype(F32)
    hz = hc_ref[0].astype(F32) * hu_ref[0].astype(F32)
    hz = jnp.where(t > 0, hz, jnp.zeros_like(hz))
    z_ext = jnp.concatenate([hz, z], axis=0)
    cw = conv_w_ref[...]
    y_conv = z * cw[CONV_K - 1:CONV_K, :]
    for back in range(1, CONV_K):
        lo = BF16_SUBLANES - back
        y_conv = y_conv + z_ext[lo:lo + tm, :] * cw[CONV_K - 1 - back:CONV_K - back, :]
    y_conv = bg_ref[0].astype(F32) * y_conv

    qx = qx_ref[0]
    km = km_ref[...]
    vm = vm_ref[...]
    lane = lax.broadcasted_iota(jnp.int32, (1, xw), 1)
    xhd = xw // N_MEM_HEADS
    assert xhd == HEAD_DIM
    y_x = jnp.zeros((tm, xw), F32)
    for hd in range(N_MEM_HEADS):
        sel = (lane >= hd * xhd) & (lane < (hd + 1) * xhd)
        qh = jnp.where(sel, qx, jnp.zeros_like(qx))
        sc = lax.dot_general(qh, km, (((1,), (1,)), ((), ())),
                             preferred_element_type=F32)
        p = jnp.exp2(sc - jnp.max(sc, axis=-1, keepdims=True))
        p = p / jnp.sum(p, axis=-1, keepdims=True)
        o = jnp.dot(p.astype(BF16), vm, preferred_element_type=F32)
        y_x = jnp.where(sel, o, y_x)

    halves = ((0, tm // 2), (tm // 2, tm))

    x1_h, h2_h = [], []
    for r0, r1 in halves:
        y = jnp.concatenate([
            _rms(ya_ref[0, r0:r1, :].astype(F32), g_attn_ref[...]).astype(BF16),
            _rms(y_conv[r0:r1], g_conv_ref[...]).astype(BF16),
            _rms(y_x[r0:r1], g_xattn_ref[...]).astype(BF16)], axis=-1)
        y = jnp.dot(y, w_out_ref[...], preferred_element_type=F32)
        x1 = x_ref[0, r0:r1, :] + _rms(y, g_post_mix_ref[...])
        x1_h.append(x1)
        h2_h.append(_rms(x1, g_pre_mlp_ref[...]).astype(BF16))

    def act_fn(up):
        return jnp.square(jnp.maximum(up, 0.0)).astype(BF16)

    chunks = list(range(0, w_up_ref.shape[1], ff_chunk))
    h2 = jnp.concatenate(h2_h, axis=0)
    acc = None
    for f0 in chunks:
        w_up_c = w_up_ref[:, f0:f0 + ff_chunk]
        if f0 == chunks[0]:
            act = jnp.concatenate(
                [act_fn(jnp.dot(h, w_up_c, preferred_element_type=F32)) for h in h2_h],
                axis=0)
        else:
            act = act_fn(jnp.dot(h2, w_up_c, preferred_element_type=F32))
        w_down_c = w_down_ref[f0:f0 + ff_chunk, :]
        if f0 != chunks[-1]:
            part = jnp.dot(act, w_down_c, preferred_element_type=F32)
            acc = part if acc is None else acc + part
        else:
            for (r0, r1), x1 in zip(halves, x1_h):
                f = acc[r0:r1] + jnp.dot(act[r0:r1], w_down_c, preferred_element_type=F32)
                o_ref[0, r0:r1, :] = x1 + _rms(f, g_post_mlp_ref[...])


def _mix_mlp(x, y_attn, proj, mem, g_mem, w_mem, conv_w, g_attn, g_conv, g_xattn,
             w_out, g_post_mix, g_pre_mlp, w_up, w_down, g_post_mlp,
             *, attn_width, conv_width, xattn_width, tm=512, ff_chunk=1024):
    B, S, D = x.shape
    n_mem = mem.shape[1]
    d_ff = w_up.shape[1]
    assert conv_width == xattn_width and (3 * attn_width) % conv_width == 0
    cb0 = 3 * attn_width // conv_width
    halo = BF16_SUBLANES
    const = lambda shape: pl.BlockSpec(shape, lambda b, t: (0,) * len(shape),
                                       pipeline_mode=pl.Buffered(1))
    hbm = pl.BlockSpec(memory_space=pl.ANY)
    pcol = lambda cb: pl.BlockSpec((1, tm, conv_width), lambda b, t: (b, t, cb))
    phalo = lambda cb: pl.BlockSpec(
        (1, halo, conv_width),
        lambda b, t: (b, jnp.maximum(t * (tm // halo) - 1, 0), cb))
    return pl.pallas_call(
        functools.partial(_mix_mlp_kernel, ff_chunk=ff_chunk),
        grid=(B, S // tm),
        in_specs=[
            pl.BlockSpec((1, tm, D), lambda b, t: (b, t, 0)),
            pl.BlockSpec((1, tm, attn_width), lambda b, t: (b, t, 0)),
            pcol(cb0), pcol(cb0 + 1), pcol(cb0 + 2), pcol(cb0 + 3),
            phalo(cb0 + 1), phalo(cb0 + 2),
            pl.BlockSpec((1, n_mem, D), lambda b, t: (b, 0, 0)),
            const((1, D)), hbm, const((CONV_K, conv_width)),
            const((1, attn_width)), const((1, conv_width)), const((1, xattn_width)),
            hbm, const((1, D)),
            const((1, D)), hbm, hbm, const((1, D)),
        ],
        out_specs=pl.BlockSpec((1, tm, D), lambda b, t: (b, t, 0)),
        out_shape=jax.ShapeDtypeStruct((B, S, D), x.dtype),
        scratch_shapes=[pltpu.VMEM((n_mem, xattn_width), BF16),
                        pltpu.VMEM((n_mem, xattn_width), BF16),
                        pltpu.VMEM(w_mem.shape, BF16),
                        pltpu.VMEM(w_out.shape, BF16),
                        pltpu.VMEM(w_up.shape, BF16),
                        pltpu.VMEM(w_down.shape, BF16),
                        pltpu.VMEM((2, STAGE_ROWS, STAGE_COLS), F32),
                        pltpu.SemaphoreType.DMA((2,))],
        compiler_params=pltpu.CompilerParams(
            dimension_semantics=("arbitrary", "arbitrary"),
            vmem_limit_bytes=VMEM_LIMIT_BYTES),
        name="mix_mlp",
    )(x, y_attn, proj, proj, proj, proj, proj, proj, mem, g_mem, w_mem, conv_w,
      g_attn, g_conv, g_xattn, w_out, g_post_mix, g_pre_mlp, w_up, w_down, g_post_mlp)


def kernel(x, mem, positions, g_pre_mix, g_mem, w_in, w_mem_kv, conv_w, g_attn_out,
           g_conv_out, g_xattn_out, w_out, g_post_mix, g_pre_mlp, w_up, w_down,
           g_post_mlp):
    depth = w_in.shape[0]
    attn_width = g_attn_out.shape[1]
    conv_width = g_conv_out.shape[1]
    xattn_width = g_xattn_out.shape[1]
    pos3 = positions[:, None, :]
    row = lambda g: g[None, :]
    for l in range(depth):
        proj = _in_proj(x, pos3, row(g_pre_mix[l]), w_in[l].astype(BF16),
                        attn_width=attn_width, xattn_width=xattn_width)
        y_attn = _dil_attn(proj, attn_width=attn_width)
        x = _mix_mlp(x, y_attn, proj, mem, row(g_mem[l]), w_mem_kv[l], conv_w[l],
                     row(g_attn_out[l]), row(g_conv_out[l]), row(g_xattn_out[l]),
                     w_out[l], row(g_post_mix[l]), row(g_pre_mlp[l]), w_up[l], w_down[l],
                     row(g_post_mlp[l]),
                     attn_width=attn_width, conv_width=conv_width,
                     xattn_width=xattn_width)
    return x
```

```python
import functools

import jax
import jax.numpy as jnp
from jax import lax
from jax.experimental import pallas as pl
from jax.experimental.pallas import tpu as pltpu

F32 = jnp.float32
BF16 = jnp.bfloat16

HEAD_DIM = 64
N_MEM_HEADS = 4
DILATED_PATTERNS = ((128, 1), (512, 4), (2048, 16))
CONV_K = 3
ROPE_THETA = 10000.0
EPS = 1e-6
NEG_INF = -1e30
LOG2_E = 1.4426950408889634

LANES = 128
BF16_SUBLANES = 16
ATTN_BLK = 128
VMEM_LIMIT_BYTES = 56 * 1024 * 1024


def _rms(x, g):
    return x * lax.rsqrt(jnp.mean(x * x, axis=-1, keepdims=True) + EPS) * g


def _in_proj_kernel(x_ref, pos_ref, g_ref, w_ref, o_ref, *, attn_width, xattn_width, chunk):
    h = _rms(x_ref[0], g_ref[...]).astype(BF16)

    half = HEAD_DIM // 2
    freq = lax.broadcasted_iota(jnp.int32, (half, 1), 0).astype(F32)
    inv_freq = jnp.float32(ROPE_THETA) ** (-(freq * 2.0 / HEAD_DIM))
    ang = inv_freq * pos_ref[0].astype(F32)
    cos_t, sin_t = jnp.cos(ang), jnp.sin(ang)
    reps = LANES // HEAD_DIM
    cos_k = jnp.concatenate([cos_t, cos_t] * reps, axis=0).T
    sin_k = jnp.concatenate([-sin_t, sin_t] * reps, axis=0).T
    q_scale = HEAD_DIM ** -0.5 * LOG2_E
    cos_q, sin_q = cos_k * q_scale, sin_k * q_scale
    lane = lax.broadcasted_iota(jnp.int32, (1, LANES), 1)
    first_half = (lane % HEAD_DIM) < half

    n_out = o_ref.shape[-1]
    for c0 in range(0, n_out, chunk):
        p = jnp.dot(h, w_ref[:, c0:c0 + chunk], preferred_element_type=F32)
        if c0 < 2 * attn_width:
            cos, sin = (cos_q, sin_q) if c0 < attn_width else (cos_k, sin_k)
            for g0 in range(0, chunk, LANES):
                t = p[:, g0:g0 + LANES]
                rot = jnp.where(first_half,
                                pltpu.roll(t, LANES - half, 1),
                                pltpu.roll(t, half, 1))
                r = t * cos + rot * sin
                o_ref[0, :, c0 + g0:c0 + g0 + LANES] = r.astype(BF16)
        elif c0 + chunk == n_out:
            lo = chunk - xattn_width
            o_ref[0, :, c0:c0 + lo] = p[:, :lo].astype(BF16)
            o_ref[0, :, c0 + lo:c0 + chunk] = (p[:, lo:] * q_scale).astype(BF16)
        else:
            o_ref[0, :, c0:c0 + chunk] = p.astype(BF16)


def _in_proj(x, pos3, g, w_bf16, *, attn_width, xattn_width, tm=512, chunk=512):
    B, S, D = x.shape
    n_out = w_bf16.shape[1]
    return pl.pallas_call(
        functools.partial(_in_proj_kernel, attn_width=attn_width,
                          xattn_width=xattn_width, chunk=chunk),
        grid=(B, S // tm),
        in_specs=[
            pl.BlockSpec((1, tm, D), lambda b, t: (b, t, 0)),
            pl.BlockSpec((1, 1, tm), lambda b, t: (b, 0, t)),
            pl.BlockSpec((1, D), lambda b, t: (0, 0)),
            pl.BlockSpec((D, n_out), lambda b, t: (0, 0)),
        ],
        out_specs=pl.BlockSpec((1, tm, n_out), lambda b, t: (b, t, 0)),
        out_shape=jax.ShapeDtypeStruct((B, S, n_out), BF16),
        compiler_params=pltpu.CompilerParams(
            dimension_semantics=("parallel", "parallel"),
            vmem_limit_bytes=VMEM_LIMIT_BYTES),
        name="in_proj",
    )(x, pos3, g, w_bf16)


def _band_block(qb, kb, vb1, bias, lane_lo):
    ms, ls, accs = [], [], []
    for head_lo in (True, False):
        sel = lane_lo if head_lo else jnp.logical_not(lane_lo)
        qh = jnp.where(sel, qb, jnp.zeros_like(qb))
        s = lax.dot_general(qh, kb, (((1,), (1,)), ((), ())),
                            preferred_element_type=F32) + bias
        m = jnp.max(s, axis=-1, keepdims=True)
        p = jnp.exp2(s - m).astype(BF16)
        r = jnp.dot(p, vb1, preferred_element_type=F32)
        accs.append(r[:, :LANES])
        ls.append(r[:, LANES:])
        ms.append(jnp.broadcast_to(m, (m.shape[0], LANES)))
    return (jnp.where(lane_lo, ms[0], ms[1]), jnp.where(lane_lo, ls[0], ls[1]),
            jnp.where(lane_lo, accs[0], accs[1]))


COPY_ROWS = 64
DEINTERLEAVE = 4


def _for_each_split(seq, region, body):
    run = region // DEINTERLEAVE
    span = DEINTERLEAVE * COPY_ROWS
    steps_per_region = region // span

    def step(t, carry):
        strided0 = pl.multiple_of(t * span, span)
        dense0 = pl.multiple_of((t // steps_per_region) * region
                                + (t % steps_per_region) * COPY_ROWS, COPY_ROWS)
        for j in range(DEINTERLEAVE):
            body(pl.ds(strided0 + j, COPY_ROWS, stride=DEINTERLEAVE),
                 pl.ds(dense0 + j * run, COPY_ROWS))
        return carry

    lax.fori_loop(0, seq // span, step, 0)


def _merge_softmax(a, b):
    (m_a, l_a, a_a), (m_b, l_b, a_b) = a, b
    m = jnp.maximum(m_a, m_b)
    w_a = jnp.exp2(m_a - m)
    w_b = jnp.exp2(m_b - m)
    return m, l_a * w_a + l_b * w_b, a_a * w_a + a_b * w_b


def _dil_attn_kernel(trips_ref, q_ref, k_ref, v_ref, o_ref,
                     qf, kf, vf, q4f, k4f, v4f, qp, kp, vp1,
                     m_p4, l_p4, a_p4, m_tmp, l_tmp, a_tmp, bias_ref,
                     *, seq, patterns, unroll):
    blk = ATTN_BLK
    n_blocks = seq // blk
    lane_lo = lax.broadcasted_iota(jnp.int32, (1, LANES), 1) < HEAD_DIM
    assert [d for _, d in patterns] == [1, DEINTERLEAVE, DEINTERLEAVE ** 2]
    assert all(w // d == blk for w, d in patterns)
    run4 = seq // DEINTERLEAVE
    run16 = run4 // DEINTERLEAVE
    blocks_per_run4 = run4 // blk
    blocks_per_run16 = run16 // blk
    assert n_blocks % unroll == 0 and unroll % blocks_per_run4 == 0
    n_trips = trips_ref[0]

    qi = lax.broadcasted_iota(jnp.int32, (blk, 2 * blk), 0)
    kj = lax.broadcasted_iota(jnp.int32, (blk, 2 * blk), 1)
    bias_ref[0] = jnp.where(kj <= qi, 0.0, NEG_INF).astype(F32)
    bias_ref[1] = jnp.where((kj >= qi) & (kj <= qi + blk), 0.0, NEG_INF).astype(F32)
    vp1[:, LANES:] = jnp.ones((seq, LANES), BF16)

    pat4 = (m_p4, l_p4, a_p4)
    tmp = (m_tmp, l_tmp, a_tmp)

    def attend(sub_len, q_src, k_src, finish):
        nb = sub_len // blk
        assert nb >= 2

        def trip_body(trip, carry):
            for u in range(unroll):
                g = trip * unroll + u
                n = g % nb
                q0 = pl.multiple_of(g * blk, blk)
                k0 = pl.multiple_of(jnp.where(n > 0, q0 - blk, q0), blk)
                triple = _band_block(q_src[pl.ds(q0, blk), :],
                                     k_src[pl.ds(k0, 2 * blk), :],
                                     vp1[pl.ds(k0, 2 * blk), :],
                                     bias_ref[jnp.minimum(n, 1)], lane_lo)
                finish(trip, u, q0, triple)
            return carry

        lax.fori_loop(0, n_trips, trip_body, 0)

    qf[...] = q_ref[0].astype(F32)
    kf[...] = k_ref[0].astype(F32)
    vf[...] = v_ref[0].astype(F32)

    def gather4(strided, dense):
        for src, dst_f, dst_b in ((qf, q4f, qp), (kf, k4f, kp)):
            rows = src[strided, :]
            dst_f[dense, :] = rows
            dst_b[dense, :] = rows.astype(BF16)
        rows = vf[strided, :]
        v4f[dense, :] = rows
        vp1[dense, :LANES] = rows.astype(BF16)

    def store_pat4(trip, u, q0, triple):
        for ref, val in zip(pat4, triple):
            ref[pl.ds(q0, blk), :] = val

    _for_each_split(seq, seq, gather4)
    attend(run4, qp, kp, store_pat4)

    def gather16(strided, dense):
        qp[dense, :] = q4f[strided, :].astype(BF16)
        kp[dense, :] = k4f[strided, :].astype(BF16)
        vp1[dense, :LANES] = v4f[strided, :].astype(BF16)

    def fold_into_pat4(trip, u, q0, triple):
        r_static, idx = divmod(u, blocks_per_run4)
        j, a0 = idx // blocks_per_run16, (idx % blocks_per_run16) * blk
        r_dyn = pl.multiple_of(trip * (unroll // blocks_per_run4) * run4, run4)
        rows = pl.ds(r_dyn + r_static * run4 + j + DEINTERLEAVE * a0, blk,
                     stride=DEINTERLEAVE)
        merged = _merge_softmax(tuple(ref[rows, :] for ref in pat4), triple)
        for ref, val in zip(pat4, merged):
            ref[rows, :] = val

    _for_each_split(seq, run4, gather16)
    attend(run16, qp, kp, fold_into_pat4)

    def finish_natural(trip, u, q0, triple):
        sub = blk // DEINTERLEAVE
        i_dyn = pl.multiple_of(trip * (unroll * sub), unroll * sub)
        for j in range(DEINTERLEAVE):
            src = pl.ds(i_dyn + j * run4 + u * sub, sub)
            for t_ref, p_ref in zip(tmp, pat4):
                t_ref[u, pl.ds(j, sub, stride=DEINTERLEAVE), :] = p_ref[src, :]
        _, l, acc = _merge_softmax(tuple(t_ref[u] for t_ref in tmp), triple)
        o_ref[0, pl.ds(q0, blk), :] = (acc / l).astype(o_ref.dtype)

    vp1[:, :LANES] = v_ref[0]
    attend(seq, q_ref.at[0], k_ref.at[0], finish_natural)


def _dil_attn(proj, *, attn_width, unroll=32):
    B, S, _ = proj.shape
    n_pairs = attn_width // LANES
    trips = jnp.full((1,), S // ATTN_BLK // unroll, jnp.int32)
    kern = functools.partial(_dil_attn_kernel, seq=S, patterns=DILATED_PATTERNS,
                             unroll=unroll)
    col = lambda off: (lambda b, hp: (b, 0, off + hp))
    f32_buf = pltpu.VMEM((S, LANES), F32)
    bf16_buf = pltpu.VMEM((S, LANES), BF16)
    tmp_buf = pltpu.VMEM((unroll, ATTN_BLK, LANES), F32)
    return pl.pallas_call(
        kern,
        grid=(B, n_pairs),
        in_specs=[pl.BlockSpec(memory_space=pltpu.SMEM),
                  pl.BlockSpec((1, S, LANES), col(0)),
                  pl.BlockSpec((1, S, LANES), col(n_pairs)),
                  pl.BlockSpec((1, S, LANES), col(2 * n_pairs))],
        out_specs=pl.BlockSpec((1, S, LANES), lambda b, hp: (b, 0, hp)),
        out_shape=jax.ShapeDtypeStruct((B, S, attn_width), BF16),
        scratch_shapes=[f32_buf, f32_buf, f32_buf,
                        f32_buf, f32_buf, f32_buf,
                        bf16_buf, bf16_buf,
                        pltpu.VMEM((S, 2 * LANES), BF16),
                        f32_buf, f32_buf, f32_buf,
                        tmp_buf, tmp_buf, tmp_buf,
                        pltpu.VMEM((2, ATTN_BLK, 2 * ATTN_BLK), F32)],
        compiler_params=pltpu.CompilerParams(
            dimension_semantics=("parallel", "parallel"),
            vmem_limit_bytes=VMEM_LIMIT_BYTES),
        name="dil_attn",
    )(trips, proj, proj, proj)


STAGE_ROWS, STAGE_COLS = 512, 1024


def _load_weights_as_bf16(pairs, stage, sems):
    tiles = []
    for w_hbm, w_vmem in pairs:
        n_rows, n_cols = w_hbm.shape
        rows, cols = min(n_rows, STAGE_ROWS), min(n_cols, STAGE_COLS)
        assert n_rows % rows == 0 and n_cols % cols == 0
        tiles += [(w_hbm, w_vmem, r0, c0, rows, cols)
                  for r0 in range(0, n_rows, rows) for c0 in range(0, n_cols, cols)]

    def copy(i):
        w_hbm, _, r0, c0, rows, cols = tiles[i]
        slot = i % 2
        return pltpu.make_async_copy(
            w_hbm.at[pl.ds(r0, rows), pl.ds(c0, cols)],
            stage.at[slot, pl.ds(0, rows), pl.ds(0, cols)], sems.at[slot])

    copy(0).start()
    for i, (_, w_vmem, r0, c0, rows, cols) in enumerate(tiles):
        if i + 1 < len(tiles):
            copy(i + 1).start()
        copy(i).wait()
        w_vmem[pl.ds(r0, rows), pl.ds(c0, cols)] = stage[i % 2, :rows, :cols].astype(BF16)


def _mix_mlp_kernel(x_ref, ya_ref, bg_ref, cg_ref, u_ref, qx_ref, hc_ref, hu_ref,
                    mem_ref, g_mem_ref, w_mem_hbm, conv_w_ref,
                    g_attn_ref, g_conv_ref, g_xattn_ref, w_out_hbm, g_post_mix_ref,
                    g_pre_mlp_ref, w_up_hbm, w_down_hbm, g_post_mlp_ref,
                    o_ref, km_ref, vm_ref, w_mem_ref, w_out_ref, w_up_ref, w_down_ref,
                    stage_ref, sem_ref, *, ff_chunk):
    t = pl.program_id(1)
    xw = km_ref.shape[1]

    @pl.when((pl.program_id(0) == 0) & (t == 0))
    def _():
        _load_weights_as_bf16([(w_mem_hbm, w_mem_ref), (w_out_hbm, w_out_ref),
                               (w_up_hbm, w_up_ref), (w_down_hbm, w_down_ref)],
                              stage_ref, sem_ref)

    @pl.when(t == 0)
    def _():
        hm = _rms(mem_ref[0], g_mem_ref[...]).astype(BF16)
        kv = jnp.dot(hm, w_mem_ref[...], preferred_element_type=F32)
        km_ref[...] = kv[:, :xw].astype(BF16)
        vm_ref[...] = kv[:, xw:].astype(BF16)

    tm = x_ref.shape[1]

    z = cg_ref[0].astype(F32) * u_ref[0].astype(F32)
    hz = hc_ref[0].astype(F32) * hu_ref[0].astype(F32)
    hz = jnp.where(t > 0, hz, jnp.zeros_like(hz))
    z_ext = jnp.concatenate([hz, z], axis=0)
    cw = conv_w_ref[...]
    y_conv = z * cw[CONV_K - 1:CONV_K, :]
    for back in range(1, CONV_K):
        lo = BF16_SUBLANES - back
        y_conv = y_conv + z_ext[lo:lo + tm, :] * cw[CONV_K - 1 - back:CONV_K - back, :]
    y_conv = bg_ref[0].astype(F32) * y_conv

    qx = qx_ref[0]
    km = km_ref[...]
    vm = vm_ref[...]
    lane = lax.broadcasted_iota(jnp.int32, (1, xw), 1)
    xhd = xw // N_MEM_HEADS
    assert xhd == HEAD_DIM
    sels = [(lane >= hd * xhd) & (lane < (hd + 1) * xhd) for hd in range(N_MEM_HEADS)]

    def scores(hd):
        qh = jnp.where(sels[hd], qx, jnp.zeros_like(qx))
        return lax.dot_general(qh, km, (((1,), (1,)), ((), ())),
                               preferred_element_type=F32)

    def softmax(sc):
        p = jnp.exp2(sc - jnp.max(sc, axis=-1, keepdims=True))
        return (p / jnp.sum(p, axis=-1, keepdims=True)).astype(BF16)

    sc, p, y_x = {}, {}, jnp.zeros((tm, xw), F32)
    for hd in range(N_MEM_HEADS + 2):
        if hd < N_MEM_HEADS:
            sc[hd] = scores(hd)
        if 0 <= hd - 1 < N_MEM_HEADS:
            p[hd - 1] = softmax(sc.pop(hd - 1))
        if 0 <= hd - 2 < N_MEM_HEADS:
            o = jnp.dot(p.pop(hd - 2), vm, preferred_element_type=F32)
            y_x = jnp.where(sels[hd - 2], o, y_x)

    halves = ((0, tm // 2), (tm // 2, tm))

    x1_h, h2_h = [], []
    for r0, r1 in halves:
        y = jnp.concatenate([
            _rms(ya_ref[0, r0:r1, :].astype(F32), g_attn_ref[...]).astype(BF16),
            _rms(y_conv[r0:r1], g_conv_ref[...]).astype(BF16),
            _rms(y_x[r0:r1], g_xattn_ref[...]).astype(BF16)], axis=-1)
        y = jnp.dot(y, w_out_ref[...], preferred_element_type=F32)
        x1 = x_ref[0, r0:r1, :] + _rms(y, g_post_mix_ref[...])
        x1_h.append(x1)
        h2_h.append(_rms(x1, g_pre_mlp_ref[...]).astype(BF16))

    def act_fn(up):
        return jnp.square(jnp.maximum(up, 0.0)).astype(BF16)

    chunks = list(range(0, w_up_ref.shape[1], ff_chunk))
    h2 = jnp.concatenate(h2_h, axis=0)
    acc = None
    for f0 in chunks:
        w_up_c = w_up_ref[:, f0:f0 + ff_chunk]
        if f0 == chunks[0]:
            act = jnp.concatenate(
                [act_fn(jnp.dot(h, w_up_c, preferred_element_type=F32)) for h in h2_h],
                axis=0)
        else:
            act = act_fn(jnp.dot(h2, w_up_c, preferred_element_type=F32))
        w_down_c = w_down_ref[f0:f0 + ff_chunk, :]
        if f0 != chunks[-1]:
            part = jnp.dot(act, w_down_c, preferred_element_type=F32)
            acc = part if acc is None else acc + part
        else:
            for (r0, r1), x1 in zip(halves, x1_h):
                f = acc[r0:r1] + jnp.dot(act[r0:r1], w_down_c, preferred_element_type=F32)
                o_ref[0, r0:r1, :] = x1 + _rms(f, g_post_mlp_ref[...])


def _mix_mlp(x, y_attn, proj, mem, g_mem, w_mem, conv_w, g_attn, g_conv, g_xattn,
             w_out, g_post_mix, g_pre_mlp, w_up, w_down, g_post_mlp,
             *, attn_width, conv_width, xattn_width, tm=512, ff_chunk=1024):
    B, S, D = x.shape
    n_mem = mem.shape[1]
    d_ff = w_up.shape[1]
    assert conv_width == xattn_width and (3 * attn_width) % conv_width == 0
    cb0 = 3 * attn_width // conv_width
    halo = BF16_SUBLANES
    const = lambda shape: pl.BlockSpec(shape, lambda b, t: (0,) * len(shape),
                                       pipeline_mode=pl.Buffered(1))
    hbm = pl.BlockSpec(memory_space=pl.ANY)
    pcol = lambda cb: pl.BlockSpec((1, tm, conv_width), lambda b, t: (b, t, cb))
    phalo = lambda cb: pl.BlockSpec(
        (1, halo, conv_width),
        lambda b, t: (b, jnp.maximum(t * (tm // halo) - 1, 0), cb))
    return pl.pallas_call(
        functools.partial(_mix_mlp_kernel, ff_chunk=ff_chunk),
        grid=(B, S // tm),
        in_specs=[
            pl.BlockSpec((1, tm, D), lambda b, t: (b, t, 0)),
            pl.BlockSpec((1, tm, attn_width), lambda b, t: (b, t, 0)),
            pcol(cb0), pcol(cb0 + 1), pcol(cb0 + 2), pcol(cb0 + 3),
            phalo(cb0 + 1), phalo(cb0 + 2),
            pl.BlockSpec((1, n_mem, D), lambda b, t: (b, 0, 0)),
            const((1, D)), hbm, const((CONV_K, conv_width)),
            const((1, attn_width)), const((1, conv_width)), const((1, xattn_width)),
            hbm, const((1, D)),
            const((1, D)), hbm, hbm, const((1, D)),
        ],
        out_specs=pl.BlockSpec((1, tm, D), lambda b, t: (b, t, 0)),
        out_shape=jax.ShapeDtypeStruct((B, S, D), x.dtype),
        scratch_shapes=[pltpu.VMEM((n_mem, xattn_width), BF16),
                        pltpu.VMEM((n_mem, xattn_width), BF16),
                        pltpu.VMEM(w_mem.shape, BF16),
                        pltpu.VMEM(w_out.shape, BF16),
                        pltpu.VMEM(w_up.shape, BF16),
                        pltpu.VMEM(w_down.shape, BF16),
                        pltpu.VMEM((2, STAGE_ROWS, STAGE_COLS), F32),
                        pltpu.SemaphoreType.DMA((2,))],
        compiler_params=pltpu.CompilerParams(
            dimension_semantics=("arbitrary", "arbitrary"),
            vmem_limit_bytes=VMEM_LIMIT_BYTES),
        name="mix_mlp",
    )(x, y_attn, proj, proj, proj, proj, proj, proj, mem, g_mem, w_mem, conv_w,
      g_attn, g_conv, g_xattn, w_out, g_post_mix, g_pre_mlp, w_up, w_down, g_post_mlp)


def kernel(x, mem, positions, g_pre_mix, g_mem, w_in, w_mem_kv, conv_w, g_attn_out,
           g_conv_out, g_xattn_out, w_out, g_post_mix, g_pre_mlp, w_up, w_down,
           g_post_mlp):
    depth = w_in.shape[0]
    attn_width = g_attn_out.shape[1]
    conv_width = g_conv_out.shape[1]
    xattn_width = g_xattn_out.shape[1]
    pos3 = positions[:, None, :]
    row = lambda g: g[None, :]
    for l in range(depth):
        proj = _in_proj(x, pos3, row(g_pre_mix[l]), w_in[l].astype(BF16),
                        attn_width=attn_width, xattn_width=xattn_width)
        y_attn = _dil_attn(proj, attn_width=attn_width)
        x = _mix_mlp(x, y_attn, proj, mem, row(g_mem[l]), w_mem_kv[l], conv_w[l],
                     row(g_attn_out[l]), row(g_conv_out[l]), row(g_xattn_out[l]),
                     w_out[l], row(g_post_mix[l]), row(g_pre_mlp[l]), w_up[l], w_down[l],
                     row(g_post_mlp[l]),
                     attn_width=attn_width, conv_width=conv_width,
                     xattn_width=xattn_width)
    return x
```

```python
import functools

import jax
import jax.numpy as jnp
from jax import lax
from jax.experimental import pallas as pl
from jax.experimental.pallas import tpu as pltpu

F32 = jnp.float32
BF16 = jnp.bfloat16

HEAD_DIM = 64
N_MEM_HEADS = 4
DILATED_PATTERNS = ((128, 1), (512, 4), (2048, 16))
CONV_K = 3
ROPE_THETA = 10000.0
EPS = 1e-6
NEG_INF = -1e30
LOG2_E = 1.4426950408889634

LANES = 128
BF16_SUBLANES = 16
ATTN_BLK = 128
VMEM_LIMIT_BYTES = 56 * 1024 * 1024


def _rms(x, g):
    return x * lax.rsqrt(jnp.mean(x * x, axis=-1, keepdims=True) + EPS) * g


def _in_proj_kernel(x_ref, pos_ref, g_ref, w_ref, o_ref, *, attn_width, xattn_width, chunk):
    h = _rms(x_ref[0], g_ref[...]).astype(BF16)

    half = HEAD_DIM // 2
    freq = lax.broadcasted_iota(jnp.int32, (half, 1), 0).astype(F32)
    inv_freq = jnp.float32(ROPE_THETA) ** (-(freq * 2.0 / HEAD_DIM))
    ang = inv_freq * pos_ref[0].astype(F32)
    cos_t, sin_t = jnp.cos(ang), jnp.sin(ang)
    reps = LANES // HEAD_DIM
    cos_k = jnp.concatenate([cos_t, cos_t] * reps, axis=0).T
    sin_k = jnp.concatenate([-sin_t, sin_t] * reps, axis=0).T
    q_scale = HEAD_DIM ** -0.5 * LOG2_E
    cos_q, sin_q = cos_k * q_scale, sin_k * q_scale
    lane = lax.broadcasted_iota(jnp.int32, (1, LANES), 1)
    first_half = (lane % HEAD_DIM) < half

    n_out = o_ref.shape[-1]
    for c0 in range(0, n_out, chunk):
        p = jnp.dot(h, w_ref[:, c0:c0 + chunk], preferred_element_type=F32)
        if c0 < 2 * attn_width:
            cos, sin = (cos_q, sin_q) if c0 < attn_width else (cos_k, sin_k)
            for g0 in range(0, chunk, LANES):
                t = p[:, g0:g0 + LANES]
                rot = jnp.where(first_half,
                                pltpu.roll(t, LANES - half, 1),
                                pltpu.roll(t, half, 1))
                r = t * cos + rot * sin
                o_ref[0, :, c0 + g0:c0 + g0 + LANES] = r.astype(BF16)
        elif c0 + chunk == n_out:
            lo = chunk - xattn_width
            o_ref[0, :, c0:c0 + lo] = p[:, :lo].astype(BF16)
            o_ref[0, :, c0 + lo:c0 + chunk] = (p[:, lo:] * q_scale).astype(BF16)
        else:
            o_ref[0, :, c0:c0 + chunk] = p.astype(BF16)


def _in_proj(x, pos3, g, w_bf16, *, attn_width, xattn_width, tm=512, chunk=512):
    B, S, D = x.shape
    n_out = w_bf16.shape[1]
    return pl.pallas_call(
        functools.partial(_in_proj_kernel, attn_width=attn_width,
                          xattn_width=xattn_width, chunk=chunk),
        grid=(B, S // tm),
        in_specs=[
            pl.BlockSpec((1, tm, D), lambda b, t: (b, t, 0)),
            pl.BlockSpec((1, 1, tm), lambda b, t: (b, 0, t)),
            pl.BlockSpec((1, D), lambda b, t: (0, 0)),
            pl.BlockSpec((D, n_out), lambda b, t: (0, 0)),
        ],
        out_specs=pl.BlockSpec((1, tm, n_out), lambda b, t: (b, t, 0)),
        out_shape=jax.ShapeDtypeStruct((B, S, n_out), BF16),
        compiler_params=pltpu.CompilerParams(
            dimension_semantics=("parallel", "parallel"),
            vmem_limit_bytes=VMEM_LIMIT_BYTES),
        name="in_proj",
    )(x, pos3, g, w_bf16)


def _band_scores(qb, kb, bias, sel):
    qh = jnp.where(sel, qb, jnp.zeros_like(qb))
    return lax.dot_general(qh, kb, (((1,), (1,)), ((), ())),
                           preferred_element_type=F32) + bias


def _band_probs(s):
    m = jnp.max(s, axis=-1, keepdims=True)
    return jnp.exp2(s - m).astype(BF16), jnp.broadcast_to(m, (m.shape[0], LANES))


def _band_values(p, vb1):
    r = jnp.dot(p, vb1, preferred_element_type=F32)
    return r[:, :LANES], r[:, LANES:]


COPY_ROWS = 64
DEINTERLEAVE = 4


def _for_each_split(seq, region, body):
    run = region // DEINTERLEAVE
    span = DEINTERLEAVE * COPY_ROWS
    steps_per_region = region // span

    def step(t, carry):
        strided0 = pl.multiple_of(t * span, span)
        dense0 = pl.multiple_of((t // steps_per_region) * region
                                + (t % steps_per_region) * COPY_ROWS, COPY_ROWS)
        for j in range(DEINTERLEAVE):
            body(pl.ds(strided0 + j, COPY_ROWS, stride=DEINTERLEAVE),
                 pl.ds(dense0 + j * run, COPY_ROWS))
        return carry

    lax.fori_loop(0, seq // span, step, 0)


def _merge_softmax(a, b):
    (m_a, l_a, a_a), (m_b, l_b, a_b) = a, b
    m = jnp.maximum(m_a, m_b)
    w_a = jnp.exp2(m_a - m)
    w_b = jnp.exp2(m_b - m)
    return m, l_a * w_a + l_b * w_b, a_a * w_a + a_b * w_b


def _dil_attn_kernel(trips_ref, q_ref, k_ref, v_ref, o_ref,
                     qf, kf, vf, q4f, k4f, v4f, qp, kp, vp1,
                     m_p4, l_p4, a_p4, m_tmp, l_tmp, a_tmp, bias_ref,
                     *, seq, patterns, unroll):
    blk = ATTN_BLK
    n_blocks = seq // blk
    lane_lo = lax.broadcasted_iota(jnp.int32, (1, LANES), 1) < HEAD_DIM
    assert [d for _, d in patterns] == [1, DEINTERLEAVE, DEINTERLEAVE ** 2]
    assert all(w // d == blk for w, d in patterns)
    run4 = seq // DEINTERLEAVE
    run16 = run4 // DEINTERLEAVE
    blocks_per_run4 = run4 // blk
    blocks_per_run16 = run16 // blk
    assert n_blocks % unroll == 0 and unroll % blocks_per_run4 == 0
    n_trips = trips_ref[0]

    qi = lax.broadcasted_iota(jnp.int32, (blk, 2 * blk), 0)
    kj = lax.broadcasted_iota(jnp.int32, (blk, 2 * blk), 1)
    bias_ref[0] = jnp.where(kj <= qi, 0.0, NEG_INF).astype(F32)
    bias_ref[1] = jnp.where((kj >= qi) & (kj <= qi + blk), 0.0, NEG_INF).astype(F32)
    vp1[:, LANES:] = jnp.ones((seq, LANES), BF16)

    pat4 = (m_p4, l_p4, a_p4)
    tmp = (m_tmp, l_tmp, a_tmp)

    def attend(sub_len, q_src, k_src, finish):
        nb = sub_len // blk
        assert nb >= 2

        def trip_body(trip, carry):
            def rows(u):
                g = trip * unroll + u
                n = g % nb
                q0 = pl.multiple_of(g * blk, blk)
                k0 = pl.multiple_of(jnp.where(n > 0, q0 - blk, q0), blk)
                return n, q0, k0

            items = [(u, lo) for u in range(unroll) for lo in (True, False)]
            s_q, p_q, done = {}, {}, {}
            for i in range(len(items) + 2):
                if i < len(items):
                    u, lo = items[i]
                    n, q0, k0 = rows(u)
                    s_q[i] = _band_scores(q_src[pl.ds(q0, blk), :],
                                          k_src[pl.ds(k0, 2 * blk), :],
                                          bias_ref[jnp.minimum(n, 1)],
                                          lane_lo if lo else jnp.logical_not(lane_lo))
                if 0 <= i - 1 < len(items):
                    p_q[i - 1] = _band_probs(s_q.pop(i - 1))
                if 0 <= i - 2 < len(items):
                    u, lo = items[i - 2]
                    _, q0, k0 = rows(u)
                    p, m = p_q.pop(i - 2)
                    acc, l = _band_values(p, vp1[pl.ds(k0, 2 * blk), :])
                    done[lo] = (m, l, acc)
                    if not lo:
                        triple = tuple(jnp.where(lane_lo, a, b)
                                       for a, b in zip(done[True], done[False]))
                        finish(trip, u, q0, triple)
            return carry

        lax.fori_loop(0, n_trips, trip_body, 0)

    qf[...] = q_ref[0].astype(F32)
    kf[...] = k_ref[0].astype(F32)
    vf[...] = v_ref[0].astype(F32)

    def gather4(strided, dense):
        for src, dst_f, dst_b in ((qf, q4f, qp), (kf, k4f, kp)):
            rows = src[strided, :]
            dst_f[dense, :] = rows
            dst_b[dense, :] = rows.astype(BF16)
        rows = vf[strided, :]
        v4f[dense, :] = rows
        vp1[dense, :LANES] = rows.astype(BF16)

    def store_pat4(trip, u, q0, triple):
        for ref, val in zip(pat4, triple):
            ref[pl.ds(q0, blk), :] = val

    _for_each_split(seq, seq, gather4)
    attend(run4, qp, kp, store_pat4)

    def gather16(strided, dense):
        qp[dense, :] = q4f[strided, :].astype(BF16)
        kp[dense, :] = k4f[strided, :].astype(BF16)
        vp1[dense, :LANES] = v4f[strided, :].astype(BF16)

    def fold_into_pat4(trip, u, q0, triple):
        r_static, idx = divmod(u, blocks_per_run4)
        j, a0 = idx // blocks_per_run16, (idx % blocks_per_run16) * blk
        r_dyn = pl.multiple_of(trip * (unroll // blocks_per_run4) * run4, run4)
        rows = pl.ds(r_dyn + r_static * run4 + j + DEINTERLEAVE * a0, blk,
                     stride=DEINTERLEAVE)
        merged = _merge_softmax(tuple(ref[rows, :] for ref in pat4), triple)
        for ref, val in zip(pat4, merged):
            ref[rows, :] = val

    _for_each_split(seq, run4, gather16)
    attend(run16, qp, kp, fold_into_pat4)

    def finish_natural(trip, u, q0, triple):
        sub = blk // DEINTERLEAVE
        i_dyn = pl.multiple_of(trip * (unroll * sub), unroll * sub)
        for j in range(DEINTERLEAVE):
            src = pl.ds(i_dyn + j * run4 + u * sub, sub)
            for t_ref, p_ref in zip(tmp, pat4):
                t_ref[u, pl.ds(j, sub, stride=DEINTERLEAVE), :] = p_ref[src, :]
        _, l, acc = _merge_softmax(tuple(t_ref[u] for t_ref in tmp), triple)
        o_ref[0, pl.ds(q0, blk), :] = (acc / l).astype(o_ref.dtype)

    vp1[:, :LANES] = v_ref[0]
    attend(seq, q_ref.at[0], k_ref.at[0], finish_natural)


def _dil_attn(proj, *, attn_width, unroll=32):
    B, S, _ = proj.shape
    n_pairs = attn_width // LANES
    trips = jnp.full((1,), S // ATTN_BLK // unroll, jnp.int32)
    kern = functools.partial(_dil_attn_kernel, seq=S, patterns=DILATED_PATTERNS,
                             unroll=unroll)
    col = lambda off: (lambda b, hp: (b, 0, off + hp))
    f32_buf = pltpu.VMEM((S, LANES), F32)
    bf16_buf = pltpu.VMEM((S, LANES), BF16)
    tmp_buf = pltpu.VMEM((unroll, ATTN_BLK, LANES), F32)
    return pl.pallas_call(
        kern,
        grid=(B, n_pairs),
        in_specs=[pl.BlockSpec(memory_space=pltpu.SMEM),
                  pl.BlockSpec((1, S, LANES), col(0)),
                  pl.BlockSpec((1, S, LANES), col(n_pairs)),
                  pl.BlockSpec((1, S, LANES), col(2 * n_pairs))],
        out_specs=pl.BlockSpec((1, S, LANES), lambda b, hp: (b, 0, hp)),
        out_shape=jax.ShapeDtypeStruct((B, S, attn_width), BF16),
        scratch_shapes=[f32_buf, f32_buf, f32_buf,
                        f32_buf, f32_buf, f32_buf,
                        bf16_buf, bf16_buf,
                        pltpu.VMEM((S, 2 * LANES), BF16),
                        f32_buf, f32_buf, f32_buf,
                        tmp_buf, tmp_buf, tmp_buf,
                        pltpu.VMEM((2, ATTN_BLK, 2 * ATTN_BLK), F32)],
        compiler_params=pltpu.CompilerParams(
            dimension_semantics=("parallel", "parallel"),
            vmem_limit_bytes=VMEM_LIMIT_BYTES),
        name="dil_attn",
    )(trips, proj, proj, proj)


STAGE_ROWS, STAGE_COLS = 512, 1024


def _load_weights_as_bf16(pairs, stage, sems):
    tiles = []
    for w_hbm, w_vmem in pairs:
        n_rows, n_cols = w_hbm.shape
        rows, cols = min(n_rows, STAGE_ROWS), min(n_cols, STAGE_COLS)
        assert n_rows % rows == 0 and n_cols % cols == 0
        tiles += [(w_hbm, w_vmem, r0, c0, rows, cols)
                  for r0 in range(0, n_rows, rows) for c0 in range(0, n_cols, cols)]

    def copy(i):
        w_hbm, _, r0, c0, rows, cols = tiles[i]
        slot = i % 2
        return pltpu.make_async_copy(
            w_hbm.at[pl.ds(r0, rows), pl.ds(c0, cols)],
            stage.at[slot, pl.ds(0, rows), pl.ds(0, cols)], sems.at[slot])

    copy(0).start()
    for i, (_, w_vmem, r0, c0, rows, cols) in enumerate(tiles):
        if i + 1 < len(tiles):
            copy(i + 1).start()
        copy(i).wait()
        w_vmem[pl.ds(r0, rows), pl.ds(c0, cols)] = stage[i % 2, :rows, :cols].astype(BF16)


def _mix_mlp_kernel(x_ref, ya_ref, bg_ref, cg_ref, u_ref, qx_ref, hc_ref, hu_ref,
                    mem_ref, g_mem_ref, w_mem_hbm, conv_w_ref,
                    g_attn_ref, g_conv_ref, g_xattn_ref, w_out_hbm, g_post_mix_ref,
                    g_pre_mlp_ref, w_up_hbm, w_down_hbm, g_post_mlp_ref,
                    o_ref, km_ref, vm_ref, w_mem_ref, w_out_ref, w_up_ref, w_down_ref,
                    stage_ref, sem_ref, *, ff_chunk):
    t = pl.program_id(1)
    xw = km_ref.shape[1]

    @pl.when((pl.program_id(0) == 0) & (t == 0))
    def _():
        _load_weights_as_bf16([(w_mem_hbm, w_mem_ref), (w_out_hbm, w_out_ref),
                               (w_up_hbm, w_up_ref), (w_down_hbm, w_down_ref)],
                              stage_ref, sem_ref)

    @pl.when(t == 0)
    def _():
        hm = _rms(mem_ref[0], g_mem_ref[...]).astype(BF16)
        kv = jnp.dot(hm, w_mem_ref[...], preferred_element_type=F32)
        km_ref[...] = kv[:, :xw].astype(BF16)
        vm_ref[...] = kv[:, xw:].astype(BF16)

    tm = x_ref.shape[1]

    z = cg_ref[0].astype(F32) * u_ref[0].astype(F32)
    hz = hc_ref[0].astype(F32) * hu_ref[0].astype(F32)
    hz = jnp.where(t > 0, hz, jnp.zeros_like(hz))
    z_ext = jnp.concatenate([hz, z], axis=0)
    cw = conv_w_ref[...]
    y_conv = z * cw[CONV_K - 1:CONV_K, :]
    for back in range(1, CONV_K):
        lo = BF16_SUBLANES - back
        y_conv = y_conv + z_ext[lo:lo + tm, :] * cw[CONV_K - 1 - back:CONV_K - back, :]
    y_conv = bg_ref[0].astype(F32) * y_conv

    qx = qx_ref[0]
    km = km_ref[...]
    vm = vm_ref[...]
    lane = lax.broadcasted_iota(jnp.int32, (1, xw), 1)
    xhd = xw // N_MEM_HEADS
    assert xhd == HEAD_DIM
    sels = [(lane >= hd * xhd) & (lane < (hd + 1) * xhd) for hd in range(N_MEM_HEADS)]

    def scores(hd):
        qh = jnp.where(sels[hd], qx, jnp.zeros_like(qx))
        return lax.dot_general(qh, km, (((1,), (1,)), ((), ())),
                               preferred_element_type=F32)

    def softmax(sc):
        p = jnp.exp2(sc - jnp.max(sc, axis=-1, keepdims=True))
        return (p / jnp.sum(p, axis=-1, keepdims=True)).astype(BF16)

    sc, p, y_x = {}, {}, jnp.zeros((tm, xw), F32)
    for hd in range(N_MEM_HEADS + 2):
        if hd < N_MEM_HEADS:
            sc[hd] = scores(hd)
        if 0 <= hd - 1 < N_MEM_HEADS:
            p[hd - 1] = softmax(sc.pop(hd - 1))
        if 0 <= hd - 2 < N_MEM_HEADS:
            o = jnp.dot(p.pop(hd - 2), vm, preferred_element_type=F32)
            y_x = jnp.where(sels[hd - 2], o, y_x)

    halves = ((0, tm // 2), (tm // 2, tm))

    x1_h, h2_h = [], []
    for r0, r1 in halves:
        y = jnp.concatenate([
            _rms(ya_ref[0, r0:r1, :].astype(F32), g_attn_ref[...]).astype(BF16),
            _rms(y_conv[r0:r1], g_conv_ref[...]).astype(BF16),
            _rms(y_x[r0:r1], g_xattn_ref[...]).astype(BF16)], axis=-1)
        y = jnp.dot(y, w_out_ref[...], preferred_element_type=F32)
        x1 = x_ref[0, r0:r1, :] + _rms(y, g_post_mix_ref[...])
        x1_h.append(x1)
        h2_h.append(_rms(x1, g_pre_mlp_ref[...]).astype(BF16))

    def act_fn(up):
        return jnp.square(jnp.maximum(up, 0.0)).astype(BF16)

    chunks = list(range(0, w_up_ref.shape[1], ff_chunk))
    h2 = jnp.concatenate(h2_h, axis=0)
    acc = None
    for f0 in chunks:
        w_up_c = w_up_ref[:, f0:f0 + ff_chunk]
        if f0 == chunks[0]:
            act = jnp.concatenate(
                [act_fn(jnp.dot(h, w_up_c, preferred_element_type=F32)) for h in h2_h],
                axis=0)
        else:
            act = act_fn(jnp.dot(h2, w_up_c, preferred_element_type=F32))
        w_down_c = w_down_ref[f0:f0 + ff_chunk, :]
        if f0 != chunks[-1]:
            part = jnp.dot(act, w_down_c, preferred_element_type=F32)
            acc = part if acc is None else acc + part
        else:
            for (r0, r1), x1 in zip(halves, x1_h):
                f = acc[r0:r1] + jnp.dot(act[r0:r1], w_down_c, preferred_element_type=F32)
                o_ref[0, r0:r1, :] = x1 + _rms(f, g_post_mlp_ref[...])


def _mix_mlp(x, y_attn, proj, mem, g_mem, w_mem, conv_w, g_attn, g_conv, g_xattn,
             w_out, g_post_mix, g_pre_mlp, w_up, w_down, g_post_mlp,
             *, attn_width, conv_width, xattn_width, tm=512, ff_chunk=1024):
    B, S, D = x.shape
    n_mem = mem.shape[1]
    d_ff = w_up.shape[1]
    assert conv_width == xattn_width and (3 * attn_width) % conv_width == 0
    cb0 = 3 * attn_width // conv_width
    halo = BF16_SUBLANES
    const = lambda shape: pl.BlockSpec(shape, lambda b, t: (0,) * len(shape),
                                       pipeline_mode=pl.Buffered(1))
    hbm = pl.BlockSpec(memory_space=pl.ANY)
    pcol = lambda cb: pl.BlockSpec((1, tm, conv_width), lambda b, t: (b, t, cb))
    phalo = lambda cb: pl.BlockSpec(
        (1, halo, conv_width),
        lambda b, t: (b, jnp.maximum(t * (tm // halo) - 1, 0), cb))
    return pl.pallas_call(
        functools.partial(_mix_mlp_kernel, ff_chunk=ff_chunk),
        grid=(B, S // tm),
        in_specs=[
            pl.BlockSpec((1, tm, D), lambda b, t: (b, t, 0)),
            pl.BlockSpec((1, tm, attn_width), lambda b, t: (b, t, 0)),
            pcol(cb0), pcol(cb0 + 1), pcol(cb0 + 2), pcol(cb0 + 3),
            phalo(cb0 + 1), phalo(cb0 + 2),
            pl.BlockSpec((1, n_mem, D), lambda b, t: (b, 0, 0)),
            const((1, D)), hbm, const((CONV_K, conv_width)),
            const((1, attn_width)), const((1, conv_width)), const((1, xattn_width)),
            hbm, const((1, D)),
            const((1, D)), hbm, hbm, const((1, D)),
        ],
        out_specs=pl.BlockSpec((1, tm, D), lambda b, t: (b, t, 0)),
        out_shape=jax.ShapeDtypeStruct((B, S, D), x.dtype),
        scratch_shapes=[pltpu.VMEM((n_mem, xattn_width), BF16),
                        pltpu.VMEM((n_mem, xattn_width), BF16),
                        pltpu.VMEM(w_mem.shape, BF16),
                        pltpu.VMEM(w_out.shape, BF16),
                        pltpu.VMEM(w_up.shape, BF16),
                        pltpu.VMEM(w_down.shape, BF16),
                        pltpu.VMEM((2, STAGE_ROWS, STAGE_COLS), F32),
                        pltpu.SemaphoreType.DMA((2,))],
        compiler_params=pltpu.CompilerParams(
            dimension_semantics=("arbitrary", "arbitrary"),
            vmem_limit_bytes=VMEM_LIMIT_BYTES),
        name="mix_mlp",
    )(x, y_attn, proj, proj, proj, proj, proj, proj, mem, g_mem, w_mem, conv_w,
      g_attn, g_conv, g_xattn, w_out, g_post_mix, g_pre_mlp, w_up, w_down, g_post_mlp)


def kernel(x, mem, positions, g_pre_mix, g_mem, w_in, w_mem_kv, conv_w, g_attn_out,
           g_conv_out, g_xattn_out, w_out, g_post_mix, g_pre_mlp, w_up, w_down,
           g_post_mlp):
    depth = w_in.shape[0]
    attn_width = g_attn_out.shape[1]
    conv_width = g_conv_out.shape[1]
    xattn_width = g_xattn_out.shape[1]
    pos3 = positions[:, None, :]
    row = lambda g: g[None, :]
    for l in range(depth):
        proj = _in_proj(x, pos3, row(g_pre_mix[l]), w_in[l].astype(BF16),
                        attn_width=attn_width, xattn_width=xattn_width)
        y_attn = _dil_attn(proj, attn_width=attn_width)
        x = _mix_mlp(x, y_attn, proj, mem, row(g_mem[l]), w_mem_kv[l], conv_w[l],
                     row(g_attn_out[l]), row(g_conv_out[l]), row(g_xattn_out[l]),
                     w_out[l], row(g_post_mix[l]), row(g_pre_mlp[l]), w_up[l], w_down[l],
                     row(g_post_mlp[l]),
                     attn_width=attn_width, conv_width=conv_width,
                     xattn_width=xattn_width)
    return x
```

```python
import functools

import jax
import jax.numpy as jnp
from jax import lax
from jax.experimental import pallas as pl
from jax.experimental.pallas import tpu as pltpu

F32 = jnp.float32
BF16 = jnp.bfloat16

HEAD_DIM = 64
N_MEM_HEADS = 4
DILATED_PATTERNS = ((128, 1), (512, 4), (2048, 16))
CONV_K = 3
ROPE_THETA = 10000.0
EPS = 1e-6
NEG_INF = -1e30
LOG2_E = 1.4426950408889634

LANES = 128
BF16_SUBLANES = 16
ATTN_BLK = 128
VMEM_LIMIT_BYTES = 56 * 1024 * 1024


def _rms(x, g):
    return x * lax.rsqrt(jnp.mean(x * x, axis=-1, keepdims=True) + EPS) * g


def _in_proj_kernel(x_ref, pos_ref, g_ref, w_ref, *rest, attn_width, xattn_width, chunk):
    n_cast = (len(rest) - 1) // 2
    cast_in, o_ref, cast_out = rest[:n_cast], rest[n_cast], rest[n_cast + 1:]

    h = _rms(x_ref[0], g_ref[...]).astype(BF16)

    half = HEAD_DIM // 2
    freq = lax.broadcasted_iota(jnp.int32, (half, 1), 0).astype(F32)
    inv_freq = jnp.float32(ROPE_THETA) ** (-(freq * 2.0 / HEAD_DIM))
    ang = inv_freq * pos_ref[0].astype(F32)
    cos_t, sin_t = jnp.cos(ang), jnp.sin(ang)
    reps = LANES // HEAD_DIM
    cos_k = jnp.concatenate([cos_t, cos_t] * reps, axis=0).T
    sin_k = jnp.concatenate([-sin_t, sin_t] * reps, axis=0).T
    q_scale = HEAD_DIM ** -0.5 * LOG2_E
    cos_q, sin_q = cos_k * q_scale, sin_k * q_scale
    lane = lax.broadcasted_iota(jnp.int32, (1, LANES), 1)
    first_half = (lane % HEAD_DIM) < half

    n_out = o_ref.shape[-1]
    for c0 in range(0, n_out, chunk):
        p = jnp.dot(h, w_ref[:, c0:c0 + chunk], preferred_element_type=F32)
        if c0 < 2 * attn_width:
            cos, sin = (cos_q, sin_q) if c0 < attn_width else (cos_k, sin_k)
            for g0 in range(0, chunk, LANES):
                t = p[:, g0:g0 + LANES]
                rot = jnp.where(first_half,
                                pltpu.roll(t, LANES - half, 1),
                                pltpu.roll(t, half, 1))
                r = t * cos + rot * sin
                o_ref[0, :, c0 + g0:c0 + g0 + LANES] = r.astype(BF16)
        elif c0 + chunk == n_out:
            lo = chunk - xattn_width
            o_ref[0, :, c0:c0 + lo] = p[:, :lo].astype(BF16)
            o_ref[0, :, c0 + lo:c0 + chunk] = (p[:, lo:] * q_scale).astype(BF16)
        else:
            o_ref[0, :, c0:c0 + chunk] = p.astype(BF16)
        if c0 == 2 * attn_width:
            for src, dst in zip(cast_in, cast_out):
                dst[...] = src[...].astype(BF16)


def _in_proj(x, pos3, g, w_bf16, later_weights, *, attn_width, xattn_width, tm=512, chunk=512):
    B, S, D = x.shape
    n_out = w_bf16.shape[1]
    n_steps = B * (S // tm)
    slices = []
    for w in later_weights:
        rows = w.shape[0] // n_steps
        assert rows * n_steps == w.shape[0] and rows % BF16_SUBLANES == 0
        slices.append(pl.BlockSpec((rows, w.shape[1]), lambda b, t: (b * (S // tm) + t, 0)))
    out = pl.pallas_call(
        functools.partial(_in_proj_kernel, attn_width=attn_width,
                          xattn_width=xattn_width, chunk=chunk),
        grid=(B, S // tm),
        in_specs=[
            pl.BlockSpec((1, tm, D), lambda b, t: (b, t, 0)),
            pl.BlockSpec((1, 1, tm), lambda b, t: (b, 0, t)),
            pl.BlockSpec((1, D), lambda b, t: (0, 0)),
            pl.BlockSpec((D, n_out), lambda b, t: (0, 0)),
        ] + slices,
        out_specs=[pl.BlockSpec((1, tm, n_out), lambda b, t: (b, t, 0))] + slices,
        out_shape=[jax.ShapeDtypeStruct((B, S, n_out), BF16)]
        + [jax.ShapeDtypeStruct(w.shape, BF16) for w in later_weights],
        compiler_params=pltpu.CompilerParams(
            dimension_semantics=("parallel", "parallel"),
            vmem_limit_bytes=VMEM_LIMIT_BYTES),
        name="in_proj",
    )(x, pos3, g, w_bf16, *later_weights)
    return out[0], out[1:]


def _band_block(qb, kb, vb1, bias, lane_lo):
    ms, ls, accs = [], [], []
    for head_lo in (True, False):
        sel = lane_lo if head_lo else jnp.logical_not(lane_lo)
        qh = jnp.where(sel, qb, jnp.zeros_like(qb))
        s = lax.dot_general(qh, kb, (((1,), (1,)), ((), ())),
                            preferred_element_type=F32) + bias
        m = jnp.max(s, axis=-1, keepdims=True)
        p = jnp.exp2(s - m).astype(BF16)
        r = jnp.dot(p, vb1, preferred_element_type=F32)
        accs.append(r[:, :LANES])
        ls.append(r[:, LANES:])
        ms.append(jnp.broadcast_to(m, (m.shape[0], LANES)))
    return (jnp.where(lane_lo, ms[0], ms[1]), jnp.where(lane_lo, ls[0], ls[1]),
            jnp.where(lane_lo, accs[0], accs[1]))


COPY_ROWS = 64
DEINTERLEAVE = 4


def _for_each_split(seq, region, body):
    run = region // DEINTERLEAVE
    span = DEINTERLEAVE * COPY_ROWS
    steps_per_region = region // span

    def step(t, carry):
        strided0 = pl.multiple_of(t * span, span)
        dense0 = pl.multiple_of((t // steps_per_region) * region
                                + (t % steps_per_region) * COPY_ROWS, COPY_ROWS)
        for j in range(DEINTERLEAVE):
            body(pl.ds(strided0 + j, COPY_ROWS, stride=DEINTERLEAVE),
                 pl.ds(dense0 + j * run, COPY_ROWS))
        return carry

    lax.fori_loop(0, seq // span, step, 0)


def _merge_softmax(a, b):
    (m_a, l_a, a_a), (m_b, l_b, a_b) = a, b
    m = jnp.maximum(m_a, m_b)
    w_a = jnp.exp2(m_a - m)
    w_b = jnp.exp2(m_b - m)
    return m, l_a * w_a + l_b * w_b, a_a * w_a + a_b * w_b


def _dil_attn_kernel(trips_ref, q_ref, k_ref, v_ref, o_ref,
                     qf, kf, vf, q4f, k4f, v4f, qp, kp, vp1,
                     m_p4, l_p4, a_p4, m_tmp, l_tmp, a_tmp, bias_ref,
                     *, seq, patterns, unroll):
    blk = ATTN_BLK
    n_blocks = seq // blk
    lane_lo = lax.broadcasted_iota(jnp.int32, (1, LANES), 1) < HEAD_DIM
    assert [d for _, d in patterns] == [1, DEINTERLEAVE, DEINTERLEAVE ** 2]
    assert all(w // d == blk for w, d in patterns)
    run4 = seq // DEINTERLEAVE
    run16 = run4 // DEINTERLEAVE
    blocks_per_run4 = run4 // blk
    blocks_per_run16 = run16 // blk
    assert n_blocks % unroll == 0 and unroll % blocks_per_run4 == 0
    n_trips = trips_ref[0]

    qi = lax.broadcasted_iota(jnp.int32, (blk, 2 * blk), 0)
    kj = lax.broadcasted_iota(jnp.int32, (blk, 2 * blk), 1)
    bias_ref[0] = jnp.where(kj <= qi, 0.0, NEG_INF).astype(F32)
    bias_ref[1] = jnp.where((kj >= qi) & (kj <= qi + blk), 0.0, NEG_INF).astype(F32)
    vp1[:, LANES:] = jnp.ones((seq, LANES), BF16)

    pat4 = (m_p4, l_p4, a_p4)
    tmp = (m_tmp, l_tmp, a_tmp)

    def attend(sub_len, q_src, k_src, finish):
        nb = sub_len // blk
        assert nb >= 2

        def trip_body(trip, carry):
            for u in range(unroll):
                g = trip * unroll + u
                n = g % nb
                q0 = pl.multiple_of(g * blk, blk)
                k0 = pl.multiple_of(jnp.where(n > 0, q0 - blk, q0), blk)
                triple = _band_block(q_src[pl.ds(q0, blk), :],
                                     k_src[pl.ds(k0, 2 * blk), :],
                                     vp1[pl.ds(k0, 2 * blk), :],
                                     bias_ref[jnp.minimum(n, 1)], lane_lo)
                finish(trip, u, q0, triple)
            return carry

        lax.fori_loop(0, n_trips, trip_body, 0)

    qf[...] = q_ref[0].astype(F32)
    kf[...] = k_ref[0].astype(F32)
    vf[...] = v_ref[0].astype(F32)

    def gather4(strided, dense):
        for src, dst_f, dst_b in ((qf, q4f, qp), (kf, k4f, kp)):
            rows = src[strided, :]
            dst_f[dense, :] = rows
            dst_b[dense, :] = rows.astype(BF16)
        rows = vf[strided, :]
        v4f[dense, :] = rows
        vp1[dense, :LANES] = rows.astype(BF16)

    def store_pat4(trip, u, q0, triple):
        for ref, val in zip(pat4, triple):
            ref[pl.ds(q0, blk), :] = val

    _for_each_split(seq, seq, gather4)
    attend(run4, qp, kp, store_pat4)

    def gather16(strided, dense):
        qp[dense, :] = q4f[strided, :].astype(BF16)
        kp[dense, :] = k4f[strided, :].astype(BF16)
        vp1[dense, :LANES] = v4f[strided, :].astype(BF16)

    def fold_into_pat4(trip, u, q0, triple):
        r_static, idx = divmod(u, blocks_per_run4)
        j, a0 = idx // blocks_per_run16, (idx % blocks_per_run16) * blk
        r_dyn = pl.multiple_of(trip * (unroll // blocks_per_run4) * run4, run4)
        rows = pl.ds(r_dyn + r_static * run4 + j + DEINTERLEAVE * a0, blk,
                     stride=DEINTERLEAVE)
        merged = _merge_softmax(tuple(ref[rows, :] for ref in pat4), triple)
        for ref, val in zip(pat4, merged):
            ref[rows, :] = val

    _for_each_split(seq, run4, gather16)
    attend(run16, qp, kp, fold_into_pat4)

    def finish_natural(trip, u, q0, triple):
        sub = blk // DEINTERLEAVE
        i_dyn = pl.multiple_of(trip * (unroll * sub), unroll * sub)
        for j in range(DEINTERLEAVE):
            src = pl.ds(i_dyn + j * run4 + u * sub, sub)
            for t_ref, p_ref in zip(tmp, pat4):
                t_ref[u, pl.ds(j, sub, stride=DEINTERLEAVE), :] = p_ref[src, :]
        _, l, acc = _merge_softmax(tuple(t_ref[u] for t_ref in tmp), triple)
        o_ref[0, pl.ds(q0, blk), :] = (acc / l).astype(o_ref.dtype)

    vp1[:, :LANES] = v_ref[0]
    attend(seq, q_ref.at[0], k_ref.at[0], finish_natural)


def _dil_attn(proj, *, attn_width, unroll=32):
    B, S, _ = proj.shape
    n_pairs = attn_width // LANES
    trips = jnp.full((1,), S // ATTN_BLK // unroll, jnp.int32)
    kern = functools.partial(_dil_attn_kernel, seq=S, patterns=DILATED_PATTERNS,
                             unroll=unroll)
    col = lambda off: (lambda b, hp: (b, 0, off + hp))
    f32_buf = pltpu.VMEM((S, LANES), F32)
    bf16_buf = pltpu.VMEM((S, LANES), BF16)
    tmp_buf = pltpu.VMEM((unroll, ATTN_BLK, LANES), F32)
    return pl.pallas_call(
        kern,
        grid=(B, n_pairs),
        in_specs=[pl.BlockSpec(memory_space=pltpu.SMEM),
                  pl.BlockSpec((1, S, LANES), col(0)),
                  pl.BlockSpec((1, S, LANES), col(n_pairs)),
                  pl.BlockSpec((1, S, LANES), col(2 * n_pairs))],
        out_specs=pl.BlockSpec((1, S, LANES), lambda b, hp: (b, 0, hp)),
        out_shape=jax.ShapeDtypeStruct((B, S, attn_width), BF16),
        scratch_shapes=[f32_buf, f32_buf, f32_buf,
                        f32_buf, f32_buf, f32_buf,
                        bf16_buf, bf16_buf,
                        pltpu.VMEM((S, 2 * LANES), BF16),
                        f32_buf, f32_buf, f32_buf,
                        tmp_buf, tmp_buf, tmp_buf,
                        pltpu.VMEM((2, ATTN_BLK, 2 * ATTN_BLK), F32)],
        compiler_params=pltpu.CompilerParams(
            dimension_semantics=("parallel", "parallel"),
            vmem_limit_bytes=VMEM_LIMIT_BYTES),
        name="dil_attn",
    )(trips, proj, proj, proj)


def _mix_mlp_kernel(x_ref, ya_ref, bg_ref, cg_ref, u_ref, qx_ref, hc_ref, hu_ref,
                    mem_ref, g_mem_ref, w_mem_ref, conv_w_ref,
                    g_attn_ref, g_conv_ref, g_xattn_ref, w_out_ref, g_post_mix_ref,
                    g_pre_mlp_ref, w_up_ref, w_down_ref, g_post_mlp_ref,
                    o_ref, km_ref, vm_ref, *, ff_chunk):
    t = pl.program_id(1)
    xw = km_ref.shape[1]

    @pl.when(t == 0)
    def _():
        hm = _rms(mem_ref[0], g_mem_ref[...]).astype(BF16)
        kv = jnp.dot(hm, w_mem_ref[...], preferred_element_type=F32)
        km_ref[...] = kv[:, :xw].astype(BF16)
        vm_ref[...] = kv[:, xw:].astype(BF16)

    tm = x_ref.shape[1]

    z = cg_ref[0].astype(F32) * u_ref[0].astype(F32)
    hz = hc_ref[0].astype(F32) * hu_ref[0].astype(F32)
    hz = jnp.where(t > 0, hz, jnp.zeros_like(hz))
    z_ext = jnp.concatenate([hz, z], axis=0)
    cw = conv_w_ref[...]
    y_conv = z * cw[CONV_K - 1:CONV_K, :]
    for back in range(1, CONV_K):
        lo = BF16_SUBLANES - back
        y_conv = y_conv + z_ext[lo:lo + tm, :] * cw[CONV_K - 1 - back:CONV_K - back, :]
    y_conv = bg_ref[0].astype(F32) * y_conv

    qx = qx_ref[0]
    km = km_ref[...]
    vm = vm_ref[...]
    lane = lax.broadcasted_iota(jnp.int32, (1, xw), 1)
    xhd = xw // N_MEM_HEADS
    assert xhd == HEAD_DIM
    sels = [(lane >= hd * xhd) & (lane < (hd + 1) * xhd) for hd in range(N_MEM_HEADS)]

    def scores(hd):
        qh = jnp.where(sels[hd], qx, jnp.zeros_like(qx))
        return lax.dot_general(qh, km, (((1,), (1,)), ((), ())),
                               preferred_element_type=F32)

    def softmax(sc):
        p = jnp.exp2(sc - jnp.max(sc, axis=-1, keepdims=True))
        return (p / jnp.sum(p, axis=-1, keepdims=True)).astype(BF16)

    sc, p, y_x = {}, {}, jnp.zeros((tm, xw), F32)
    for hd in range(N_MEM_HEADS + 2):
        if hd < N_MEM_HEADS:
            sc[hd] = scores(hd)
        if 0 <= hd - 1 < N_MEM_HEADS:
            p[hd - 1] = softmax(sc.pop(hd - 1))
        if 0 <= hd - 2 < N_MEM_HEADS:
            o = jnp.dot(p.pop(hd - 2), vm, preferred_element_type=F32)
            y_x = jnp.where(sels[hd - 2], o, y_x)

    halves = ((0, tm // 2), (tm // 2, tm))

    x1_h, h2_h = [], []
    for r0, r1 in halves:
        y = jnp.concatenate([
            _rms(ya_ref[0, r0:r1, :].astype(F32), g_attn_ref[...]).astype(BF16),
            _rms(y_conv[r0:r1], g_conv_ref[...]).astype(BF16),
            _rms(y_x[r0:r1], g_xattn_ref[...]).astype(BF16)], axis=-1)
        y = jnp.dot(y, w_out_ref[...], preferred_element_type=F32)
        x1 = x_ref[0, r0:r1, :] + _rms(y, g_post_mix_ref[...])
        x1_h.append(x1)
        h2_h.append(_rms(x1, g_pre_mlp_ref[...]).astype(BF16))

    def act_fn(up):
        return jnp.square(jnp.maximum(up, 0.0)).astype(BF16)

    chunks = list(range(0, w_up_ref.shape[1], ff_chunk))
    h2 = jnp.concatenate(h2_h, axis=0)
    acc = None
    for f0 in chunks:
        w_up_c = w_up_ref[:, f0:f0 + ff_chunk]
        if f0 == chunks[0]:
            act = jnp.concatenate(
                [act_fn(jnp.dot(h, w_up_c, preferred_element_type=F32)) for h in h2_h],
                axis=0)
        else:
            act = act_fn(jnp.dot(h2, w_up_c, preferred_element_type=F32))
        w_down_c = w_down_ref[f0:f0 + ff_chunk, :]
        if f0 != chunks[-1]:
            part = jnp.dot(act, w_down_c, preferred_element_type=F32)
            acc = part if acc is None else acc + part
        else:
            for (r0, r1), x1 in zip(halves, x1_h):
                f = acc[r0:r1] + jnp.dot(act[r0:r1], w_down_c, preferred_element_type=F32)
                o_ref[0, r0:r1, :] = x1 + _rms(f, g_post_mlp_ref[...])


def _mix_mlp(x, y_attn, proj, mem, g_mem, w_mem, conv_w, g_attn, g_conv, g_xattn,
             w_out, g_post_mix, g_pre_mlp, w_up, w_down, g_post_mlp,
             *, attn_width, conv_width, xattn_width, tm=512, ff_chunk=1024):
    B, S, D = x.shape
    n_mem = mem.shape[1]
    d_ff = w_up.shape[1]
    assert conv_width == xattn_width and (3 * attn_width) % conv_width == 0
    cb0 = 3 * attn_width // conv_width
    halo = BF16_SUBLANES
    const = lambda shape: pl.BlockSpec(shape, lambda b, t: (0,) * len(shape),
                                       pipeline_mode=pl.Buffered(1))
    pcol = lambda cb: pl.BlockSpec((1, tm, conv_width), lambda b, t: (b, t, cb))
    phalo = lambda cb: pl.BlockSpec(
        (1, halo, conv_width),
        lambda b, t: (b, jnp.maximum(t * (tm // halo) - 1, 0), cb))
    return pl.pallas_call(
        functools.partial(_mix_mlp_kernel, ff_chunk=ff_chunk),
        grid=(B, S // tm),
        in_specs=[
            pl.BlockSpec((1, tm, D), lambda b, t: (b, t, 0)),
            pl.BlockSpec((1, tm, attn_width), lambda b, t: (b, t, 0)),
            pcol(cb0), pcol(cb0 + 1), pcol(cb0 + 2), pcol(cb0 + 3),
            phalo(cb0 + 1), phalo(cb0 + 2),
            pl.BlockSpec((1, n_mem, D), lambda b, t: (b, 0, 0)),
            const((1, D)), const((D, 2 * xattn_width)), const((CONV_K, conv_width)),
            const((1, attn_width)), const((1, conv_width)), const((1, xattn_width)),
            const((D, D)), const((1, D)),
            const((1, D)), const((D, d_ff)), const((d_ff, D)), const((1, D)),
        ],
        out_specs=pl.BlockSpec((1, tm, D), lambda b, t: (b, t, 0)),
        out_shape=jax.ShapeDtypeStruct((B, S, D), x.dtype),
        scratch_shapes=[pltpu.VMEM((n_mem, xattn_width), BF16),
                        pltpu.VMEM((n_mem, xattn_width), BF16)],
        compiler_params=pltpu.CompilerParams(
            dimension_semantics=("parallel", "arbitrary"),
            vmem_limit_bytes=VMEM_LIMIT_BYTES),
        name="mix_mlp",
    )(x, y_attn, proj, proj, proj, proj, proj, proj, mem, g_mem, w_mem, conv_w,
      g_attn, g_conv, g_xattn, w_out, g_post_mix, g_pre_mlp, w_up, w_down, g_post_mlp)


def kernel(x, mem, positions, g_pre_mix, g_mem, w_in, w_mem_kv, conv_w, g_attn_out,
           g_conv_out, g_xattn_out, w_out, g_post_mix, g_pre_mlp, w_up, w_down,
           g_post_mlp):
    depth = w_in.shape[0]
    attn_width = g_attn_out.shape[1]
    conv_width = g_conv_out.shape[1]
    xattn_width = g_xattn_out.shape[1]
    pos3 = positions[:, None, :]
    row = lambda g: g[None, :]
    for l in range(depth):
        proj, (w_mem_b, w_out_b, w_up_b, w_down_b) = _in_proj(
            x, pos3, row(g_pre_mix[l]), w_in[l].astype(BF16),
            (w_mem_kv[l], w_out[l], w_up[l], w_down[l]),
            attn_width=attn_width, xattn_width=xattn_width)
        y_attn = _dil_attn(proj, attn_width=attn_width)
        x = _mix_mlp(x, y_attn, proj, mem, row(g_mem[l]), w_mem_b, conv_w[l],
                     row(g_attn_out[l]), row(g_conv_out[l]), row(g_xattn_out[l]),
                     w_out_b, row(g_post_mix[l]), row(g_pre_mlp[l]), w_up_b, w_down_b,
                     row(g_post_mlp[l]),
                     attn_width=attn_width, conv_width=conv_width,
                     xattn_width=xattn_width)
    return x
```

```python
import functools

import jax
import jax.numpy as jnp
from jax import lax
from jax.experimental import pallas as pl
from jax.experimental.pallas import tpu as pltpu

F32 = jnp.float32
BF16 = jnp.bfloat16

HEAD_DIM = 64
N_MEM_HEADS = 4
DILATED_PATTERNS = ((128, 1), (512, 4), (2048, 16))
CONV_K = 3
ROPE_THETA = 10000.0
EPS = 1e-6
NEG_INF = -1e30
LOG2_E = 1.4426950408889634

LANES = 128
BF16_SUBLANES = 16
ATTN_BLK = 128
VMEM_LIMIT_BYTES = 56 * 1024 * 1024


def _rms(x, g):
    return x * lax.rsqrt(jnp.mean(x * x, axis=-1, keepdims=True) + EPS) * g


DEINTERLEAVE = 4


def _in_proj_kernel(x_ref, pos_ref, g_ref, w_ref, *rest, attn_width, xattn_width, chunk):
    n_cast = (len(rest) - 4) // 2
    cast_in, (o_ref, o4_ref, o16_ref) = rest[:n_cast], rest[n_cast:n_cast + 3]
    cast_out, stage_ref = rest[n_cast + 3:-1], rest[-1]
    step = pl.program_id(0)
    tm = x_ref.shape[1]

    @pl.when(step == 0)
    def _():
        stage_ref[1] = jnp.zeros(stage_ref.shape[1:], F32)

    h = _rms(x_ref[0], g_ref[...]).astype(BF16)

    half = HEAD_DIM // 2
    freq = lax.broadcasted_iota(jnp.int32, (half, 1), 0).astype(F32)
    inv_freq = jnp.float32(ROPE_THETA) ** (-(freq * 2.0 / HEAD_DIM))
    ang = inv_freq * pos_ref[0].astype(F32)
    cos_t, sin_t = jnp.cos(ang), jnp.sin(ang)
    reps = LANES // HEAD_DIM
    cos_k = jnp.concatenate([cos_t, cos_t] * reps, axis=0).T
    sin_k = jnp.concatenate([-sin_t, sin_t] * reps, axis=0).T
    q_scale = HEAD_DIM ** -0.5 * LOG2_E
    cos_q, sin_q = cos_k * q_scale, sin_k * q_scale
    lane = lax.broadcasted_iota(jnp.int32, (1, LANES), 1)
    first_half = (lane % HEAD_DIM) < half

    n_out = o_ref.shape[-1]
    for c0 in range(0, n_out, chunk):
        p = jnp.dot(h, w_ref[:, c0:c0 + chunk], preferred_element_type=F32)
        if c0 < 2 * attn_width:
            cos, sin = (cos_q, sin_q) if c0 < attn_width else (cos_k, sin_k)
            for g0 in range(0, chunk, LANES):
                t = p[:, g0:g0 + LANES]
                rot = jnp.where(first_half,
                                pltpu.roll(t, LANES - half, 1),
                                pltpu.roll(t, half, 1))
                r = t * cos + rot * sin
                o_ref[0, :, c0 + g0:c0 + g0 + LANES] = r.astype(BF16)
                stage_ref[step % 2, (c0 + g0) // LANES] = r
        elif c0 + chunk == n_out:
            lo = chunk - xattn_width
            o_ref[0, :, c0:c0 + lo] = p[:, :lo].astype(BF16)
            o_ref[0, :, c0 + lo:c0 + chunk] = (p[:, lo:] * q_scale).astype(BF16)
        else:
            o_ref[0, :, c0:c0 + chunk] = p.astype(BF16)
            if c0 < 3 * attn_width:
                for g0 in range(0, chunk, LANES):
                    stage_ref[step % 2, (c0 + g0) // LANES] = p[:, g0:g0 + LANES]
        if c0 == 2 * attn_width:
            for src, dst in zip(cast_in, cast_out):
                dst[...] = src[...].astype(BF16)
            for grp in range(stage_ref.shape[1]):
                prev = stage_ref.at[(step + 1) % 2, grp]
                cols = slice(grp * LANES, (grp + 1) * LANES)
                for j in range(DEINTERLEAVE):
                    rows = prev[pl.ds(j, tm // DEINTERLEAVE, stride=DEINTERLEAVE), :]
                    o4_ref[0, j, :, cols] = rows.astype(BF16)
                    for jj in range(DEINTERLEAVE):
                        rows = prev[pl.ds(j + DEINTERLEAVE * jj, tm // DEINTERLEAVE ** 2,
                                          stride=DEINTERLEAVE ** 2), :]
                        o16_ref[0, DEINTERLEAVE * j + jj, :, cols] = rows.astype(BF16)


def _in_proj(x, positions, g, w_bf16, later_weights, *, attn_width, xattn_width,
             tm=512, chunk=512):
    B, S, D = x.shape
    n_out = w_bf16.shape[1]
    qkv = 3 * attn_width
    tiles_per_seq = S // tm
    n_tiles = B * tiles_per_seq
    d4, d16 = DEINTERLEAVE, DEINTERLEAVE ** 2
    assert tm % (d16 * BF16_SUBLANES) == 0
    cur = lambda i: jnp.minimum(i, n_tiles - 1)
    prev = lambda i: jnp.maximum(i - 1, 0)
    slices = []
    for w in later_weights:
        rows = w.shape[0] // n_tiles
        assert rows * n_tiles == w.shape[0] and rows % BF16_SUBLANES == 0
        slices.append(pl.BlockSpec((rows, w.shape[1]), lambda i: (cur(i), 0)))
    out = pl.pallas_call(
        functools.partial(_in_proj_kernel, attn_width=attn_width,
                          xattn_width=xattn_width, chunk=chunk),
        grid=(n_tiles + 1,),
        in_specs=[
            pl.BlockSpec((1, tm, D), lambda i: (cur(i), 0, 0)),
            pl.BlockSpec((1, 1, tm), lambda i: (cur(i), 0, 0)),
            pl.BlockSpec((1, D), lambda i: (0, 0)),
            pl.BlockSpec((D, n_out), lambda i: (0, 0)),
        ] + slices,
        out_specs=[
            pl.BlockSpec((1, tm, n_out), lambda i: (cur(i), 0, 0)),
            pl.BlockSpec((1, d4, tm // d4, qkv),
                         lambda i: (prev(i) // tiles_per_seq, 0, prev(i) % tiles_per_seq, 0)),
            pl.BlockSpec((1, d16, tm // d16, qkv),
                         lambda i: (prev(i) // tiles_per_seq, 0, prev(i) % tiles_per_seq, 0)),
        ] + slices,
        out_shape=[jax.ShapeDtypeStruct((n_tiles, tm, n_out), BF16),
                   jax.ShapeDtypeStruct((B, d4, S // d4, qkv), BF16),
                   jax.ShapeDtypeStruct((B, d16, S // d16, qkv), BF16)]
        + [jax.ShapeDtypeStruct(w.shape, BF16) for w in later_weights],
        scratch_shapes=[pltpu.VMEM((2, qkv // LANES, tm, LANES), F32)],
        compiler_params=pltpu.CompilerParams(
            dimension_semantics=("arbitrary",),
            vmem_limit_bytes=VMEM_LIMIT_BYTES),
        name="in_proj",
    )(x.reshape(n_tiles, tm, D), positions.reshape(n_tiles, 1, tm), g, w_bf16, *later_weights)
    return out[0].reshape(B, S, n_out), out[1], out[2], out[3:]


def _band_block(qb, kb, vb1, bias, lane_lo):
    ms, ls, accs = [], [], []
    for head_lo in (True, False):
        sel = lane_lo if head_lo else jnp.logical_not(lane_lo)
        qh = jnp.where(sel, qb, jnp.zeros_like(qb))
        s = lax.dot_general(qh, kb, (((1,), (1,)), ((), ())),
                            preferred_element_type=F32) + bias
        m = jnp.max(s, axis=-1, keepdims=True)
        p = jnp.exp2(s - m).astype(BF16)
        r = jnp.dot(p, vb1, preferred_element_type=F32)
        accs.append(r[:, :LANES])
        ls.append(r[:, LANES:])
        ms.append(jnp.broadcast_to(m, (m.shape[0], LANES)))
    return (jnp.where(lane_lo, ms[0], ms[1]), jnp.where(lane_lo, ls[0], ls[1]),
            jnp.where(lane_lo, accs[0], accs[1]))


def _merge_softmax(a, b):
    (m_a, l_a, a_a), (m_b, l_b, a_b) = a, b
    m = jnp.maximum(m_a, m_b)
    w_a = jnp.exp2(m_a - m)
    w_b = jnp.exp2(m_b - m)
    return m, l_a * w_a + l_b * w_b, a_a * w_a + a_b * w_b


def _dil_attn_kernel(trips_ref, q_ref, k_ref, v_ref, q4_ref, k4_ref, v4_ref,
                     q16_ref, k16_ref, v16_ref, o_ref, vp1,
                     m_p4, l_p4, a_p4, m_tmp, l_tmp, a_tmp, bias_ref,
                     *, seq, patterns, unroll):
    blk = ATTN_BLK
    n_blocks = seq // blk
    lane_lo = lax.broadcasted_iota(jnp.int32, (1, LANES), 1) < HEAD_DIM
    assert [d for _, d in patterns] == [1, DEINTERLEAVE, DEINTERLEAVE ** 2]
    assert all(w // d == blk for w, d in patterns)
    run4 = seq // DEINTERLEAVE
    run16 = run4 // DEINTERLEAVE
    blocks_per_run4 = run4 // blk
    blocks_per_run16 = run16 // blk
    assert n_blocks % unroll == 0 and unroll % blocks_per_run4 == 0
    n_trips = trips_ref[0]

    qi = lax.broadcasted_iota(jnp.int32, (blk, 2 * blk), 0)
    kj = lax.broadcasted_iota(jnp.int32, (blk, 2 * blk), 1)
    bias_ref[0] = jnp.where(kj <= qi, 0.0, NEG_INF).astype(F32)
    bias_ref[1] = jnp.where((kj >= qi) & (kj <= qi + blk), 0.0, NEG_INF).astype(F32)
    vp1[:, LANES:] = jnp.ones((seq, LANES), BF16)

    pat4 = (m_p4, l_p4, a_p4)
    tmp = (m_tmp, l_tmp, a_tmp)

    def attend(q_src, k_src, v_src, finish):
        n_runs, sub_len, _ = q_src.shape
        nb = sub_len // blk
        assert nb >= 2 and n_runs * sub_len == seq
        for r in range(n_runs):
            vp1[r * sub_len:(r + 1) * sub_len, :LANES] = v_src[r]

        def trip_body(trip, carry):
            for u in range(unroll):
                g = trip * unroll + u
                run, n = g // nb, g % nb
                q0 = pl.multiple_of(g * blk, blk)
                k0 = pl.multiple_of(jnp.where(n > 0, q0 - blk, q0), blk)
                k_in_run = pl.multiple_of(jnp.maximum(n - 1, 0) * blk, blk)
                triple = _band_block(q_src[run, pl.ds(pl.multiple_of(n * blk, blk), blk), :],
                                     k_src[run, pl.ds(k_in_run, 2 * blk), :],
                                     vp1[pl.ds(k0, 2 * blk), :],
                                     bias_ref[jnp.minimum(n, 1)], lane_lo)
                finish(trip, u, q0, triple)
            return carry

        lax.fori_loop(0, n_trips, trip_body, 0)

    def store_pat4(trip, u, q0, triple):
        for ref, val in zip(pat4, triple):
            ref[pl.ds(q0, blk), :] = val

    attend(q4_ref.at[0], k4_ref.at[0], v4_ref.at[0], store_pat4)

    def fold_into_pat4(trip, u, q0, triple):
        r_static, idx = divmod(u, blocks_per_run4)
        j, a0 = idx // blocks_per_run16, (idx % blocks_per_run16) * blk
        r_dyn = pl.multiple_of(trip * (unroll // blocks_per_run4) * run4, run4)
        rows = pl.ds(r_dyn + r_static * run4 + j + DEINTERLEAVE * a0, blk,
                     stride=DEINTERLEAVE)
        merged = _merge_softmax(tuple(ref[rows, :] for ref in pat4), triple)
        for ref, val in zip(pat4, merged):
            ref[rows, :] = val

    attend(q16_ref.at[0], k16_ref.at[0], v16_ref.at[0], fold_into_pat4)

    def finish_natural(trip, u, q0, triple):
        sub = blk // DEINTERLEAVE
        i_dyn = pl.multiple_of(trip * (unroll * sub), unroll * sub)
        for j in range(DEINTERLEAVE):
            src = pl.ds(i_dyn + j * run4 + u * sub, sub)
            for t_ref, p_ref in zip(tmp, pat4):
                t_ref[u, pl.ds(j, sub, stride=DEINTERLEAVE), :] = p_ref[src, :]
        _, l, acc = _merge_softmax(tuple(t_ref[u] for t_ref in tmp), triple)
        o_ref[0, pl.ds(q0, blk), :] = (acc / l).astype(o_ref.dtype)

    attend(q_ref, k_ref, v_ref, finish_natural)


def _dil_attn(proj, qkv4, qkv16, *, attn_width, unroll=32):
    B, S, _ = proj.shape
    n_pairs = attn_width // LANES
    trips = jnp.full((1,), S // ATTN_BLK // unroll, jnp.int32)
    kern = functools.partial(_dil_attn_kernel, seq=S, patterns=DILATED_PATTERNS,
                             unroll=unroll)
    col = lambda off: (lambda b, hp: (b, 0, off + hp))
    col4 = lambda off: (lambda b, hp: (b, 0, 0, off + hp))
    f32_buf = pltpu.VMEM((S, LANES), F32)
    tmp_buf = pltpu.VMEM((unroll, ATTN_BLK, LANES), F32)
    return pl.pallas_call(
        kern,
        grid=(B, n_pairs),
        in_specs=[pl.BlockSpec(memory_space=pltpu.SMEM),
                  pl.BlockSpec((1, S, LANES), col(0)),
                  pl.BlockSpec((1, S, LANES), col(n_pairs)),
                  pl.BlockSpec((1, S, LANES), col(2 * n_pairs))]
        + [pl.BlockSpec((1,) + a.shape[1:3] + (LANES,), col4(off))
           for a in (qkv4, qkv16) for off in (0, n_pairs, 2 * n_pairs)],
        out_specs=pl.BlockSpec((1, S, LANES), lambda b, hp: (b, 0, hp)),
        out_shape=jax.ShapeDtypeStruct((B, S, attn_width), BF16),
        scratch_shapes=[pltpu.VMEM((S, 2 * LANES), BF16),
                        f32_buf, f32_buf, f32_buf,
                        tmp_buf, tmp_buf, tmp_buf,
                        pltpu.VMEM((2, ATTN_BLK, 2 * ATTN_BLK), F32)],
        compiler_params=pltpu.CompilerParams(
            dimension_semantics=("parallel", "parallel"),
            vmem_limit_bytes=VMEM_LIMIT_BYTES),
        name="dil_attn",
    )(trips, proj, proj, proj, qkv4, qkv4, qkv4, qkv16, qkv16, qkv16)


def _mix_mlp_kernel(x_ref, ya_ref, bg_ref, cg_ref, u_ref, qx_ref, hc_ref, hu_ref,
                    mem_ref, g_mem_ref, w_mem_ref, conv_w_ref,
                    g_attn_ref, g_conv_ref, g_xattn_ref, w_out_ref, g_post_mix_ref,
                    g_pre_mlp_ref, w_up_ref, w_down_ref, g_post_mlp_ref,
                    o_ref, km_ref, vm_ref, *, ff_chunk):
    t = pl.program_id(1)
    xw = km_ref.shape[1]

    @pl.when(t == 0)
    def _():
        hm = _rms(mem_ref[0], g_mem_ref[...]).astype(BF16)
        kv = jnp.dot(hm, w_mem_ref[...], preferred_element_type=F32)
        km_ref[...] = kv[:, :xw].astype(BF16)
        vm_ref[...] = kv[:, xw:].astype(BF16)

    tm = x_ref.shape[1]

    z = cg_ref[0].astype(F32) * u_ref[0].astype(F32)
    hz = hc_ref[0].astype(F32) * hu_ref[0].astype(F32)
    hz = jnp.where(t > 0, hz, jnp.zeros_like(hz))
    z_ext = jnp.concatenate([hz, z], axis=0)
    cw = conv_w_ref[...]
    y_conv = z * cw[CONV_K - 1:CONV_K, :]
    for back in range(1, CONV_K):
        lo = BF16_SUBLANES - back
        y_conv = y_conv + z_ext[lo:lo + tm, :] * cw[CONV_K - 1 - back:CONV_K - back, :]
    y_conv = bg_ref[0].astype(F32) * y_conv

    qx = qx_ref[0]
    km = km_ref[...]
    vm = vm_ref[...]
    lane = lax.broadcasted_iota(jnp.int32, (1, xw), 1)
    xhd = xw // N_MEM_HEADS
    assert xhd == HEAD_DIM
    sels = [(lane >= hd * xhd) & (lane < (hd + 1) * xhd) for hd in range(N_MEM_HEADS)]

    def scores(hd):
        qh = jnp.where(sels[hd], qx, jnp.zeros_like(qx))
        return lax.dot_general(qh, km, (((1,), (1,)), ((), ())),
                               preferred_element_type=F32)

    def softmax(sc):
        p = jnp.exp2(sc - jnp.max(sc, axis=-1, keepdims=True))
        return (p / jnp.sum(p, axis=-1, keepdims=True)).astype(BF16)

    sc, p, y_x = {}, {}, jnp.zeros((tm, xw), F32)
    for hd in range(N_MEM_HEADS + 2):
        if hd < N_MEM_HEADS:
            sc[hd] = scores(hd)
        if 0 <= hd - 1 < N_MEM_HEADS:
            p[hd - 1] = softmax(sc.pop(hd - 1))
        if 0 <= hd - 2 < N_MEM_HEADS:
            o = jnp.dot(p.pop(hd - 2), vm, preferred_element_type=F32)
            y_x = jnp.where(sels[hd - 2], o, y_x)

    halves = ((0, tm // 2), (tm // 2, tm))

    x1_h, h2_h = [], []
    for r0, r1 in halves:
        y = jnp.concatenate([
            _rms(ya_ref[0, r0:r1, :].astype(F32), g_attn_ref[...]).astype(BF16),
            _rms(y_conv[r0:r1], g_conv_ref[...]).astype(BF16),
            _rms(y_x[r0:r1], g_xattn_ref[...]).astype(BF16)], axis=-1)
        y = jnp.dot(y, w_out_ref[...], preferred_element_type=F32)
        x1 = x_ref[0, r0:r1, :] + _rms(y, g_post_mix_ref[...])
        x1_h.append(x1)
        h2_h.append(_rms(x1, g_pre_mlp_ref[...]).astype(BF16))

    def act_fn(up):
        return jnp.square(jnp.maximum(up, 0.0)).astype(BF16)

    chunks = list(range(0, w_up_ref.shape[1], ff_chunk))
    h2 = jnp.concatenate(h2_h, axis=0)
    acc = None
    for f0 in chunks:
        w_up_c = w_up_ref[:, f0:f0 + ff_chunk]
        if f0 == chunks[0]:
            act = jnp.concatenate(
                [act_fn(jnp.dot(h, w_up_c, preferred_element_type=F32)) for h in h2_h],
                axis=0)
        else:
            act = act_fn(jnp.dot(h2, w_up_c, preferred_element_type=F32))
        w_down_c = w_down_ref[f0:f0 + ff_chunk, :]
        if f0 != chunks[-1]:
            part = jnp.dot(act, w_down_c, preferred_element_type=F32)
            acc = part if acc is None else acc + part
        else:
            for (r0, r1), x1 in zip(halves, x1_h):
                f = acc[r0:r1] + jnp.dot(act[r0:r1], w_down_c, preferred_element_type=F32)
                o_ref[0, r0:r1, :] = x1 + _rms(f, g_post_mlp_ref[...])


def _mix_mlp(x, y_attn, proj, mem, g_mem, w_mem, conv_w, g_attn, g_conv, g_xattn,
             w_out, g_post_mix, g_pre_mlp, w_up, w_down, g_post_mlp,
             *, attn_width, conv_width, xattn_width, tm=512, ff_chunk=1024):
    B, S, D = x.shape
    n_mem = mem.shape[1]
    d_ff = w_up.shape[1]
    assert conv_width == xattn_width and (3 * attn_width) % conv_width == 0
    cb0 = 3 * attn_width // conv_width
    halo = BF16_SUBLANES
    const = lambda shape: pl.BlockSpec(shape, lambda b, t: (0,) * len(shape),
                                       pipeline_mode=pl.Buffered(1))
    pcol = lambda cb: pl.BlockSpec((1, tm, conv_width), lambda b, t: (b, t, cb))
    phalo = lambda cb: pl.BlockSpec(
        (1, halo, conv_width),
        lambda b, t: (b, jnp.maximum(t * (tm // halo) - 1, 0), cb))
    return pl.pallas_call(
        functools.partial(_mix_mlp_kernel, ff_chunk=ff_chunk),
        grid=(B, S // tm),
        in_specs=[
            pl.BlockSpec((1, tm, D), lambda b, t: (b, t, 0)),
            pl.BlockSpec((1, tm, attn_width), lambda b, t: (b, t, 0)),
            pcol(cb0), pcol(cb0 + 1), pcol(cb0 + 2), pcol(cb0 + 3),
            phalo(cb0 + 1), phalo(cb0 + 2),
            pl.BlockSpec((1, n_mem, D), lambda b, t: (b, 0, 0)),
            const((1, D)), const((D, 2 * xattn_width)), const((CONV_K, conv_width)),
            const((1, attn_width)), const((1, conv_width)), const((1, xattn_width)),
            const((D, D)), const((1, D)),
            const((1, D)), const((D, d_ff)), const((d_ff, D)), const((1, D)),
        ],
        out_specs=pl.BlockSpec((1, tm, D), lambda b, t: (b, t, 0)),
        out_shape=jax.ShapeDtypeStruct((B, S, D), x.dtype),
        scratch_shapes=[pltpu.VMEM((n_mem, xattn_width), BF16),
                        pltpu.VMEM((n_mem, xattn_width), BF16)],
        compiler_params=pltpu.CompilerParams(
            dimension_semantics=("parallel", "arbitrary"),
            vmem_limit_bytes=VMEM_LIMIT_BYTES),
        name="mix_mlp",
    )(x, y_attn, proj, proj, proj, proj, proj, proj, mem, g_mem, w_mem, conv_w,
      g_attn, g_conv, g_xattn, w_out, g_post_mix, g_pre_mlp, w_up, w_down, g_post_mlp)


def kernel(x, mem, positions, g_pre_mix, g_mem, w_in, w_mem_kv, conv_w, g_attn_out,
           g_conv_out, g_xattn_out, w_out, g_post_mix, g_pre_mlp, w_up, w_down,
           g_post_mlp):
    depth = w_in.shape[0]
    attn_width = g_attn_out.shape[1]
    conv_width = g_conv_out.shape[1]
    xattn_width = g_xattn_out.shape[1]
    row = lambda g: g[None, :]
    for l in range(depth):
        proj, qkv4, qkv16, (w_mem_b, w_out_b, w_up_b, w_down_b) = _in_proj(
            x, positions, row(g_pre_mix[l]), w_in[l].astype(BF16),
            (w_mem_kv[l], w_out[l], w_up[l], w_down[l]),
            attn_width=attn_width, xattn_width=xattn_width)
        y_attn = _dil_attn(proj, qkv4, qkv16, attn_width=attn_width)
        x = _mix_mlp(x, y_attn, proj, mem, row(g_mem[l]), w_mem_b, conv_w[l],
                     row(g_attn_out[l]), row(g_conv_out[l]), row(g_xattn_out[l]),
                     w_out_b, row(g_post_mix[l]), row(g_pre_mlp[l]), w_up_b, w_down_b,
                     row(g_post_mlp[l]),
                     attn_width=attn_width, conv_width=conv_width,
                     xattn_width=xattn_width)
    return x
```

```python
import functools

import jax
import jax.numpy as jnp
from jax import lax
from jax.experimental import pallas as pl
from jax.experimental.pallas import tpu as pltpu

F32 = jnp.float32
BF16 = jnp.bfloat16

HEAD_DIM = 64
N_MEM_HEADS = 4
DILATED_PATTERNS = ((128, 1), (512, 4), (2048, 16))
CONV_K = 3
ROPE_THETA = 10000.0
EPS = 1e-6
NEG_INF = -1e30
LOG2_E = 1.4426950408889634

LANES = 128
BF16_SUBLANES = 16
ATTN_BLK = 128
VMEM_LIMIT_BYTES = 56 * 1024 * 1024


def _rms(x, g):
    return x * lax.rsqrt(jnp.mean(x * x, axis=-1, keepdims=True) + EPS) * g


def _in_proj_kernel(x_ref, pos_ref, g_ref, w_f32_ref, *rest, attn_width, xattn_width, chunk):
    n_cast = (len(rest) - 2) // 2
    cast_in, o_ref, cast_out = rest[:n_cast], rest[n_cast], rest[n_cast + 1:-1]
    w_ref = rest[-1]

    @pl.when((pl.program_id(0) == 0) & (pl.program_id(1) == 0))
    def _():
        w_ref[...] = w_f32_ref[...].astype(BF16)

    h = _rms(x_ref[0], g_ref[...]).astype(BF16)

    half = HEAD_DIM // 2
    freq = lax.broadcasted_iota(jnp.int32, (half, 1), 0).astype(F32)
    inv_freq = jnp.float32(ROPE_THETA) ** (-(freq * 2.0 / HEAD_DIM))
    ang = inv_freq * pos_ref[0].astype(F32)
    cos_t, sin_t = jnp.cos(ang), jnp.sin(ang)
    reps = LANES // HEAD_DIM
    cos_k = jnp.concatenate([cos_t, cos_t] * reps, axis=0).T
    sin_k = jnp.concatenate([-sin_t, sin_t] * reps, axis=0).T
    q_scale = HEAD_DIM ** -0.5 * LOG2_E
    cos_q, sin_q = cos_k * q_scale, sin_k * q_scale
    lane = lax.broadcasted_iota(jnp.int32, (1, LANES), 1)
    first_half = (lane % HEAD_DIM) < half

    n_out = o_ref.shape[-1]
    for c0 in range(0, n_out, chunk):
        p = jnp.dot(h, w_ref[:, c0:c0 + chunk], preferred_element_type=F32)
        if c0 < 2 * attn_width:
            cos, sin = (cos_q, sin_q) if c0 < attn_width else (cos_k, sin_k)
            for g0 in range(0, chunk, LANES):
                t = p[:, g0:g0 + LANES]
                rot = jnp.where(first_half,
                                pltpu.roll(t, LANES - half, 1),
                                pltpu.roll(t, half, 1))
                r = t * cos + rot * sin
                o_ref[0, :, c0 + g0:c0 + g0 + LANES] = r.astype(BF16)
        elif c0 + chunk == n_out:
            lo = chunk - xattn_width
            o_ref[0, :, c0:c0 + lo] = p[:, :lo].astype(BF16)
            o_ref[0, :, c0 + lo:c0 + chunk] = (p[:, lo:] * q_scale).astype(BF16)
        else:
            o_ref[0, :, c0:c0 + chunk] = p.astype(BF16)
        if c0 == 2 * attn_width:
            for src, dst in zip(cast_in, cast_out):
                dst[...] = src[...].astype(BF16)


def _in_proj(x, pos3, g, w_in, later_weights, *, attn_width, xattn_width, tm=512, chunk=512):
    B, S, D = x.shape
    n_out = w_in.shape[1]
    n_steps = B * (S // tm)
    slices = []
    for w in later_weights:
        rows = w.shape[0] // n_steps
        assert rows * n_steps == w.shape[0] and rows % BF16_SUBLANES == 0
        slices.append(pl.BlockSpec((rows, w.shape[1]), lambda b, t: (b * (S // tm) + t, 0)))
    out = pl.pallas_call(
        functools.partial(_in_proj_kernel, attn_width=attn_width,
                          xattn_width=xattn_width, chunk=chunk),
        grid=(B, S // tm),
        in_specs=[
            pl.BlockSpec((1, tm, D), lambda b, t: (b, t, 0)),
            pl.BlockSpec((1, 1, tm), lambda b, t: (b, 0, t)),
            pl.BlockSpec((1, D), lambda b, t: (0, 0)),
            pl.BlockSpec((D, n_out), lambda b, t: (0, 0), pipeline_mode=pl.Buffered(1)),
        ] + slices,
        out_specs=[pl.BlockSpec((1, tm, n_out), lambda b, t: (b, t, 0))] + slices,
        out_shape=[jax.ShapeDtypeStruct((B, S, n_out), BF16)]
        + [jax.ShapeDtypeStruct(w.shape, BF16) for w in later_weights],
        scratch_shapes=[pltpu.VMEM(w_in.shape, BF16)],
        compiler_params=pltpu.CompilerParams(
            dimension_semantics=("arbitrary", "arbitrary"),
            vmem_limit_bytes=VMEM_LIMIT_BYTES),
        name="in_proj",
    )(x, pos3, g, w_in, *later_weights)
    return out[0], out[1:]


def _band_block(qb, kb, vb1, bias, lane_lo):
    ms, ls, accs = [], [], []
    for head_lo in (True, False):
        sel = lane_lo if head_lo else jnp.logical_not(lane_lo)
        qh = jnp.where(sel, qb, jnp.zeros_like(qb))
        s = lax.dot_general(qh, kb, (((1,), (1,)), ((), ())),
                            preferred_element_type=F32) + bias
        m = jnp.max(s, axis=-1, keepdims=True)
        p = jnp.exp2(s - m).astype(BF16)
        r = jnp.dot(p, vb1, preferred_element_type=F32)
        accs.append(r[:, :LANES])
        ls.append(r[:, LANES:])
        ms.append(jnp.broadcast_to(m, (m.shape[0], LANES)))
    return (jnp.where(lane_lo, ms[0], ms[1]), jnp.where(lane_lo, ls[0], ls[1]),
            jnp.where(lane_lo, accs[0], accs[1]))


COPY_ROWS = 64
DEINTERLEAVE = 4


def _for_each_split(seq, region, body):
    run = region // DEINTERLEAVE
    span = DEINTERLEAVE * COPY_ROWS
    steps_per_region = region // span

    def step(t, carry):
        strided0 = pl.multiple_of(t * span, span)
        dense0 = pl.multiple_of((t // steps_per_region) * region
                                + (t % steps_per_region) * COPY_ROWS, COPY_ROWS)
        for j in range(DEINTERLEAVE):
            body(pl.ds(strided0 + j, COPY_ROWS, stride=DEINTERLEAVE),
                 pl.ds(dense0 + j * run, COPY_ROWS))
        return carry

    lax.fori_loop(0, seq // span, step, 0)


def _merge_softmax(a, b):
    (m_a, l_a, a_a), (m_b, l_b, a_b) = a, b
    m = jnp.maximum(m_a, m_b)
    w_a = jnp.exp2(m_a - m)
    w_b = jnp.exp2(m_b - m)
    return m, l_a * w_a + l_b * w_b, a_a * w_a + a_b * w_b


def _dil_attn_kernel(trips_ref, q_ref, k_ref, v_ref, o_ref,
                     qf, kf, vf, q4f, k4f, v4f, qp, kp, vp1,
                     m_p4, l_p4, a_p4, m_tmp, l_tmp, a_tmp, bias_ref,
                     *, seq, patterns, unroll):
    blk = ATTN_BLK
    n_blocks = seq // blk
    lane_lo = lax.broadcasted_iota(jnp.int32, (1, LANES), 1) < HEAD_DIM
    assert [d for _, d in patterns] == [1, DEINTERLEAVE, DEINTERLEAVE ** 2]
    assert all(w // d == blk for w, d in patterns)
    run4 = seq // DEINTERLEAVE
    run16 = run4 // DEINTERLEAVE
    blocks_per_run4 = run4 // blk
    blocks_per_run16 = run16 // blk
    assert n_blocks % unroll == 0 and unroll % blocks_per_run4 == 0
    n_trips = trips_ref[0]

    @pl.when((pl.program_id(0) == 0) & (pl.program_id(1) == 0))
    def _():
        qi = lax.broadcasted_iota(jnp.int32, (blk, 2 * blk), 0)
        kj = lax.broadcasted_iota(jnp.int32, (blk, 2 * blk), 1)
        bias_ref[0] = jnp.where(kj <= qi, 0.0, NEG_INF).astype(F32)
        bias_ref[1] = jnp.where((kj >= qi) & (kj <= qi + blk), 0.0, NEG_INF).astype(F32)
        vp1[:, LANES:] = jnp.ones((seq, LANES), BF16)

    pat4 = (m_p4, l_p4, a_p4)
    tmp = (m_tmp, l_tmp, a_tmp)

    def attend(sub_len, q_src, k_src, finish):
        nb = sub_len // blk
        assert nb >= 2

        def trip_body(trip, carry):
            for u in range(unroll):
                g = trip * unroll + u
                n = g % nb
                q0 = pl.multiple_of(g * blk, blk)
                k0 = pl.multiple_of(jnp.where(n > 0, q0 - blk, q0), blk)
                triple = _band_block(q_src[pl.ds(q0, blk), :],
                                     k_src[pl.ds(k0, 2 * blk), :],
                                     vp1[pl.ds(k0, 2 * blk), :],
                                     bias_ref[jnp.minimum(n, 1)], lane_lo)
                finish(trip, u, q0, triple)
            return carry

        lax.fori_loop(0, n_trips, trip_body, 0)

    qf[...] = q_ref[0].astype(F32)
    kf[...] = k_ref[0].astype(F32)
    vf[...] = v_ref[0].astype(F32)

    def gather4(strided, dense):
        for src, dst_f, dst_b in ((qf, q4f, qp), (kf, k4f, kp)):
            rows = src[strided, :]
            dst_f[dense, :] = rows
            dst_b[dense, :] = rows.astype(BF16)
        rows = vf[strided, :]
        v4f[dense, :] = rows
        vp1[dense, :LANES] = rows.astype(BF16)

    def store_pat4(trip, u, q0, triple):
        for ref, val in zip(pat4, triple):
            ref[pl.ds(q0, blk), :] = val

    _for_each_split(seq, seq, gather4)
    attend(run4, qp, kp, store_pat4)

    def gather16(strided, dense):
        qp[dense, :] = q4f[strided, :].astype(BF16)
        kp[dense, :] = k4f[strided, :].astype(BF16)
        vp1[dense, :LANES] = v4f[strided, :].astype(BF16)

    def fold_into_pat4(trip, u, q0, triple):
        r_static, idx = divmod(u, blocks_per_run4)
        j, a0 = idx // blocks_per_run16, (idx % blocks_per_run16) * blk
        r_dyn = pl.multiple_of(trip * (unroll // blocks_per_run4) * run4, run4)
        rows = pl.ds(r_dyn + r_static * run4 + j + DEINTERLEAVE * a0, blk,
                     stride=DEINTERLEAVE)
        merged = _merge_softmax(tuple(ref[rows, :] for ref in pat4), triple)
        for ref, val in zip(pat4, merged):
            ref[rows, :] = val

    _for_each_split(seq, run4, gather16)
    attend(run16, qp, kp, fold_into_pat4)

    def finish_natural(trip, u, q0, triple):
        sub = blk // DEINTERLEAVE
        i_dyn = pl.multiple_of(trip * (unroll * sub), unroll * sub)
        for j in range(DEINTERLEAVE):
            src = pl.ds(i_dyn + j * run4 + u * sub, sub)
            for t_ref, p_ref in zip(tmp, pat4):
                t_ref[u, pl.ds(j, sub, stride=DEINTERLEAVE), :] = p_ref[src, :]
        _, l, acc = _merge_softmax(tuple(t_ref[u] for t_ref in tmp), triple)
        o_ref[0, pl.ds(q0, blk), :] = (acc / l).astype(o_ref.dtype)

    vp1[:, :LANES] = v_ref[0]
    attend(seq, q_ref.at[0], k_ref.at[0], finish_natural)


def _dil_attn(proj, *, attn_width, unroll=32):
    B, S, _ = proj.shape
    n_pairs = attn_width // LANES
    trips = jnp.full((1,), S // ATTN_BLK // unroll, jnp.int32)
    kern = functools.partial(_dil_attn_kernel, seq=S, patterns=DILATED_PATTERNS,
                             unroll=unroll)
    col = lambda off: (lambda b, hp: (b, 0, off + hp))
    f32_buf = pltpu.VMEM((S, LANES), F32)
    bf16_buf = pltpu.VMEM((S, LANES), BF16)
    tmp_buf = pltpu.VMEM((unroll, ATTN_BLK, LANES), F32)
    return pl.pallas_call(
        kern,
        grid=(B, n_pairs),
        in_specs=[pl.BlockSpec(memory_space=pltpu.SMEM),
                  pl.BlockSpec((1, S, LANES), col(0)),
                  pl.BlockSpec((1, S, LANES), col(n_pairs)),
                  pl.BlockSpec((1, S, LANES), col(2 * n_pairs))],
        out_specs=pl.BlockSpec((1, S, LANES), lambda b, hp: (b, 0, hp)),
        out_shape=jax.ShapeDtypeStruct((B, S, attn_width), BF16),
        scratch_shapes=[f32_buf, f32_buf, f32_buf,
                        f32_buf, f32_buf, f32_buf,
                        bf16_buf, bf16_buf,
                        pltpu.VMEM((S, 2 * LANES), BF16),
                        f32_buf, f32_buf, f32_buf,
                        tmp_buf, tmp_buf, tmp_buf,
                        pltpu.VMEM((2, ATTN_BLK, 2 * ATTN_BLK), F32)],
        compiler_params=pltpu.CompilerParams(
            dimension_semantics=("arbitrary", "arbitrary"),
            vmem_limit_bytes=VMEM_LIMIT_BYTES),
        name="dil_attn",
    )(trips, proj, proj, proj)


def _mix_mlp_kernel(x_ref, ya_ref, bg_ref, cg_ref, u_ref, qx_ref, hc_ref, hu_ref,
                    mem_ref, g_mem_ref, w_mem_ref, conv_w_ref,
                    g_attn_ref, g_conv_ref, g_xattn_ref, w_out_ref, g_post_mix_ref,
                    g_pre_mlp_ref, w_up_ref, w_down_ref, g_post_mlp_ref,
                    o_ref, km_ref, vm_ref, *, ff_chunk):
    t = pl.program_id(1)
    xw = km_ref.shape[1]

    @pl.when(t == 0)
    def _():
        hm = _rms(mem_ref[0], g_mem_ref[...]).astype(BF16)
        kv = jnp.dot(hm, w_mem_ref[...], preferred_element_type=F32)
        km_ref[...] = kv[:, :xw].astype(BF16)
        vm_ref[...] = kv[:, xw:].astype(BF16)

    tm = x_ref.shape[1]

    z = cg_ref[0].astype(F32) * u_ref[0].astype(F32)
    hz = hc_ref[0].astype(F32) * hu_ref[0].astype(F32)
    hz = jnp.where(t > 0, hz, jnp.zeros_like(hz))
    z_ext = jnp.concatenate([hz, z], axis=0)
    cw = conv_w_ref[...]
    y_conv = z * cw[CONV_K - 1:CONV_K, :]
    for back in range(1, CONV_K):
        lo = BF16_SUBLANES - back
        y_conv = y_conv + z_ext[lo:lo + tm, :] * cw[CONV_K - 1 - back:CONV_K - back, :]
    y_conv = bg_ref[0].astype(F32) * y_conv

    qx = qx_ref[0]
    km = km_ref[...]
    vm = vm_ref[...]
    lane = lax.broadcasted_iota(jnp.int32, (1, xw), 1)
    xhd = xw // N_MEM_HEADS
    assert xhd == HEAD_DIM
    sels = [(lane >= hd * xhd) & (lane < (hd + 1) * xhd) for hd in range(N_MEM_HEADS)]

    def scores(hd):
        qh = jnp.where(sels[hd], qx, jnp.zeros_like(qx))
        return lax.dot_general(qh, km, (((1,), (1,)), ((), ())),
                               preferred_element_type=F32)

    def softmax(sc):
        p = jnp.exp2(sc - jnp.max(sc, axis=-1, keepdims=True))
        return (p / jnp.sum(p, axis=-1, keepdims=True)).astype(BF16)

    sc, p, y_x = {}, {}, jnp.zeros((tm, xw), F32)
    for hd in range(N_MEM_HEADS + 2):
        if hd < N_MEM_HEADS:
            sc[hd] = scores(hd)
        if 0 <= hd - 1 < N_MEM_HEADS:
            p[hd - 1] = softmax(sc.pop(hd - 1))
        if 0 <= hd - 2 < N_MEM_HEADS:
            o = jnp.dot(p.pop(hd - 2), vm, preferred_element_type=F32)
            y_x = jnp.where(sels[hd - 2], o, y_x)

    halves = ((0, tm // 2), (tm // 2, tm))

    x1_h, h2_h = [], []
    for r0, r1 in halves:
        y = jnp.concatenate([
            _rms(ya_ref[0, r0:r1, :].astype(F32), g_attn_ref[...]).astype(BF16),
            _rms(y_conv[r0:r1], g_conv_ref[...]).astype(BF16),
            _rms(y_x[r0:r1], g_xattn_ref[...]).astype(BF16)], axis=-1)
        y = jnp.dot(y, w_out_ref[...], preferred_element_type=F32)
        x1 = x_ref[0, r0:r1, :] + _rms(y, g_post_mix_ref[...])
        x1_h.append(x1)
        h2_h.append(_rms(x1, g_pre_mlp_ref[...]).astype(BF16))

    def act_fn(up):
        return jnp.square(jnp.maximum(up, 0.0)).astype(BF16)

    chunks = list(range(0, w_up_ref.shape[1], ff_chunk))
    h2 = jnp.concatenate(h2_h, axis=0)
    acc = None
    for f0 in chunks:
        w_up_c = w_up_ref[:, f0:f0 + ff_chunk]
        if f0 == chunks[0]:
            act = jnp.concatenate(
                [act_fn(jnp.dot(h, w_up_c, preferred_element_type=F32)) for h in h2_h],
                axis=0)
        else:
            act = act_fn(jnp.dot(h2, w_up_c, preferred_element_type=F32))
        w_down_c = w_down_ref[f0:f0 + ff_chunk, :]
        if f0 != chunks[-1]:
            part = jnp.dot(act, w_down_c, preferred_element_type=F32)
            acc = part if acc is None else acc + part
        else:
            for (r0, r1), x1 in zip(halves, x1_h):
                f = acc[r0:r1] + jnp.dot(act[r0:r1], w_down_c, preferred_element_type=F32)
                o_ref[0, r0:r1, :] = x1 + _rms(f, g_post_mlp_ref[...])


def _mix_mlp(x, y_attn, proj, mem, g_mem, w_mem, conv_w, g_attn, g_conv, g_xattn,
             w_out, g_post_mix, g_pre_mlp, w_up, w_down, g_post_mlp,
             *, attn_width, conv_width, xattn_width, tm=512, ff_chunk=1024):
    B, S, D = x.shape
    n_mem = mem.shape[1]
    d_ff = w_up.shape[1]
    assert conv_width == xattn_width and (3 * attn_width) % conv_width == 0
    cb0 = 3 * attn_width // conv_width
    halo = BF16_SUBLANES
    const = lambda shape: pl.BlockSpec(shape, lambda b, t: (0,) * len(shape),
                                       pipeline_mode=pl.Buffered(1))
    pcol = lambda cb: pl.BlockSpec((1, tm, conv_width), lambda b, t: (b, t, cb))
    phalo = lambda cb: pl.BlockSpec(
        (1, halo, conv_width),
        lambda b, t: (b, jnp.maximum(t * (tm // halo) - 1, 0), cb))
    return pl.pallas_call(
        functools.partial(_mix_mlp_kernel, ff_chunk=ff_chunk),
        grid=(B, S // tm),
        in_specs=[
            pl.BlockSpec((1, tm, D), lambda b, t: (b, t, 0)),
            pl.BlockSpec((1, tm, attn_width), lambda b, t: (b, t, 0)),
            pcol(cb0), pcol(cb0 + 1), pcol(cb0 + 2), pcol(cb0 + 3),
            phalo(cb0 + 1), phalo(cb0 + 2),
            pl.BlockSpec((1, n_mem, D), lambda b, t: (b, 0, 0)),
            const((1, D)), const((D, 2 * xattn_width)), const((CONV_K, conv_width)),
            const((1, attn_width)), const((1, conv_width)), const((1, xattn_width)),
            const((D, D)), const((1, D)),
            const((1, D)), const((D, d_ff)), const((d_ff, D)), const((1, D)),
        ],
        out_specs=pl.BlockSpec((1, tm, D), lambda b, t: (b, t, 0)),
        out_shape=jax.ShapeDtypeStruct((B, S, D), x.dtype),
        scratch_shapes=[pltpu.VMEM((n_mem, xattn_width), BF16),
                        pltpu.VMEM((n_mem, xattn_width), BF16)],
        compiler_params=pltpu.CompilerParams(
            dimension_semantics=("parallel", "arbitrary"),
            vmem_limit_bytes=VMEM_LIMIT_BYTES),
        name="mix_mlp",
    )(x, y_attn, proj, proj, proj, proj, proj, proj, mem, g_mem, w_mem, conv_w,
      g_attn, g_conv, g_xattn, w_out, g_post_mix, g_pre_mlp, w_up, w_down, g_post_mlp)


def kernel(x, mem, positions, g_pre_mix, g_mem, w_in, w_mem_kv, conv_w, g_attn_out,
           g_conv_out, g_xattn_out, w_out, g_post_mix, g_pre_mlp, w_up, w_down,
           g_post_mlp):
    depth = w_in.shape[0]
    attn_width = g_attn_out.shape[1]
    conv_width = g_conv_out.shape[1]
    xattn_width = g_xattn_out.shape[1]
    pos3 = positions[:, None, :]
    row = lambda g: g[None, :]
    for l in range(depth):
        proj, (w_mem_b, w_out_b, w_up_b, w_down_b) = _in_proj(
            x, pos3, row(g_pre_mix[l]), w_in[l],
            (w_mem_kv[l], w_out[l], w_up[l], w_down[l]),
            attn_width=attn_width, xattn_width=xattn_width)
        y_attn = _dil_attn(proj, attn_width=attn_width)
        x = _mix_mlp(x, y_attn, proj, mem, row(g_mem[l]), w_mem_b, conv_w[l],
                     row(g_attn_out[l]), row(g_conv_out[l]), row(g_xattn_out[l]),
                     w_out_b, row(g_post_mix[l]), row(g_pre_mlp[l]), w_up_b, w_down_b,
                     row(g_post_mlp[l]),
                     attn_width=attn_width, conv_width=conv_width,
                     xattn_width=xattn_width)
    return x
```

```python
import functools

import jax
import jax.numpy as jnp
from jax import lax
from jax.experimental import pallas as pl
from jax.experimental.pallas import tpu as pltpu

F32 = jnp.float32
BF16 = jnp.bfloat16

HEAD_DIM = 64
N_MEM_HEADS = 4
DILATED_PATTERNS = ((128, 1), (512, 4), (2048, 16))
CONV_K = 3
ROPE_THETA = 10000.0
EPS = 1e-6
NEG_INF = -1e30
LOG2_E = 1.4426950408889634

LANES = 128
BF16_SUBLANES = 16
ATTN_BLK = 128
VMEM_LIMIT_BYTES = 56 * 1024 * 1024


def _rms(x, g):
    return x * lax.rsqrt(jnp.mean(x * x, axis=-1, keepdims=True) + EPS) * g


def _in_proj_kernel(x_ref, pos_ref, g_ref, w_f32_ref, *rest, attn_width, xattn_width, chunk):
    n_cast = (len(rest) - 2) // 2
    cast_in, o_ref, cast_out = rest[:n_cast], rest[n_cast], rest[n_cast + 1:-1]
    w_ref = rest[-1]

    @pl.when((pl.program_id(0) == 0) & (pl.program_id(1) == 0))
    def _():
        w_ref[...] = w_f32_ref[...].astype(BF16)

    h = _rms(x_ref[0], g_ref[...]).astype(BF16)

    half = HEAD_DIM // 2
    freq = lax.broadcasted_iota(jnp.int32, (half, 1), 0).astype(F32)
    inv_freq = jnp.float32(ROPE_THETA) ** (-(freq * 2.0 / HEAD_DIM))
    ang = inv_freq * pos_ref[0].astype(F32)
    cos_t, sin_t = jnp.cos(ang), jnp.sin(ang)
    reps = LANES // HEAD_DIM
    cos_k = jnp.concatenate([cos_t, cos_t] * reps, axis=0).T
    sin_k = jnp.concatenate([-sin_t, sin_t] * reps, axis=0).T
    q_scale = HEAD_DIM ** -0.5 * LOG2_E
    cos_q, sin_q = cos_k * q_scale, sin_k * q_scale
    lane = lax.broadcasted_iota(jnp.int32, (1, LANES), 1)
    first_half = (lane % HEAD_DIM) < half

    n_out = o_ref.shape[-1]
    for c0 in range(0, n_out, chunk):
        p = jnp.dot(h, w_ref[:, c0:c0 + chunk], preferred_element_type=F32)
        if c0 < 2 * attn_width:
            cos, sin = (cos_q, sin_q) if c0 < attn_width else (cos_k, sin_k)
            for g0 in range(0, chunk, LANES):
                t = p[:, g0:g0 + LANES]
                rot = jnp.where(first_half,
                                pltpu.roll(t, LANES - half, 1),
                                pltpu.roll(t, half, 1))
                r = t * cos + rot * sin
                o_ref[0, :, c0 + g0:c0 + g0 + LANES] = r.astype(BF16)
        elif c0 + chunk == n_out:
            lo = chunk - xattn_width
            o_ref[0, :, c0:c0 + lo] = p[:, :lo].astype(BF16)
            o_ref[0, :, c0 + lo:c0 + chunk] = (p[:, lo:] * q_scale).astype(BF16)
        else:
            o_ref[0, :, c0:c0 + chunk] = p.astype(BF16)
        if c0 == 2 * attn_width:
            for src, dst in zip(cast_in, cast_out):
                dst[...] = src[...].astype(BF16)


def _in_proj(x, pos3, g, w_in, later_weights, *, attn_width, xattn_width, tm=512, chunk=512):
    B, S, D = x.shape
    n_out = w_in.shape[1]
    n_steps = B * (S // tm)
    slices = []
    for w in later_weights:
        rows = w.shape[0] // n_steps
        assert rows * n_steps == w.shape[0] and rows % BF16_SUBLANES == 0
        slices.append(pl.BlockSpec((rows, w.shape[1]), lambda b, t: (b * (S // tm) + t, 0)))
    out = pl.pallas_call(
        functools.partial(_in_proj_kernel, attn_width=attn_width,
                          xattn_width=xattn_width, chunk=chunk),
        grid=(B, S // tm),
        in_specs=[
            pl.BlockSpec((1, tm, D), lambda b, t: (b, t, 0)),
            pl.BlockSpec((1, 1, tm), lambda b, t: (b, 0, t)),
            pl.BlockSpec((1, D), lambda b, t: (0, 0)),
            pl.BlockSpec((D, n_out), lambda b, t: (0, 0), pipeline_mode=pl.Buffered(1)),
        ] + slices,
        out_specs=[pl.BlockSpec((1, tm, n_out), lambda b, t: (b, t, 0))] + slices,
        out_shape=[jax.ShapeDtypeStruct((B, S, n_out), BF16)]
        + [jax.ShapeDtypeStruct(w.shape, BF16) for w in later_weights],
        scratch_shapes=[pltpu.VMEM(w_in.shape, BF16)],
        compiler_params=pltpu.CompilerParams(
            dimension_semantics=("arbitrary", "arbitrary"),
            vmem_limit_bytes=VMEM_LIMIT_BYTES),
        name="in_proj",
    )(x, pos3, g, w_in, *later_weights)
    return out[0], out[1:]


def _band_block(qb, kb, vb1, bias, lane_lo):
    ms, ls, accs = [], [], []
    for head_lo in (True, False):
        sel = lane_lo if head_lo else jnp.logical_not(lane_lo)
        qh = jnp.where(sel, qb, jnp.zeros_like(qb))
        s = lax.dot_general(qh, kb, (((1,), (1,)), ((), ())),
                            preferred_element_type=F32) + bias
        m = jnp.max(s, axis=-1, keepdims=True)
        p = jnp.exp2(s - m).astype(BF16)
        r = jnp.dot(p, vb1, preferred_element_type=F32)
        accs.append(r[:, :LANES])
        ls.append(r[:, LANES:])
        ms.append(jnp.broadcast_to(m, (m.shape[0], LANES)))
    return (jnp.where(lane_lo, ms[0], ms[1]), jnp.where(lane_lo, ls[0], ls[1]),
            jnp.where(lane_lo, accs[0], accs[1]))


COPY_ROWS = 64
MERGE_ROWS = 32
DEINTERLEAVE = 4


def _for_each_split(seq, region, body):
    run = region // DEINTERLEAVE
    span = DEINTERLEAVE * COPY_ROWS
    steps_per_region = region // span

    def step(t, carry):
        strided0 = pl.multiple_of(t * span, span)
        dense0 = pl.multiple_of((t // steps_per_region) * region
                                + (t % steps_per_region) * COPY_ROWS, COPY_ROWS)
        for j in range(DEINTERLEAVE):
            body(pl.ds(strided0 + j, COPY_ROWS, stride=DEINTERLEAVE),
                 pl.ds(dense0 + j * run, COPY_ROWS))
        return carry

    lax.fori_loop(0, seq // span, step, 0)


def _merge_softmax(a, b):
    (m_a, l_a, a_a), (m_b, l_b, a_b) = a, b
    m = jnp.maximum(m_a, m_b)
    w_a = jnp.exp2(m_a - m)
    w_b = jnp.exp2(m_b - m)
    return m, l_a * w_a + l_b * w_b, a_a * w_a + a_b * w_b


def _dil_attn_kernel(trips_ref, q_ref, k_ref, v_ref, o_ref,
                     qf, kf, vf, q4f, k4f, v4f, qp, kp, vp1,
                     m_p4, l_p4, a_p4, m_tmp, l_tmp, a_tmp, bias_ref,
                     *, seq, patterns, unroll):
    blk = ATTN_BLK
    n_blocks = seq // blk
    lane_lo = lax.broadcasted_iota(jnp.int32, (1, LANES), 1) < HEAD_DIM
    assert [d for _, d in patterns] == [1, DEINTERLEAVE, DEINTERLEAVE ** 2]
    assert all(w // d == blk for w, d in patterns)
    run4 = seq // DEINTERLEAVE
    run16 = run4 // DEINTERLEAVE
    blocks_per_run4 = run4 // blk
    blocks_per_run16 = run16 // blk
    assert n_blocks % unroll == 0 and unroll % blocks_per_run4 == 0
    n_trips = trips_ref[0]

    @pl.when((pl.program_id(0) == 0) & (pl.program_id(1) == 0))
    def _():
        qi = lax.broadcasted_iota(jnp.int32, (blk, 2 * blk), 0)
        kj = lax.broadcasted_iota(jnp.int32, (blk, 2 * blk), 1)
        bias_ref[0] = jnp.where(kj <= qi, 0.0, NEG_INF).astype(F32)
        bias_ref[1] = jnp.where((kj >= qi) & (kj <= qi + blk), 0.0, NEG_INF).astype(F32)
        vp1[:, LANES:] = jnp.ones((seq, LANES), BF16)

    pat4 = (m_p4, l_p4, a_p4)
    tmp = (m_tmp, l_tmp, a_tmp)

    def attend(sub_len, q_src, k_src, finish):
        nb = sub_len // blk
        assert nb >= 2

        def trip_body(trip, carry):
            for u in range(unroll):
                g = trip * unroll + u
                n = g % nb
                q0 = pl.multiple_of(g * blk, blk)
                k0 = pl.multiple_of(jnp.where(n > 0, q0 - blk, q0), blk)
                triple = _band_block(q_src[pl.ds(q0, blk), :],
                                     k_src[pl.ds(k0, 2 * blk), :],
                                     vp1[pl.ds(k0, 2 * blk), :],
                                     bias_ref[jnp.minimum(n, 1)], lane_lo)
                finish(trip, u, q0, triple)
            return carry

        lax.fori_loop(0, n_trips, trip_body, 0)

    qf[...] = q_ref[0].astype(F32)
    kf[...] = k_ref[0].astype(F32)
    vf[...] = v_ref[0].astype(F32)

    def gather4(strided, dense):
        for src, dst_f, dst_b in ((qf, q4f, qp), (kf, k4f, kp)):
            rows = src[strided, :]
            dst_f[dense, :] = rows
            dst_b[dense, :] = rows.astype(BF16)
        rows = vf[strided, :]
        v4f[dense, :] = rows
        vp1[dense, :LANES] = rows.astype(BF16)

    def store_pat4(trip, u, q0, triple):
        for ref, val in zip(pat4, triple):
            ref[pl.ds(q0, blk), :] = val

    _for_each_split(seq, seq, gather4)
    attend(run4, qp, kp, store_pat4)

    def gather16(strided, dense):
        qp[dense, :] = q4f[strided, :].astype(BF16)
        kp[dense, :] = k4f[strided, :].astype(BF16)
        vp1[dense, :LANES] = v4f[strided, :].astype(BF16)

    def fold_into_pat4(trip, u, q0, triple):
        r_static, idx = divmod(u, blocks_per_run4)
        j, a0 = idx // blocks_per_run16, (idx % blocks_per_run16) * blk
        r_dyn = pl.multiple_of(trip * (unroll // blocks_per_run4) * run4, run4)
        for c0 in range(0, blk, MERGE_ROWS):
            rows = pl.ds(r_dyn + r_static * run4 + j + DEINTERLEAVE * (a0 + c0), MERGE_ROWS,
                         stride=DEINTERLEAVE)
            merged = _merge_softmax(tuple(ref[rows, :] for ref in pat4),
                                    tuple(x[c0:c0 + MERGE_ROWS] for x in triple))
            for ref, val in zip(pat4, merged):
                ref[rows, :] = val

    _for_each_split(seq, run4, gather16)
    attend(run16, qp, kp, fold_into_pat4)

    def finish_natural(trip, u, q0, triple):
        sub = blk // DEINTERLEAVE
        i_dyn = pl.multiple_of(trip * (unroll * sub), unroll * sub)
        piece = MERGE_ROWS // DEINTERLEAVE
        for c0 in range(0, blk, MERGE_ROWS):
            for j in range(DEINTERLEAVE):
                src = pl.ds(i_dyn + j * run4 + u * sub + c0 // DEINTERLEAVE, piece)
                for t_ref, p_ref in zip(tmp, pat4):
                    t_ref[u, pl.ds(c0 + j, piece, stride=DEINTERLEAVE), :] = p_ref[src, :]
            _, l, acc = _merge_softmax(tuple(t_ref[u, c0:c0 + MERGE_ROWS] for t_ref in tmp),
                                       tuple(x[c0:c0 + MERGE_ROWS] for x in triple))
            o_ref[0, pl.ds(q0 + c0, MERGE_ROWS), :] = (acc / l).astype(o_ref.dtype)

    vp1[:, :LANES] = v_ref[0]
    attend(seq, q_ref.at[0], k_ref.at[0], finish_natural)


def _dil_attn(proj, *, attn_width, unroll=32):
    B, S, _ = proj.shape
    n_pairs = attn_width // LANES
    trips = jnp.full((1,), S // ATTN_BLK // unroll, jnp.int32)
    kern = functools.partial(_dil_attn_kernel, seq=S, patterns=DILATED_PATTERNS,
                             unroll=unroll)
    col = lambda off: (lambda b, hp: (b, 0, off + hp))
    f32_buf = pltpu.VMEM((S, LANES), F32)
    bf16_buf = pltpu.VMEM((S, LANES), BF16)
    tmp_buf = pltpu.VMEM((unroll, ATTN_BLK, LANES), F32)
    return pl.pallas_call(
        kern,
        grid=(B, n_pairs),
        in_specs=[pl.BlockSpec(memory_space=pltpu.SMEM),
                  pl.BlockSpec((1, S, LANES), col(0)),
                  pl.BlockSpec((1, S, LANES), col(n_pairs)),
                  pl.BlockSpec((1, S, LANES), col(2 * n_pairs))],
        out_specs=pl.BlockSpec((1, S, LANES), lambda b, hp: (b, 0, hp)),
        out_shape=jax.ShapeDtypeStruct((B, S, attn_width), BF16),
        scratch_shapes=[f32_buf, f32_buf, f32_buf,
                        f32_buf, f32_buf, f32_buf,
                        bf16_buf, bf16_buf,
                        pltpu.VMEM((S, 2 * LANES), BF16),
                        f32_buf, f32_buf, f32_buf,
                        tmp_buf, tmp_buf, tmp_buf,
                        pltpu.VMEM((2, ATTN_BLK, 2 * ATTN_BLK), F32)],
        compiler_params=pltpu.CompilerParams(
            dimension_semantics=("arbitrary", "arbitrary"),
            vmem_limit_bytes=VMEM_LIMIT_BYTES),
        name="dil_attn",
    )(trips, proj, proj, proj)


def _mix_mlp_kernel(x_ref, ya_ref, bg_ref, cg_ref, u_ref, qx_ref, hc_ref, hu_ref,
                    mem_ref, g_mem_ref, w_mem_ref, conv_w_ref,
                    g_attn_ref, g_conv_ref, g_xattn_ref, w_out_ref, g_post_mix_ref,
                    g_pre_mlp_ref, w_up_ref, w_down_ref, g_post_mlp_ref,
                    o_ref, km_ref, vm_ref, *, ff_chunk):
    t = pl.program_id(1)
    xw = km_ref.shape[1]

    @pl.when(t == 0)
    def _():
        hm = _rms(mem_ref[0], g_mem_ref[...]).astype(BF16)
        kv = jnp.dot(hm, w_mem_ref[...], preferred_element_type=F32)
        km_ref[...] = kv[:, :xw].astype(BF16)
        vm_ref[...] = kv[:, xw:].astype(BF16)

    tm = x_ref.shape[1]

    z = cg_ref[0].astype(F32) * u_ref[0].astype(F32)
    hz = hc_ref[0].astype(F32) * hu_ref[0].astype(F32)
    hz = jnp.where(t > 0, hz, jnp.zeros_like(hz))
    z_ext = jnp.concatenate([hz, z], axis=0)
    cw = conv_w_ref[...]
    y_conv = z * cw[CONV_K - 1:CONV_K, :]
    for back in range(1, CONV_K):
        lo = BF16_SUBLANES - back
        y_conv = y_conv + z_ext[lo:lo + tm, :] * cw[CONV_K - 1 - back:CONV_K - back, :]
    y_conv = bg_ref[0].astype(F32) * y_conv

    qx = qx_ref[0]
    km = km_ref[...]
    vm = vm_ref[...]
    lane = lax.broadcasted_iota(jnp.int32, (1, xw), 1)
    xhd = xw // N_MEM_HEADS
    assert xhd == HEAD_DIM
    sels = [(lane >= hd * xhd) & (lane < (hd + 1) * xhd) for hd in range(N_MEM_HEADS)]

    def scores(hd):
        qh = jnp.where(sels[hd], qx, jnp.zeros_like(qx))
        return lax.dot_general(qh, km, (((1,), (1,)), ((), ())),
                               preferred_element_type=F32)

    def softmax(sc):
        p = jnp.exp2(sc - jnp.max(sc, axis=-1, keepdims=True))
        return (p / jnp.sum(p, axis=-1, keepdims=True)).astype(BF16)

    sc, p, y_x = {}, {}, jnp.zeros((tm, xw), F32)
    for hd in range(N_MEM_HEADS + 2):
        if hd < N_MEM_HEADS:
            sc[hd] = scores(hd)
        if 0 <= hd - 1 < N_MEM_HEADS:
            p[hd - 1] = softmax(sc.pop(hd - 1))
        if 0 <= hd - 2 < N_MEM_HEADS:
            o = jnp.dot(p.pop(hd - 2), vm, preferred_element_type=F32)
            y_x = jnp.where(sels[hd - 2], o, y_x)

    halves = ((0, tm // 2), (tm // 2, tm))

    x1_h, h2_h = [], []
    for r0, r1 in halves:
        y = jnp.concatenate([
            _rms(ya_ref[0, r0:r1, :].astype(F32), g_attn_ref[...]).astype(BF16),
            _rms(y_conv[r0:r1], g_conv_ref[...]).astype(BF16),
            _rms(y_x[r0:r1], g_xattn_ref[...]).astype(BF16)], axis=-1)
        y = jnp.dot(y, w_out_ref[...], preferred_element_type=F32)
        x1 = x_ref[0, r0:r1, :] + _rms(y, g_post_mix_ref[...])
        x1_h.append(x1)
        h2_h.append(_rms(x1, g_pre_mlp_ref[...]).astype(BF16))

    def act_fn(up):
        return jnp.square(jnp.maximum(up, 0.0)).astype(BF16)

    chunks = list(range(0, w_up_ref.shape[1], ff_chunk))
    h2 = jnp.concatenate(h2_h, axis=0)
    acc = None
    for f0 in chunks:
        w_up_c = w_up_ref[:, f0:f0 + ff_chunk]
        if f0 == chunks[0]:
            act = jnp.concatenate(
                [act_fn(jnp.dot(h, w_up_c, preferred_element_type=F32)) for h in h2_h],
                axis=0)
        else:
            act = act_fn(jnp.dot(h2, w_up_c, preferred_element_type=F32))
        w_down_c = w_down_ref[f0:f0 + ff_chunk, :]
        if f0 != chunks[-1]:
            part = jnp.dot(act, w_down_c, preferred_element_type=F32)
            acc = part if acc is None else acc + part
        else:
            for (r0, r1), x1 in zip(halves, x1_h):
                f = acc[r0:r1] + jnp.dot(act[r0:r1], w_down_c, preferred_element_type=F32)
                o_ref[0, r0:r1, :] = x1 + _rms(f, g_post_mlp_ref[...])


def _mix_mlp(x, y_attn, proj, mem, g_mem, w_mem, conv_w, g_attn, g_conv, g_xattn,
             w_out, g_post_mix, g_pre_mlp, w_up, w_down, g_post_mlp,
             *, attn_width, conv_width, xattn_width, tm=512, ff_chunk=1024):
    B, S, D = x.shape
    n_mem = mem.shape[1]
    d_ff = w_up.shape[1]
    assert conv_width == xattn_width and (3 * attn_width) % conv_width == 0
    cb0 = 3 * attn_width // conv_width
    halo = BF16_SUBLANES
    const = lambda shape: pl.BlockSpec(shape, lambda b, t: (0,) * len(shape),
                                       pipeline_mode=pl.Buffered(1))
    pcol = lambda cb: pl.BlockSpec((1, tm, conv_width), lambda b, t: (b, t, cb))
    phalo = lambda cb: pl.BlockSpec(
        (1, halo, conv_width),
        lambda b, t: (b, jnp.maximum(t * (tm // halo) - 1, 0), cb))
    return pl.pallas_call(
        functools.partial(_mix_mlp_kernel, ff_chunk=ff_chunk),
        grid=(B, S // tm),
        in_specs=[
            pl.BlockSpec((1, tm, D), lambda b, t: (b, t, 0)),
            pl.BlockSpec((1, tm, attn_width), lambda b, t: (b, t, 0)),
            pcol(cb0), pcol(cb0 + 1), pcol(cb0 + 2), pcol(cb0 + 3),
            phalo(cb0 + 1), phalo(cb0 + 2),
            pl.BlockSpec((1, n_mem, D), lambda b, t: (b, 0, 0)),
            const((1, D)), const((D, 2 * xattn_width)), const((CONV_K, conv_width)),
            const((1, attn_width)), const((1, conv_width)), const((1, xattn_width)),
            const((D, D)), const((1, D)),
            const((1, D)), const((D, d_ff)), const((d_ff, D)), const((1, D)),
        ],
        out_specs=pl.BlockSpec((1, tm, D), lambda b, t: (b, t, 0)),
        out_shape=jax.ShapeDtypeStruct((B, S, D), x.dtype),
        scratch_shapes=[pltpu.VMEM((n_mem, xattn_width), BF16),
                        pltpu.VMEM((n_mem, xattn_width), BF16)],
        compiler_params=pltpu.CompilerParams(
            dimension_semantics=("parallel", "arbitrary"),
            vmem_limit_bytes=VMEM_LIMIT_BYTES),
        name="mix_mlp",
    )(x, y_attn, proj, proj, proj, proj, proj, proj, mem, g_mem, w_mem, conv_w,
      g_attn, g_conv, g_xattn, w_out, g_post_mix, g_pre_mlp, w_up, w_down, g_post_mlp)


def kernel(x, mem, positions, g_pre_mix, g_mem, w_in, w_mem_kv, conv_w, g_attn_out,
           g_conv_out, g_xattn_out, w_out, g_post_mix, g_pre_mlp, w_up, w_down,
           g_post_mlp):
    depth = w_in.shape[0]
    attn_width = g_attn_out.shape[1]
    conv_width = g_conv_out.shape[1]
    xattn_width = g_xattn_out.shape[1]
    pos3 = positions[:, None, :]
    row = lambda g: g[None, :]
    for l in range(depth):
        proj, (w_mem_b, w_out_b, w_up_b, w_down_b) = _in_proj(
            x, pos3, row(g_pre_mix[l]), w_in[l],
            (w_mem_kv[l], w_out[l], w_up[l], w_down[l]),
            attn_width=attn_width, xattn_width=xattn_width)
        y_attn = _dil_attn(proj, attn_width=attn_width)
        x = _mix_mlp(x, y_attn, proj, mem, row(g_mem[l]), w_mem_b, conv_w[l],
                     row(g_attn_out[l]), row(g_conv_out[l]), row(g_xattn_out[l]),
                     w_out_b, row(g_post_mix[l]), row(g_pre_mlp[l]), w_up_b, w_down_b,
                     row(g_post_mlp[l]),
                     attn_width=attn_width, conv_width=conv_width,
                     xattn_width=xattn_width)
    return x
```

```python
import functools

import jax
import jax.numpy as jnp
from jax import lax
from jax.experimental import pallas as pl
from jax.experimental.pallas import tpu as pltpu

F32 = jnp.float32
BF16 = jnp.bfloat16

HEAD_DIM = 64
N_MEM_HEADS = 4
DILATED_PATTERNS = ((128, 1), (512, 4), (2048, 16))
CONV_K = 3
ROPE_THETA = 10000.0
EPS = 1e-6
NEG_INF = -1e30
LOG2_E = 1.4426950408889634

LANES = 128
BF16_SUBLANES = 16
ATTN_BLK = 128
VMEM_LIMIT_BYTES = 56 * 1024 * 1024


def _rms(x, g):
    return x * lax.rsqrt(jnp.mean(x * x, axis=-1, keepdims=True) + EPS) * g


def _in_proj_kernel(x_ref, pos_ref, g_ref, w_f32_ref, *rest, attn_width, xattn_width, chunk):
    n_cast = (len(rest) - 2) // 2
    cast_in, o_ref, cast_out = rest[:n_cast], rest[n_cast], rest[n_cast + 1:-1]
    w_ref = rest[-1]

    @pl.when((pl.program_id(0) == 0) & (pl.program_id(1) == 0))
    def _():
        w_ref[...] = w_f32_ref[...].astype(BF16)

    h = _rms(x_ref[0], g_ref[...]).astype(BF16)

    half = HEAD_DIM // 2
    freq = lax.broadcasted_iota(jnp.int32, (half, 1), 0).astype(F32)
    inv_freq = jnp.float32(ROPE_THETA) ** (-(freq * 2.0 / HEAD_DIM))
    ang = inv_freq * pos_ref[0].astype(F32)
    cos_t, sin_t = jnp.cos(ang), jnp.sin(ang)
    reps = LANES // HEAD_DIM
    cos_k = jnp.concatenate([cos_t, cos_t] * reps, axis=0).T
    sin_k = jnp.concatenate([-sin_t, sin_t] * reps, axis=0).T
    q_scale = HEAD_DIM ** -0.5 * LOG2_E
    cos_q, sin_q = cos_k * q_scale, sin_k * q_scale
    lane = lax.broadcasted_iota(jnp.int32, (1, LANES), 1)
    first_half = (lane % HEAD_DIM) < half

    n_out = o_ref.shape[-1]
    for c0 in range(0, n_out, chunk):
        p = jnp.dot(h, w_ref[:, c0:c0 + chunk], preferred_element_type=F32)
        if c0 < 2 * attn_width:
            cos, sin = (cos_q, sin_q) if c0 < attn_width else (cos_k, sin_k)
            for g0 in range(0, chunk, LANES):
                t = p[:, g0:g0 + LANES]
                rot = jnp.where(first_half,
                                pltpu.roll(t, LANES - half, 1),
                                pltpu.roll(t, half, 1))
                r = t * cos + rot * sin
                o_ref[0, :, c0 + g0:c0 + g0 + LANES] = r.astype(BF16)
        elif c0 + chunk == n_out:
            lo = chunk - xattn_width
            o_ref[0, :, c0:c0 + lo] = p[:, :lo].astype(BF16)
            o_ref[0, :, c0 + lo:c0 + chunk] = (p[:, lo:] * q_scale).astype(BF16)
        else:
            o_ref[0, :, c0:c0 + chunk] = p.astype(BF16)
        if c0 == 2 * attn_width:
            for src, dst in zip(cast_in, cast_out):
                dst[...] = src[...].astype(BF16)


def _in_proj(x, pos3, g, w_in, later_weights, *, attn_width, xattn_width, tm=1024, chunk=512):
    B, S, D = x.shape
    n_out = w_in.shape[1]
    n_steps = B * (S // tm)
    slices = []
    for w in later_weights:
        rows = w.shape[0] // n_steps
        assert rows * n_steps == w.shape[0] and rows % BF16_SUBLANES == 0
        slices.append(pl.BlockSpec((rows, w.shape[1]), lambda b, t: (b * (S // tm) + t, 0)))
    out = pl.pallas_call(
        functools.partial(_in_proj_kernel, attn_width=attn_width,
                          xattn_width=xattn_width, chunk=chunk),
        grid=(B, S // tm),
        in_specs=[
            pl.BlockSpec((1, tm, D), lambda b, t: (b, t, 0)),
            pl.BlockSpec((1, 1, tm), lambda b, t: (b, 0, t)),
            pl.BlockSpec((1, D), lambda b, t: (0, 0)),
            pl.BlockSpec((D, n_out), lambda b, t: (0, 0), pipeline_mode=pl.Buffered(1)),
        ] + slices,
        out_specs=[pl.BlockSpec((1, tm, n_out), lambda b, t: (b, t, 0))] + slices,
        out_shape=[jax.ShapeDtypeStruct((B, S, n_out), BF16)]
        + [jax.ShapeDtypeStruct(w.shape, BF16) for w in later_weights],
        scratch_shapes=[pltpu.VMEM(w_in.shape, BF16)],
        compiler_params=pltpu.CompilerParams(
            dimension_semantics=("arbitrary", "arbitrary"),
            vmem_limit_bytes=VMEM_LIMIT_BYTES),
        name="in_proj",
    )(x, pos3, g, w_in, *later_weights)
    return out[0], out[1:]


def _band_block(qb, kb, vb1, bias, lane_lo):
    ms, ls, accs = [], [], []
    for head_lo in (True, False):
        sel = lane_lo if head_lo else jnp.logical_not(lane_lo)
        qh = jnp.where(sel, qb, jnp.zeros_like(qb))
        s = lax.dot_general(qh, kb, (((1,), (1,)), ((), ())),
                            preferred_element_type=F32) + bias
        m = jnp.max(s, axis=-1, keepdims=True)
        p = jnp.exp2(s - m).astype(BF16)
        r = jnp.dot(p, vb1, preferred_element_type=F32)
        accs.append(r[:, :LANES])
        ls.append(r[:, LANES:])
        ms.append(jnp.broadcast_to(m, (m.shape[0], LANES)))
    return (jnp.where(lane_lo, ms[0], ms[1]), jnp.where(lane_lo, ls[0], ls[1]),
            jnp.where(lane_lo, accs[0], accs[1]))


COPY_ROWS = 64
DEINTERLEAVE = 4


def _for_each_split(seq, region, body):
    run = region // DEINTERLEAVE
    span = DEINTERLEAVE * COPY_ROWS
    steps_per_region = region // span

    def step(t, carry):
        strided0 = pl.multiple_of(t * span, span)
        dense0 = pl.multiple_of((t // steps_per_region) * region
                                + (t % steps_per_region) * COPY_ROWS, COPY_ROWS)
        for j in range(DEINTERLEAVE):
            body(pl.ds(strided0 + j, COPY_ROWS, stride=DEINTERLEAVE),
                 pl.ds(dense0 + j * run, COPY_ROWS))
        return carry

    lax.fori_loop(0, seq // span, step, 0)


def _merge_softmax(a, b):
    (m_a, l_a, a_a), (m_b, l_b, a_b) = a, b
    m = jnp.maximum(m_a, m_b)
    w_a = jnp.exp2(m_a - m)
    w_b = jnp.exp2(m_b - m)
    return m, l_a * w_a + l_b * w_b, a_a * w_a + a_b * w_b


def _dil_attn_kernel(trips_ref, q_ref, k_ref, v_ref, o_ref,
                     qf, kf, vf, q4f, k4f, v4f, qp, kp, vp1,
                     m_p4, l_p4, a_p4, m_tmp, l_tmp, a_tmp, bias_ref,
                     *, seq, patterns, unroll):
    blk = ATTN_BLK
    n_blocks = seq // blk
    lane_lo = lax.broadcasted_iota(jnp.int32, (1, LANES), 1) < HEAD_DIM
    assert [d for _, d in patterns] == [1, DEINTERLEAVE, DEINTERLEAVE ** 2]
    assert all(w // d == blk for w, d in patterns)
    run4 = seq // DEINTERLEAVE
    run16 = run4 // DEINTERLEAVE
    blocks_per_run4 = run4 // blk
    blocks_per_run16 = run16 // blk
    assert n_blocks % unroll == 0 and unroll % blocks_per_run4 == 0
    n_trips = trips_ref[0]

    @pl.when((pl.program_id(0) == 0) & (pl.program_id(1) == 0))
    def _():
        qi = lax.broadcasted_iota(jnp.int32, (blk, 2 * blk), 0)
        kj = lax.broadcasted_iota(jnp.int32, (blk, 2 * blk), 1)
        bias_ref[0] = jnp.where(kj <= qi, 0.0, NEG_INF).astype(F32)
        bias_ref[1] = jnp.where((kj >= qi) & (kj <= qi + blk), 0.0, NEG_INF).astype(F32)
        vp1[:, LANES:] = jnp.ones((seq, LANES), BF16)

    pat4 = (m_p4, l_p4, a_p4)
    tmp = (m_tmp, l_tmp, a_tmp)

    def attend(sub_len, q_src, k_src, finish):
        nb = sub_len // blk
        assert nb >= 2

        def trip_body(trip, carry):
            for u in range(unroll):
                g = trip * unroll + u
                n = g % nb
                q0 = pl.multiple_of(g * blk, blk)
                k0 = pl.multiple_of(jnp.where(n > 0, q0 - blk, q0), blk)
                triple = _band_block(q_src[pl.ds(q0, blk), :],
                                     k_src[pl.ds(k0, 2 * blk), :],
                                     vp1[pl.ds(k0, 2 * blk), :],
                                     bias_ref[jnp.minimum(n, 1)], lane_lo)
                finish(trip, u, q0, triple)
            return carry

        lax.fori_loop(0, n_trips, trip_body, 0)

    qf[...] = q_ref[0].astype(F32)
    kf[...] = k_ref[0].astype(F32)
    vf[...] = v_ref[0].astype(F32)

    def gather4(strided, dense):
        for src, dst_f, dst_b in ((qf, q4f, qp), (kf, k4f, kp)):
            rows = src[strided, :]
            dst_f[dense, :] = rows
            dst_b[dense, :] = rows.astype(BF16)
        rows = vf[strided, :]
        v4f[dense, :] = rows
        vp1[dense, :LANES] = rows.astype(BF16)

    def store_pat4(trip, u, q0, triple):
        for ref, val in zip(pat4, triple):
            ref[pl.ds(q0, blk), :] = val

    _for_each_split(seq, seq, gather4)
    attend(run4, qp, kp, store_pat4)

    def gather16(strided, dense):
        qp[dense, :] = q4f[strided, :].astype(BF16)
        kp[dense, :] = k4f[strided, :].astype(BF16)
        vp1[dense, :LANES] = v4f[strided, :].astype(BF16)

    def fold_into_pat4(trip, u, q0, triple):
        r_static, idx = divmod(u, blocks_per_run4)
        j, a0 = idx // blocks_per_run16, (idx % blocks_per_run16) * blk
        r_dyn = pl.multiple_of(trip * (unroll // blocks_per_run4) * run4, run4)
        rows = pl.ds(r_dyn + r_static * run4 + j + DEINTERLEAVE * a0, blk,
                     stride=DEINTERLEAVE)
        merged = _merge_softmax(tuple(ref[rows, :] for ref in pat4), triple)
        for ref, val in zip(pat4, merged):
            ref[rows, :] = val

    _for_each_split(seq, run4, gather16)
    attend(run16, qp, kp, fold_into_pat4)

    def finish_natural(trip, u, q0, triple):
        sub = blk // DEINTERLEAVE
        i_dyn = pl.multiple_of(trip * (unroll * sub), unroll * sub)
        for j in range(DEINTERLEAVE):
            src = pl.ds(i_dyn + j * run4 + u * sub, sub)
            for t_ref, p_ref in zip(tmp, pat4):
                t_ref[u, pl.ds(j, sub, stride=DEINTERLEAVE), :] = p_ref[src, :]
        _, l, acc = _merge_softmax(tuple(t_ref[u] for t_ref in tmp), triple)
        o_ref[0, pl.ds(q0, blk), :] = (acc / l).astype(o_ref.dtype)

    vp1[:, :LANES] = v_ref[0]
    attend(seq, q_ref.at[0], k_ref.at[0], finish_natural)


def _dil_attn(proj, *, attn_width, unroll=32):
    B, S, _ = proj.shape
    n_pairs = attn_width // LANES
    trips = jnp.full((1,), S // ATTN_BLK // unroll, jnp.int32)
    kern = functools.partial(_dil_attn_kernel, seq=S, patterns=DILATED_PATTERNS,
                             unroll=unroll)
    col = lambda off: (lambda b, hp: (b, 0, off + hp))
    f32_buf = pltpu.VMEM((S, LANES), F32)
    bf16_buf = pltpu.VMEM((S, LANES), BF16)
    tmp_buf = pltpu.VMEM((unroll, ATTN_BLK, LANES), F32)
    return pl.pallas_call(
        kern,
        grid=(B, n_pairs),
        in_specs=[pl.BlockSpec(memory_space=pltpu.SMEM),
                  pl.BlockSpec((1, S, LANES), col(0)),
                  pl.BlockSpec((1, S, LANES), col(n_pairs)),
                  pl.BlockSpec((1, S, LANES), col(2 * n_pairs))],
        out_specs=pl.BlockSpec((1, S, LANES), lambda b, hp: (b, 0, hp)),
        out_shape=jax.ShapeDtypeStruct((B, S, attn_width), BF16),
        scratch_shapes=[f32_buf, f32_buf, f32_buf,
                        f32_buf, f32_buf, f32_buf,
                        bf16_buf, bf16_buf,
                        pltpu.VMEM((S, 2 * LANES), BF16),
                        f32_buf, f32_buf, f32_buf,
                        tmp_buf, tmp_buf, tmp_buf,
                        pltpu.VMEM((2, ATTN_BLK, 2 * ATTN_BLK), F32)],
        compiler_params=pltpu.CompilerParams(
            dimension_semantics=("arbitrary", "arbitrary"),
            vmem_limit_bytes=VMEM_LIMIT_BYTES),
        name="dil_attn",
    )(trips, proj, proj, proj)


def _mix_mlp_kernel(x_ref, ya_ref, bg_ref, cg_ref, u_ref, qx_ref, hc_ref, hu_ref,
                    mem_ref, g_mem_ref, w_mem_ref, conv_w_ref,
                    g_attn_ref, g_conv_ref, g_xattn_ref, w_out_ref, g_post_mix_ref,
                    g_pre_mlp_ref, w_up_ref, w_down_ref, g_post_mlp_ref,
                    o_ref, km_ref, vm_ref, *, ff_chunk):
    t = pl.program_id(1)
    xw = km_ref.shape[1]

    @pl.when(t == 0)
    def _():
        hm = _rms(mem_ref[0], g_mem_ref[...]).astype(BF16)
        kv = jnp.dot(hm, w_mem_ref[...], preferred_element_type=F32)
        km_ref[...] = kv[:, :xw].astype(BF16)
        vm_ref[...] = kv[:, xw:].astype(BF16)

    tm = x_ref.shape[1]

    z = cg_ref[0].astype(F32) * u_ref[0].astype(F32)
    hz = hc_ref[0].astype(F32) * hu_ref[0].astype(F32)
    hz = jnp.where(t > 0, hz, jnp.zeros_like(hz))
    z_ext = jnp.concatenate([hz, z], axis=0)
    cw = conv_w_ref[...]
    y_conv = z * cw[CONV_K - 1:CONV_K, :]
    for back in range(1, CONV_K):
        lo = BF16_SUBLANES - back
        y_conv = y_conv + z_ext[lo:lo + tm, :] * cw[CONV_K - 1 - back:CONV_K - back, :]
    y_conv = bg_ref[0].astype(F32) * y_conv

    qx = qx_ref[0]
    km = km_ref[...]
    vm = vm_ref[...]
    lane = lax.broadcasted_iota(jnp.int32, (1, xw), 1)
    xhd = xw // N_MEM_HEADS
    assert xhd == HEAD_DIM
    sels = [(lane >= hd * xhd) & (lane < (hd + 1) * xhd) for hd in range(N_MEM_HEADS)]

    def scores(hd):
        qh = jnp.where(sels[hd], qx, jnp.zeros_like(qx))
        return lax.dot_general(qh, km, (((1,), (1,)), ((), ())),
                               preferred_element_type=F32)

    def softmax(sc):
        p = jnp.exp2(sc - jnp.max(sc, axis=-1, keepdims=True))
        return (p / jnp.sum(p, axis=-1, keepdims=True)).astype(BF16)

    sc, p, y_x = {}, {}, jnp.zeros((tm, xw), F32)
    for hd in range(N_MEM_HEADS + 2):
        if hd < N_MEM_HEADS:
            sc[hd] = scores(hd)
        if 0 <= hd - 1 < N_MEM_HEADS:
            p[hd - 1] = softmax(sc.pop(hd - 1))
        if 0 <= hd - 2 < N_MEM_HEADS:
            o = jnp.dot(p.pop(hd - 2), vm, preferred_element_type=F32)
            y_x = jnp.where(sels[hd - 2], o, y_x)

    halves = ((0, tm // 2), (tm // 2, tm))

    x1_h, h2_h = [], []
    for r0, r1 in halves:
        y = jnp.concatenate([
            _rms(ya_ref[0, r0:r1, :].astype(F32), g_attn_ref[...]).astype(BF16),
            _rms(y_conv[r0:r1], g_conv_ref[...]).astype(BF16),
            _rms(y_x[r0:r1], g_xattn_ref[...]).astype(BF16)], axis=-1)
        y = jnp.dot(y, w_out_ref[...], preferred_element_type=F32)
        x1 = x_ref[0, r0:r1, :] + _rms(y, g_post_mix_ref[...])
        x1_h.append(x1)
        h2_h.append(_rms(x1, g_pre_mlp_ref[...]).astype(BF16))

    def act_fn(up):
        return jnp.square(jnp.maximum(up, 0.0)).astype(BF16)

    chunks = list(range(0, w_up_ref.shape[1], ff_chunk))
    h2 = jnp.concatenate(h2_h, axis=0)
    acc = None
    for f0 in chunks:
        w_up_c = w_up_ref[:, f0:f0 + ff_chunk]
        if f0 == chunks[0]:
            act = jnp.concatenate(
                [act_fn(jnp.dot(h, w_up_c, preferred_element_type=F32)) for h in h2_h],
                axis=0)
        else:
            act = act_fn(jnp.dot(h2, w_up_c, preferred_element_type=F32))
        w_down_c = w_down_ref[f0:f0 + ff_chunk, :]
        if f0 != chunks[-1]:
            part = jnp.dot(act, w_down_c, preferred_element_type=F32)
            acc = part if acc is None else acc + part
        else:
            for (r0, r1), x1 in zip(halves, x1_h):
                f = acc[r0:r1] + jnp.dot(act[r0:r1], w_down_c, preferred_element_type=F32)
                o_ref[0, r0:r1, :] = x1 + _rms(f, g_post_mlp_ref[...])


def _mix_mlp(x, y_attn, proj, mem, g_mem, w_mem, conv_w, g_attn, g_conv, g_xattn,
             w_out, g_post_mix, g_pre_mlp, w_up, w_down, g_post_mlp,
             *, attn_width, conv_width, xattn_width, tm=512, ff_chunk=1024):
    B, S, D = x.shape
    n_mem = mem.shape[1]
    d_ff = w_up.shape[1]
    assert conv_width == xattn_width and (3 * attn_width) % conv_width == 0
    cb0 = 3 * attn_width // conv_width
    halo = BF16_SUBLANES
    const = lambda shape: pl.BlockSpec(shape, lambda b, t: (0,) * len(shape),
                                       pipeline_mode=pl.Buffered(1))
    pcol = lambda cb: pl.BlockSpec((1, tm, conv_width), lambda b, t: (b, t, cb))
    phalo = lambda cb: pl.BlockSpec(
        (1, halo, conv_width),
        lambda b, t: (b, jnp.maximum(t * (tm // halo) - 1, 0), cb))
    return pl.pallas_call(
        functools.partial(_mix_mlp_kernel, ff_chunk=ff_chunk),
        grid=(B, S // tm),
        in_specs=[
            pl.BlockSpec((1, tm, D), lambda b, t: (b, t, 0)),
            pl.BlockSpec((1, tm, attn_width), lambda b, t: (b, t, 0)),
            pcol(cb0), pcol(cb0 + 1), pcol(cb0 + 2), pcol(cb0 + 3),
            phalo(cb0 + 1), phalo(cb0 + 2),
            pl.BlockSpec((1, n_mem, D), lambda b, t: (b, 0, 0)),
            const((1, D)), const((D, 2 * xattn_width)), const((CONV_K, conv_width)),
            const((1, attn_width)), const((1, conv_width)), const((1, xattn_width)),
            const((D, D)), const((1, D)),
            const((1, D)), const((D, d_ff)), const((d_ff, D)), const((1, D)),
        ],
        out_specs=pl.BlockSpec((1, tm, D), lambda b, t: (b, t, 0)),
        out_shape=jax.ShapeDtypeStruct((B, S, D), x.dtype),
        scratch_shapes=[pltpu.VMEM((n_mem, xattn_width), BF16),
                        pltpu.VMEM((n_mem, xattn_width), BF16)],
        compiler_params=pltpu.CompilerParams(
            dimension_semantics=("parallel", "arbitrary"),
            vmem_limit_bytes=VMEM_LIMIT_BYTES),
        name="mix_mlp",
    )(x, y_attn, proj, proj, proj, proj, proj, proj, mem, g_mem, w_mem, conv_w,
      g_attn, g_conv, g_xattn, w_out, g_post_mix, g_pre_mlp, w_up, w_down, g_post_mlp)


def kernel(x, mem, positions, g_pre_mix, g_mem, w_in, w_mem_kv, conv_w, g_attn_out,
           g_conv_out, g_xattn_out, w_out, g_post_mix, g_pre_mlp, w_up, w_down,
           g_post_mlp):
    depth = w_in.shape[0]
    attn_width = g_attn_out.shape[1]
    conv_width = g_conv_out.shape[1]
    xattn_width = g_xattn_out.shape[1]
    pos3 = positions[:, None, :]
    row = lambda g: g[None, :]
    for l in range(depth):
        proj, (w_mem_b, w_out_b, w_up_b, w_down_b) = _in_proj(
            x, pos3, row(g_pre_mix[l]), w_in[l],
            (w_mem_kv[l], w_out[l], w_up[l], w_down[l]),
            attn_width=attn_width, xattn_width=xattn_width)
        y_attn = _dil_attn(proj, attn_width=attn_width)
        x = _mix_mlp(x, y_attn, proj, mem, row(g_mem[l]), w_mem_b, conv_w[l],
                     row(g_attn_out[l]), row(g_conv_out[l]), row(g_xattn_out[l]),
                     w_out_b, row(g_post_mix[l]), row(g_pre_mlp[l]), w_up_b, w_down_b,
                     row(g_post_mlp[l]),
                     attn_width=attn_width, conv_width=conv_width,
                     xattn_width=xattn_width)
    return x
```

```python
import functools

import jax
import jax.numpy as jnp
from jax import lax
from jax.experimental import pallas as pl
from jax.experimental.pallas import tpu as pltpu

F32 = jnp.float32
BF16 = jnp.bfloat16

HEAD_DIM = 64
N_MEM_HEADS = 4
DILATED_PATTERNS = ((128, 1), (512, 4), (2048, 16))
CONV_K = 3
ROPE_THETA = 10000.0
EPS = 1e-6
NEG_INF = -1e30
LOG2_E = 1.4426950408889634

LANES = 128
BF16_SUBLANES = 16
ATTN_BLK = 128
VMEM_LIMIT_BYTES = 56 * 1024 * 1024


def _rms(x, g):
    return x * lax.rsqrt(jnp.mean(x * x, axis=-1, keepdims=True) + EPS) * g


CONV_HALO = 8


def _in_proj_kernel(x_ref, pos_ref, g_ref, w_f32_ref, conv_w_ref, g_conv_ref, *rest,
                    attn_width, conv_width, xattn_width, chunk):
    n_cast = (len(rest) - 4) // 2
    cast_in, (o_ref, oc_ref) = rest[:n_cast], rest[n_cast:n_cast + 2]
    cast_out, w_ref, ztail_ref = rest[n_cast + 2:-2], rest[-2], rest[-1]
    t = pl.program_id(1)
    tm = x_ref.shape[1]

    @pl.when((pl.program_id(0) == 0) & (pl.program_id(1) == 0))
    def _():
        w_ref[...] = w_f32_ref[...].astype(BF16)

    h = _rms(x_ref[0], g_ref[...]).astype(BF16)

    half = HEAD_DIM // 2
    freq = lax.broadcasted_iota(jnp.int32, (half, 1), 0).astype(F32)
    inv_freq = jnp.float32(ROPE_THETA) ** (-(freq * 2.0 / HEAD_DIM))
    ang = inv_freq * pos_ref[0].astype(F32)
    cos_t, sin_t = jnp.cos(ang), jnp.sin(ang)
    reps = LANES // HEAD_DIM
    cos_k = jnp.concatenate([cos_t, cos_t] * reps, axis=0).T
    sin_k = jnp.concatenate([-sin_t, sin_t] * reps, axis=0).T
    q_scale = HEAD_DIM ** -0.5 * LOG2_E
    cos_q, sin_q = cos_k * q_scale, sin_k * q_scale
    lane = lax.broadcasted_iota(jnp.int32, (1, LANES), 1)
    first_half = (lane % HEAD_DIM) < half

    qkv = 3 * attn_width
    assert chunk == 2 * conv_width == conv_width + xattn_width and qkv % chunk == 0

    p_bc = jnp.dot(h, w_ref[:, qkv:qkv + chunk], preferred_element_type=F32)
    p_uq = jnp.dot(h, w_ref[:, qkv + chunk:qkv + 2 * chunk], preferred_element_type=F32)
    o_ref[0, :, qkv:] = (p_uq[:, conv_width:] * q_scale).astype(BF16)
    z = p_bc[:, conv_width:] * p_uq[:, :conv_width]
    tail = jnp.where(t > 0, ztail_ref[...], jnp.zeros(ztail_ref.shape, F32))
    z_ext = jnp.concatenate([tail, z], axis=0)
    cw = conv_w_ref[...]
    y_conv = z * cw[CONV_K - 1:CONV_K, :]
    for back in range(1, CONV_K):
        lo = CONV_HALO - back
        y_conv = y_conv + z_ext[lo:lo + tm, :] * cw[CONV_K - 1 - back:CONV_K - back, :]
    oc_ref[0] = _rms(p_bc[:, :conv_width] * y_conv, g_conv_ref[...]).astype(BF16)
    ztail_ref[...] = z[tm - CONV_HALO:, :]

    for c0 in range(0, qkv, chunk):
        p = jnp.dot(h, w_ref[:, c0:c0 + chunk], preferred_element_type=F32)
        if c0 < 2 * attn_width:
            cos, sin = (cos_q, sin_q) if c0 < attn_width else (cos_k, sin_k)
            for g0 in range(0, chunk, LANES):
                v = p[:, g0:g0 + LANES]
                rot = jnp.where(first_half,
                                pltpu.roll(v, LANES - half, 1),
                                pltpu.roll(v, half, 1))
                r = v * cos + rot * sin
                o_ref[0, :, c0 + g0:c0 + g0 + LANES] = r.astype(BF16)
        else:
            o_ref[0, :, c0:c0 + chunk] = p.astype(BF16)
        if c0 == 2 * attn_width:
            for src, dst in zip(cast_in, cast_out):
                dst[...] = src[...].astype(BF16)


def _in_proj(x, pos3, g, w_in, conv_w, g_conv, later_weights, *, attn_width, conv_width,
             xattn_width, tm=1024, chunk=512):
    B, S, D = x.shape
    n_in = w_in.shape[1]
    n_out = 3 * attn_width + xattn_width
    assert n_in == n_out + 3 * conv_width
    n_steps = B * (S // tm)
    slices = []
    for w in later_weights:
        rows = w.shape[0] // n_steps
        assert rows * n_steps == w.shape[0] and rows % BF16_SUBLANES == 0
        slices.append(pl.BlockSpec((rows, w.shape[1]), lambda b, t: (b * (S // tm) + t, 0)))
    out = pl.pallas_call(
        functools.partial(_in_proj_kernel, attn_width=attn_width, conv_width=conv_width,
                          xattn_width=xattn_width, chunk=chunk),
        grid=(B, S // tm),
        in_specs=[
            pl.BlockSpec((1, tm, D), lambda b, t: (b, t, 0)),
            pl.BlockSpec((1, 1, tm), lambda b, t: (b, 0, t)),
            pl.BlockSpec((1, D), lambda b, t: (0, 0)),
            pl.BlockSpec((D, n_in), lambda b, t: (0, 0), pipeline_mode=pl.Buffered(1)),
            pl.BlockSpec(conv_w.shape, lambda b, t: (0, 0)),
            pl.BlockSpec((1, conv_width), lambda b, t: (0, 0)),
        ] + slices,
        out_specs=[pl.BlockSpec((1, tm, n_out), lambda b, t: (b, t, 0)),
                   pl.BlockSpec((1, tm, conv_width), lambda b, t: (b, t, 0))] + slices,
        out_shape=[jax.ShapeDtypeStruct((B, S, n_out), BF16),
                   jax.ShapeDtypeStruct((B, S, conv_width), BF16)]
        + [jax.ShapeDtypeStruct(w.shape, BF16) for w in later_weights],
        scratch_shapes=[pltpu.VMEM(w_in.shape, BF16),
                        pltpu.VMEM((CONV_HALO, conv_width), F32)],
        compiler_params=pltpu.CompilerParams(
            dimension_semantics=("arbitrary", "arbitrary"),
            vmem_limit_bytes=VMEM_LIMIT_BYTES),
        name="in_proj",
    )(x, pos3, g, w_in, conv_w, g_conv, *later_weights)
    return out[0], out[1], out[2:]


def _band_block(qb, kb, vb1, bias, lane_lo):
    ms, ls, accs = [], [], []
    for head_lo in (True, False):
        sel = lane_lo if head_lo else jnp.logical_not(lane_lo)
        qh = jnp.where(sel, qb, jnp.zeros_like(qb))
        s = lax.dot_general(qh, kb, (((1,), (1,)), ((), ())),
                            preferred_element_type=F32) + bias
        m = jnp.max(s, axis=-1, keepdims=True)
        p = jnp.exp2(s - m).astype(BF16)
        r = jnp.dot(p, vb1, preferred_element_type=F32)
        accs.append(r[:, :LANES])
        ls.append(r[:, LANES:])
        ms.append(jnp.broadcast_to(m, (m.shape[0], LANES)))
    return (jnp.where(lane_lo, ms[0], ms[1]), jnp.where(lane_lo, ls[0], ls[1]),
            jnp.where(lane_lo, accs[0], accs[1]))


COPY_ROWS = 64
DEINTERLEAVE = 4


def _for_each_split(seq, region, body):
    run = region // DEINTERLEAVE
    span = DEINTERLEAVE * COPY_ROWS
    steps_per_region = region // span

    def step(t, carry):
        strided0 = pl.multiple_of(t * span, span)
        dense0 = pl.multiple_of((t // steps_per_region) * region
                                + (t % steps_per_region) * COPY_ROWS, COPY_ROWS)
        for j in range(DEINTERLEAVE):
            body(pl.ds(strided0 + j, COPY_ROWS, stride=DEINTERLEAVE),
                 pl.ds(dense0 + j * run, COPY_ROWS))
        return carry

    lax.fori_loop(0, seq // span, step, 0)


def _merge_softmax(a, b):
    (m_a, l_a, a_a), (m_b, l_b, a_b) = a, b
    m = jnp.maximum(m_a, m_b)
    w_a = jnp.exp2(m_a - m)
    w_b = jnp.exp2(m_b - m)
    return m, l_a * w_a + l_b * w_b, a_a * w_a + a_b * w_b


def _dil_attn_kernel(trips_ref, q_ref, k_ref, v_ref, o_ref,
                     qf, kf, vf, q4f, k4f, v4f, qp, kp, vp1,
                     m_p4, l_p4, a_p4, m_tmp, l_tmp, a_tmp, bias_ref,
                     *, seq, patterns, unroll):
    blk = ATTN_BLK
    n_blocks = seq // blk
    lane_lo = lax.broadcasted_iota(jnp.int32, (1, LANES), 1) < HEAD_DIM
    assert [d for _, d in patterns] == [1, DEINTERLEAVE, DEINTERLEAVE ** 2]
    assert all(w // d == blk for w, d in patterns)
    run4 = seq // DEINTERLEAVE
    run16 = run4 // DEINTERLEAVE
    blocks_per_run4 = run4 // blk
    blocks_per_run16 = run16 // blk
    assert n_blocks % unroll == 0 and unroll % blocks_per_run4 == 0
    n_trips = trips_ref[0]

    @pl.when((pl.program_id(0) == 0) & (pl.program_id(1) == 0))
    def _():
        qi = lax.broadcasted_iota(jnp.int32, (blk, 2 * blk), 0)
        kj = lax.broadcasted_iota(jnp.int32, (blk, 2 * blk), 1)
        bias_ref[0] = jnp.where(kj <= qi, 0.0, NEG_INF).astype(F32)
        bias_ref[1] = jnp.where((kj >= qi) & (kj <= qi + blk), 0.0, NEG_INF).astype(F32)
        vp1[:, LANES:] = jnp.ones((seq, LANES), BF16)

    pat4 = (m_p4, l_p4, a_p4)
    tmp = (m_tmp, l_tmp, a_tmp)

    def attend(sub_len, q_src, k_src, finish):
        nb = sub_len // blk
        assert nb >= 2

        def trip_body(trip, carry):
            for u in range(unroll):
                g = trip * unroll + u
                n = g % nb
                q0 = pl.multiple_of(g * blk, blk)
                k0 = pl.multiple_of(jnp.where(n > 0, q0 - blk, q0), blk)
                triple = _band_block(q_src[pl.ds(q0, blk), :],
                                     k_src[pl.ds(k0, 2 * blk), :],
                                     vp1[pl.ds(k0, 2 * blk), :],
                                     bias_ref[jnp.minimum(n, 1)], lane_lo)
                finish(trip, u, q0, triple)
            return carry

        lax.fori_loop(0, n_trips, trip_body, 0)

    qf[...] = q_ref[0].astype(F32)
    kf[...] = k_ref[0].astype(F32)
    vf[...] = v_ref[0].astype(F32)

    def gather4(strided, dense):
        for src, dst_f, dst_b in ((qf, q4f, qp), (kf, k4f, kp)):
            rows = src[strided, :]
            dst_f[dense, :] = rows
            dst_b[dense, :] = rows.astype(BF16)
        rows = vf[strided, :]
        v4f[dense, :] = rows
        vp1[dense, :LANES] = rows.astype(BF16)

    def store_pat4(trip, u, q0, triple):
        for ref, val in zip(pat4, triple):
            ref[pl.ds(q0, blk), :] = val

    _for_each_split(seq, seq, gather4)
    attend(run4, qp, kp, store_pat4)

    def gather16(strided, dense):
        qp[dense, :] = q4f[strided, :].astype(BF16)
        kp[dense, :] = k4f[strided, :].astype(BF16)
        vp1[dense, :LANES] = v4f[strided, :].astype(BF16)

    def fold_into_pat4(trip, u, q0, triple):
        r_static, idx = divmod(u, blocks_per_run4)
        j, a0 = idx // blocks_per_run16, (idx % blocks_per_run16) * blk
        r_dyn = pl.multiple_of(trip * (unroll // blocks_per_run4) * run4, run4)
        rows = pl.ds(r_dyn + r_static * run4 + j + DEINTERLEAVE * a0, blk,
                     stride=DEINTERLEAVE)
        merged = _merge_softmax(tuple(ref[rows, :] for ref in pat4), triple)
        for ref, val in zip(pat4, merged):
            ref[rows, :] = val

    _for_each_split(seq, run4, gather16)
    attend(run16, qp, kp, fold_into_pat4)

    def finish_natural(trip, u, q0, triple):
        sub = blk // DEINTERLEAVE
        i_dyn = pl.multiple_of(trip * (unroll * sub), unroll * sub)
        for j in range(DEINTERLEAVE):
            src = pl.ds(i_dyn + j * run4 + u * sub, sub)
            for t_ref, p_ref in zip(tmp, pat4):
                t_ref[u, pl.ds(j, sub, stride=DEINTERLEAVE), :] = p_ref[src, :]
        _, l, acc = _merge_softmax(tuple(t_ref[u] for t_ref in tmp), triple)
        o_ref[0, pl.ds(q0, blk), :] = (acc / l).astype(o_ref.dtype)

    vp1[:, :LANES] = v_ref[0]
    attend(seq, q_ref.at[0], k_ref.at[0], finish_natural)


def _dil_attn(proj, *, attn_width, unroll=32):
    B, S, _ = proj.shape
    n_pairs = attn_width // LANES
    trips = jnp.full((1,), S // ATTN_BLK // unroll, jnp.int32)
    kern = functools.partial(_dil_attn_kernel, seq=S, patterns=DILATED_PATTERNS,
                             unroll=unroll)
    col = lambda off: (lambda b, hp: (b, 0, off + hp))
    f32_buf = pltpu.VMEM((S, LANES), F32)
    bf16_buf = pltpu.VMEM((S, LANES), BF16)
    tmp_buf = pltpu.VMEM((unroll, ATTN_BLK, LANES), F32)
    return pl.pallas_call(
        kern,
        grid=(B, n_pairs),
        in_specs=[pl.BlockSpec(memory_space=pltpu.SMEM),
                  pl.BlockSpec((1, S, LANES), col(0)),
                  pl.BlockSpec((1, S, LANES), col(n_pairs)),
                  pl.BlockSpec((1, S, LANES), col(2 * n_pairs))],
        out_specs=pl.BlockSpec((1, S, LANES), lambda b, hp: (b, 0, hp)),
        out_shape=jax.ShapeDtypeStruct((B, S, attn_width), BF16),
        scratch_shapes=[f32_buf, f32_buf, f32_buf,
                        f32_buf, f32_buf, f32_buf,
                        bf16_buf, bf16_buf,
                        pltpu.VMEM((S, 2 * LANES), BF16),
                        f32_buf, f32_buf, f32_buf,
                        tmp_buf, tmp_buf, tmp_buf,
                        pltpu.VMEM((2, ATTN_BLK, 2 * ATTN_BLK), F32)],
        compiler_params=pltpu.CompilerParams(
            dimension_semantics=("arbitrary", "arbitrary"),
            vmem_limit_bytes=VMEM_LIMIT_BYTES),
        name="dil_attn",
    )(trips, proj, proj, proj)


def _mix_mlp_kernel(x_ref, ya_ref, yc_ref, qx_ref,
                    mem_ref, g_mem_ref, w_mem_ref,
                    g_attn_ref, g_xattn_ref, w_out_ref, g_post_mix_ref,
                    g_pre_mlp_ref, w_up_ref, w_down_ref, g_post_mlp_ref,
                    o_ref, km_ref, vm_ref, *, ff_chunk):
    t = pl.program_id(1)
    xw = km_ref.shape[1]

    @pl.when(t == 0)
    def _():
        hm = _rms(mem_ref[0], g_mem_ref[...]).astype(BF16)
        kv = jnp.dot(hm, w_mem_ref[...], preferred_element_type=F32)
        km_ref[...] = kv[:, :xw].astype(BF16)
        vm_ref[...] = kv[:, xw:].astype(BF16)

    tm = x_ref.shape[1]

    qx = qx_ref[0]
    km = km_ref[...]
    vm = vm_ref[...]
    lane = lax.broadcasted_iota(jnp.int32, (1, xw), 1)
    xhd = xw // N_MEM_HEADS
    assert xhd == HEAD_DIM
    sels = [(lane >= hd * xhd) & (lane < (hd + 1) * xhd) for hd in range(N_MEM_HEADS)]

    def scores(hd):
        qh = jnp.where(sels[hd], qx, jnp.zeros_like(qx))
        return lax.dot_general(qh, km, (((1,), (1,)), ((), ())),
                               preferred_element_type=F32)

    def softmax(sc):
        p = jnp.exp2(sc - jnp.max(sc, axis=-1, keepdims=True))
        return (p / jnp.sum(p, axis=-1, keepdims=True)).astype(BF16)

    sc, p, y_x = {}, {}, jnp.zeros((tm, xw), F32)
    for hd in range(N_MEM_HEADS + 2):
        if hd < N_MEM_HEADS:
            sc[hd] = scores(hd)
        if 0 <= hd - 1 < N_MEM_HEADS:
            p[hd - 1] = softmax(sc.pop(hd - 1))
        if 0 <= hd - 2 < N_MEM_HEADS:
            o = jnp.dot(p.pop(hd - 2), vm, preferred_element_type=F32)
            y_x = jnp.where(sels[hd - 2], o, y_x)

    halves = ((0, tm // 2), (tm // 2, tm))

    x1_h, h2_h = [], []
    for r0, r1 in halves:
        y = jnp.concatenate([
            _rms(ya_ref[0, r0:r1, :].astype(F32), g_attn_ref[...]).astype(BF16),
            yc_ref[0, r0:r1, :],
            _rms(y_x[r0:r1], g_xattn_ref[...]).astype(BF16)], axis=-1)
        y = jnp.dot(y, w_out_ref[...], preferred_element_type=F32)
        x1 = x_ref[0, r0:r1, :] + _rms(y, g_post_mix_ref[...])
        x1_h.append(x1)
        h2_h.append(_rms(x1, g_pre_mlp_ref[...]).astype(BF16))

    def act_fn(up):
        return jnp.square(jnp.maximum(up, 0.0)).astype(BF16)

    chunks = list(range(0, w_up_ref.shape[1], ff_chunk))
    h2 = jnp.concatenate(h2_h, axis=0)
    acc = None
    for f0 in chunks:
        w_up_c = w_up_ref[:, f0:f0 + ff_chunk]
        if f0 == chunks[0]:
            act = jnp.concatenate(
                [act_fn(jnp.dot(h, w_up_c, preferred_element_type=F32)) for h in h2_h],
                axis=0)
        else:
            act = act_fn(jnp.dot(h2, w_up_c, preferred_element_type=F32))
        w_down_c = w_down_ref[f0:f0 + ff_chunk, :]
        if f0 != chunks[-1]:
            part = jnp.dot(act, w_down_c, preferred_element_type=F32)
            acc = part if acc is None else acc + part
        else:
            for (r0, r1), x1 in zip(halves, x1_h):
                f = acc[r0:r1] + jnp.dot(act[r0:r1], w_down_c, preferred_element_type=F32)
                o_ref[0, r0:r1, :] = x1 + _rms(f, g_post_mlp_ref[...])


def _mix_mlp(x, y_attn, y_conv, proj, mem, g_mem, w_mem, g_attn, g_xattn,
             w_out, g_post_mix, g_pre_mlp, w_up, w_down, g_post_mlp,
             *, attn_width, conv_width, xattn_width, tm=512, ff_chunk=1024):
    B, S, D = x.shape
    n_mem = mem.shape[1]
    d_ff = w_up.shape[1]
    assert (3 * attn_width) % xattn_width == 0
    qx_block = 3 * attn_width // xattn_width
    const = lambda shape: pl.BlockSpec(shape, lambda b, t: (0,) * len(shape),
                                       pipeline_mode=pl.Buffered(1))
    return pl.pallas_call(
        functools.partial(_mix_mlp_kernel, ff_chunk=ff_chunk),
        grid=(B, S // tm),
        in_specs=[
            pl.BlockSpec((1, tm, D), lambda b, t: (b, t, 0)),
            pl.BlockSpec((1, tm, attn_width), lambda b, t: (b, t, 0)),
            pl.BlockSpec((1, tm, conv_width), lambda b, t: (b, t, 0)),
            pl.BlockSpec((1, tm, xattn_width), lambda b, t: (b, t, qx_block)),
            pl.BlockSpec((1, n_mem, D), lambda b, t: (b, 0, 0)),
            const((1, D)), const((D, 2 * xattn_width)),
            const((1, attn_width)), const((1, xattn_width)),
            const((D, D)), const((1, D)),
            const((1, D)), const((D, d_ff)), const((d_ff, D)), const((1, D)),
        ],
        out_specs=pl.BlockSpec((1, tm, D), lambda b, t: (b, t, 0)),
        out_shape=jax.ShapeDtypeStruct((B, S, D), x.dtype),
        scratch_shapes=[pltpu.VMEM((n_mem, xattn_width), BF16),
                        pltpu.VMEM((n_mem, xattn_width), BF16)],
        compiler_params=pltpu.CompilerParams(
            dimension_semantics=("parallel", "arbitrary"),
            vmem_limit_bytes=VMEM_LIMIT_BYTES),
        name="mix_mlp",
    )(x, y_attn, y_conv, proj, mem, g_mem, w_mem,
      g_attn, g_xattn, w_out, g_post_mix, g_pre_mlp, w_up, w_down, g_post_mlp)


def kernel(x, mem, positions, g_pre_mix, g_mem, w_in, w_mem_kv, conv_w, g_attn_out,
           g_conv_out, g_xattn_out, w_out, g_post_mix, g_pre_mlp, w_up, w_down,
           g_post_mlp):
    depth = w_in.shape[0]
    attn_width = g_attn_out.shape[1]
    conv_width = g_conv_out.shape[1]
    xattn_width = g_xattn_out.shape[1]
    pos3 = positions[:, None, :]
    row = lambda g: g[None, :]
    for l in range(depth):
        proj, y_conv, (w_mem_b, w_out_b, w_up_b, w_down_b) = _in_proj(
            x, pos3, row(g_pre_mix[l]), w_in[l], conv_w[l], row(g_conv_out[l]),
            (w_mem_kv[l], w_out[l], w_up[l], w_down[l]),
            attn_width=attn_width, conv_width=conv_width, xattn_width=xattn_width)
        y_attn = _dil_attn(proj, attn_width=attn_width)
        x = _mix_mlp(x, y_attn, y_conv, proj, mem, row(g_mem[l]), w_mem_b,
                     row(g_attn_out[l]), row(g_xattn_out[l]),
                     w_out_b, row(g_post_mix[l]), row(g_pre_mlp[l]), w_up_b, w_down_b,
                     row(g_post_mlp[l]),
                     attn_width=attn_width, conv_width=conv_width,
                     xattn_width=xattn_width)
    return x
```

```python
import functools

import jax
import jax.numpy as jnp
from jax import lax
from jax.experimental import pallas as pl
from jax.experimental.pallas import tpu as pltpu

F32 = jnp.float32
BF16 = jnp.bfloat16

HEAD_DIM = 64
DILATED_PATTERNS = ((128, 1), (512, 4), (2048, 16))
CONV_K = 3
ROPE_THETA = 10000.0
EPS = 1e-6
NEG_INF = -1e30
LOG2_E = 1.4426950408889634

LANES = 128
BF16_SUBLANES = 16
ATTN_BLK = 128
VMEM_LIMIT_BYTES = 56 * 1024 * 1024


def _rms(x, g):
    return x * lax.rsqrt(jnp.mean(x * x, axis=-1, keepdims=True) + EPS) * g


CONV_HALO = 8


def _in_proj_kernel(x_ref, pos_ref, g_ref, w_f32_ref, conv_w_ref, g_conv_ref,
                    mem_ref, g_mem_ref, w_mem_ref, g_xattn_ref, *rest,
                    attn_width, conv_width, xattn_width, chunk):
    n_cast = (len(rest) - 7) // 2
    cast_in, (o_ref, oc_ref, ox_ref) = rest[:n_cast], rest[n_cast:n_cast + 3]
    cast_out = rest[n_cast + 3:-4]
    w_ref, ztail_ref, km_ref, vm_ref = rest[-4:]
    t = pl.program_id(1)
    tm = x_ref.shape[1]
    xw = xattn_width

    @pl.when((pl.program_id(0) == 0) & (pl.program_id(1) == 0))
    def _():
        w_ref[...] = w_f32_ref[...].astype(BF16)
        ztail_ref[...] = jnp.zeros(ztail_ref.shape, F32)

    @pl.when(t == 0)
    def _():
        hm = _rms(mem_ref[0], g_mem_ref[...]).astype(BF16)
        kv = jnp.dot(hm, w_mem_ref[...].astype(BF16), preferred_element_type=F32)
        km_ref[...] = kv[:, :xw].astype(BF16)
        vm_ref[...] = kv[:, xw:].astype(BF16)

    h = _rms(x_ref[0], g_ref[...]).astype(BF16)

    half = HEAD_DIM // 2
    freq = lax.broadcasted_iota(jnp.int32, (half, 1), 0).astype(F32)
    inv_freq = jnp.float32(ROPE_THETA) ** (-(freq * 2.0 / HEAD_DIM))
    ang = inv_freq * pos_ref[0].astype(F32)
    cos_t, sin_t = jnp.cos(ang), jnp.sin(ang)
    reps = LANES // HEAD_DIM
    cos_k = jnp.concatenate([cos_t, cos_t] * reps, axis=0).T
    sin_k = jnp.concatenate([-sin_t, sin_t] * reps, axis=0).T
    q_scale = HEAD_DIM ** -0.5 * LOG2_E
    cos_q, sin_q = cos_k * q_scale, sin_k * q_scale
    lane = lax.broadcasted_iota(jnp.int32, (1, LANES), 1)
    first_half = (lane % HEAD_DIM) < half

    qkv = 3 * attn_width
    assert chunk == 2 * conv_width == conv_width + xattn_width and qkv % chunk == 0

    p_bc = jnp.dot(h, w_ref[:, qkv:qkv + chunk], preferred_element_type=F32)
    p_uq = jnp.dot(h, w_ref[:, qkv + chunk:qkv + 2 * chunk], preferred_element_type=F32)
    qx = (p_uq[:, conv_width:] * q_scale).astype(BF16)
    z = p_bc[:, conv_width:] * p_uq[:, :conv_width]
    tail = jnp.where(t > 0, ztail_ref[...], jnp.zeros(ztail_ref.shape, F32))
    z_ext = jnp.concatenate([tail, z], axis=0)
    cw = conv_w_ref[...]
    y_conv = z * cw[CONV_K - 1:CONV_K, :]
    for back in range(1, CONV_K):
        lo = CONV_HALO - back
        y_conv = y_conv + z_ext[lo:lo + tm, :] * cw[CONV_K - 1 - back:CONV_K - back, :]
    oc_ref[0] = _rms(p_bc[:, :conv_width] * y_conv, g_conv_ref[...]).astype(BF16)
    ztail_ref[...] = z[tm - CONV_HALO:, :]

    km = km_ref[...]
    vm = vm_ref[...]
    xlane = lax.broadcasted_iota(jnp.int32, (1, xw), 1)
    n_heads = xw // HEAD_DIM
    sels = [(xlane >= hd * HEAD_DIM) & (xlane < (hd + 1) * HEAD_DIM) for hd in range(n_heads)]
    xa = {"sc": {}, "p": {}, "y": jnp.zeros((tm, xw), F32)}

    def xattn_stage(hd):
        if hd < n_heads:
            qh = jnp.where(sels[hd], qx, jnp.zeros_like(qx))
            xa["sc"][hd] = lax.dot_general(qh, km, (((1,), (1,)), ((), ())),
                                           preferred_element_type=F32)
        if 0 <= hd - 1 < n_heads:
            sc = xa["sc"].pop(hd - 1)
            p = jnp.exp2(sc - jnp.max(sc, axis=-1, keepdims=True))
            xa["p"][hd - 1] = (p / jnp.sum(p, axis=-1, keepdims=True)).astype(BF16)
        if 0 <= hd - 2 < n_heads:
            o = jnp.dot(xa["p"].pop(hd - 2), vm, preferred_element_type=F32)
            xa["y"] = jnp.where(sels[hd - 2], o, xa["y"])

    stages = list(range(n_heads + 2))
    n_chunks = qkv // chunk
    per_chunk = -(-len(stages) // n_chunks)

    for ci, c0 in enumerate(range(0, qkv, chunk)):
        p = jnp.dot(h, w_ref[:, c0:c0 + chunk], preferred_element_type=F32)
        for hd in stages[ci * per_chunk:(ci + 1) * per_chunk]:
            xattn_stage(hd)
        if c0 < 2 * attn_width:
            cos, sin = (cos_q, sin_q) if c0 < attn_width else (cos_k, sin_k)
            for g0 in range(0, chunk, LANES):
                v = p[:, g0:g0 + LANES]
                rot = jnp.where(first_half,
                                pltpu.roll(v, LANES - half, 1),
                                pltpu.roll(v, half, 1))
                r = v * cos + rot * sin
                o_ref[0, :, c0 + g0:c0 + g0 + LANES] = r.astype(BF16)
        else:
            o_ref[0, :, c0:c0 + chunk] = p.astype(BF16)
        if c0 == 2 * attn_width:
            for src, dst in zip(cast_in, cast_out):
                dst[...] = src[...].astype(BF16)
    ox_ref[0] = _rms(xa["y"], g_xattn_ref[...]).astype(BF16)


def _in_proj(x, pos3, g, w_in, conv_w, g_conv, mem, g_mem, w_mem, g_xattn, later_weights,
             *, attn_width, conv_width, xattn_width, tm=1024, chunk=512):
    B, S, D = x.shape
    n_in = w_in.shape[1]
    n_out = 3 * attn_width
    n_mem = mem.shape[1]
    assert n_in == n_out + 3 * conv_width + xattn_width
    n_steps = B * (S // tm)
    slices = []
    for w in later_weights:
        rows = w.shape[0] // n_steps
        assert rows * n_steps == w.shape[0] and rows % BF16_SUBLANES == 0
        slices.append(pl.BlockSpec((rows, w.shape[1]), lambda b, t: (b * (S // tm) + t, 0)))
    out = pl.pallas_call(
        functools.partial(_in_proj_kernel, attn_width=attn_width, conv_width=conv_width,
                          xattn_width=xattn_width, chunk=chunk),
        grid=(B, S // tm),
        in_specs=[
            pl.BlockSpec((1, tm, D), lambda b, t: (b, t, 0)),
            pl.BlockSpec((1, 1, tm), lambda b, t: (b, 0, t)),
            pl.BlockSpec((1, D), lambda b, t: (0, 0)),
            pl.BlockSpec((D, n_in), lambda b, t: (0, 0), pipeline_mode=pl.Buffered(1)),
            pl.BlockSpec(conv_w.shape, lambda b, t: (0, 0)),
            pl.BlockSpec((1, conv_width), lambda b, t: (0, 0)),
            pl.BlockSpec((1, n_mem, D), lambda b, t: (b, 0, 0)),
            pl.BlockSpec((1, D), lambda b, t: (0, 0)),
            pl.BlockSpec(w_mem.shape, lambda b, t: (0, 0), pipeline_mode=pl.Buffered(1)),
            pl.BlockSpec((1, xattn_width), lambda b, t: (0, 0)),
        ] + slices,
        out_specs=[pl.BlockSpec((1, tm, n_out), lambda b, t: (b, t, 0)),
                   pl.BlockSpec((1, tm, conv_width), lambda b, t: (b, t, 0)),
                   pl.BlockSpec((1, tm, xattn_width), lambda b, t: (b, t, 0))] + slices,
        out_shape=[jax.ShapeDtypeStruct((B, S, n_out), BF16),
                   jax.ShapeDtypeStruct((B, S, conv_width), BF16),
                   jax.ShapeDtypeStruct((B, S, xattn_width), BF16)]
        + [jax.ShapeDtypeStruct(w.shape, BF16) for w in later_weights],
        scratch_shapes=[pltpu.VMEM(w_in.shape, BF16),
                        pltpu.VMEM((CONV_HALO, conv_width), F32),
                        pltpu.VMEM((n_mem, xattn_width), BF16),
                        pltpu.VMEM((n_mem, xattn_width), BF16)],
        compiler_params=pltpu.CompilerParams(
            dimension_semantics=("arbitrary", "arbitrary"),
            vmem_limit_bytes=VMEM_LIMIT_BYTES),
        name="in_proj",
    )(x, pos3, g, w_in, conv_w, g_conv, mem, g_mem, w_mem, g_xattn, *later_weights)
    return out[0], out[1], out[2], out[3:]


def _band_block(qb, kb, vb1, bias, lane_lo):
    ms, ls, accs = [], [], []
    for head_lo in (True, False):
        sel = lane_lo if head_lo else jnp.logical_not(lane_lo)
        qh = jnp.where(sel, qb, jnp.zeros_like(qb))
        s = lax.dot_general(qh, kb, (((1,), (1,)), ((), ())),
                            preferred_element_type=F32) + bias
        m = jnp.max(s, axis=-1, keepdims=True)
        p = jnp.exp2(s - m).astype(BF16)
        r = jnp.dot(p, vb1, preferred_element_type=F32)
        accs.append(r[:, :LANES])
        ls.append(r[:, LANES:])
        ms.append(jnp.broadcast_to(m, (m.shape[0], LANES)))
    return (jnp.where(lane_lo, ms[0], ms[1]), jnp.where(lane_lo, ls[0], ls[1]),
            jnp.where(lane_lo, accs[0], accs[1]))


COPY_ROWS = 64
DEINTERLEAVE = 4


def _for_each_split(seq, region, body):
    run = region // DEINTERLEAVE
    span = DEINTERLEAVE * COPY_ROWS
    steps_per_region = region // span

    def step(t, carry):
        strided0 = pl.multiple_of(t * span, span)
        dense0 = pl.multiple_of((t // steps_per_region) * region
                                + (t % steps_per_region) * COPY_ROWS, COPY_ROWS)
        for j in range(DEINTERLEAVE):
            body(pl.ds(strided0 + j, COPY_ROWS, stride=DEINTERLEAVE),
                 pl.ds(dense0 + j * run, COPY_ROWS))
        return carry

    lax.fori_loop(0, seq // span, step, 0)


def _merge_softmax(a, b):
    (m_a, l_a, a_a), (m_b, l_b, a_b) = a, b
    m = jnp.maximum(m_a, m_b)
    w_a = jnp.exp2(m_a - m)
    w_b = jnp.exp2(m_b - m)
    return m, l_a * w_a + l_b * w_b, a_a * w_a + a_b * w_b


def _dil_attn_kernel(trips_ref, q_ref, k_ref, v_ref, o_ref,
                     qf, kf, vf, q4f, k4f, v4f, qp, kp, vp1,
                     m_p4, l_p4, a_p4, m_tmp, l_tmp, a_tmp, bias_ref,
                     *, seq, patterns, unroll):
    blk = ATTN_BLK
    n_blocks = seq // blk
    lane_lo = lax.broadcasted_iota(jnp.int32, (1, LANES), 1) < HEAD_DIM
    assert [d for _, d in patterns] == [1, DEINTERLEAVE, DEINTERLEAVE ** 2]
    assert all(w // d == blk for w, d in patterns)
    run4 = seq // DEINTERLEAVE
    run16 = run4 // DEINTERLEAVE
    blocks_per_run4 = run4 // blk
    blocks_per_run16 = run16 // blk
    assert n_blocks % unroll == 0 and unroll % blocks_per_run4 == 0
    n_trips = trips_ref[0]

    @pl.when((pl.program_id(0) == 0) & (pl.program_id(1) == 0))
    def _():
        qi = lax.broadcasted_iota(jnp.int32, (blk, 2 * blk), 0)
        kj = lax.broadcasted_iota(jnp.int32, (blk, 2 * blk), 1)
        bias_ref[0] = jnp.where(kj <= qi, 0.0, NEG_INF).astype(F32)
        bias_ref[1] = jnp.where((kj >= qi) & (kj <= qi + blk), 0.0, NEG_INF).astype(F32)
        vp1[:, LANES:] = jnp.ones((seq, LANES), BF16)

    pat4 = (m_p4, l_p4, a_p4)
    tmp = (m_tmp, l_tmp, a_tmp)

    def attend(sub_len, q_src, k_src, finish):
        nb = sub_len // blk
        assert nb >= 2

        def trip_body(trip, carry):
            for u in range(unroll):
                g = trip * unroll + u
                n = g % nb
                q0 = pl.multiple_of(g * blk, blk)
                k0 = pl.multiple_of(jnp.where(n > 0, q0 - blk, q0), blk)
                triple = _band_block(q_src[pl.ds(q0, blk), :],
                                     k_src[pl.ds(k0, 2 * blk), :],
                                     vp1[pl.ds(k0, 2 * blk), :],
                                     bias_ref[jnp.minimum(n, 1)], lane_lo)
                finish(trip, u, q0, triple)
            return carry

        lax.fori_loop(0, n_trips, trip_body, 0)

    qf[...] = q_ref[0].astype(F32)
    kf[...] = k_ref[0].astype(F32)
    vf[...] = v_ref[0].astype(F32)

    def gather4(strided, dense):
        for src, dst_f, dst_b in ((qf, q4f, qp), (kf, k4f, kp)):
            rows = src[strided, :]
            dst_f[dense, :] = rows
            dst_b[dense, :] = rows.astype(BF16)
        rows = vf[strided, :]
        v4f[dense, :] = rows
        vp1[dense, :LANES] = rows.astype(BF16)

    def store_pat4(trip, u, q0, triple):
        for ref, val in zip(pat4, triple):
            ref[pl.ds(q0, blk), :] = val

    _for_each_split(seq, seq, gather4)
    attend(run4, qp, kp, store_pat4)

    def gather16(strided, dense):
        qp[dense, :] = q4f[strided, :].astype(BF16)
        kp[dense, :] = k4f[strided, :].astype(BF16)
        vp1[dense, :LANES] = v4f[strided, :].astype(BF16)

    def fold_into_pat4(trip, u, q0, triple):
        r_static, idx = divmod(u, blocks_per_run4)
        j, a0 = idx // blocks_per_run16, (idx % blocks_per_run16) * blk
        r_dyn = pl.multiple_of(trip * (unroll // blocks_per_run4) * run4, run4)
        rows = pl.ds(r_dyn + r_static * run4 + j + DEINTERLEAVE * a0, blk,
                     stride=DEINTERLEAVE)
        merged = _merge_softmax(tuple(ref[rows, :] for ref in pat4), triple)
        for ref, val in zip(pat4, merged):
            ref[rows, :] = val

    _for_each_split(seq, run4, gather16)
    attend(run16, qp, kp, fold_into_pat4)

    def finish_natural(trip, u, q0, triple):
        sub = blk // DEINTERLEAVE
        i_dyn = pl.multiple_of(trip * (unroll * sub), unroll * sub)
        for j in range(DEINTERLEAVE):
            src = pl.ds(i_dyn + j * run4 + u * sub, sub)
            for t_ref, p_ref in zip(tmp, pat4):
                t_ref[u, pl.ds(j, sub, stride=DEINTERLEAVE), :] = p_ref[src, :]
        _, l, acc = _merge_softmax(tuple(t_ref[u] for t_ref in tmp), triple)
        o_ref[0, pl.ds(q0, blk), :] = (acc / l).astype(o_ref.dtype)

    vp1[:, :LANES] = v_ref[0]
    attend(seq, q_ref.at[0], k_ref.at[0], finish_natural)


def _dil_attn(proj, *, attn_width, unroll=32):
    B, S, _ = proj.shape
    n_pairs = attn_width // LANES
    trips = jnp.full((1,), S // ATTN_BLK // unroll, jnp.int32)
    kern = functools.partial(_dil_attn_kernel, seq=S, patterns=DILATED_PATTERNS,
                             unroll=unroll)
    col = lambda off: (lambda b, hp: (b, 0, off + hp))
    f32_buf = pltpu.VMEM((S, LANES), F32)
    bf16_buf = pltpu.VMEM((S, LANES), BF16)
    tmp_buf = pltpu.VMEM((unroll, ATTN_BLK, LANES), F32)
    return pl.pallas_call(
        kern,
        grid=(B, n_pairs),
        in_specs=[pl.BlockSpec(memory_space=pltpu.SMEM),
                  pl.BlockSpec((1, S, LANES), col(0)),
                  pl.BlockSpec((1, S, LANES), col(n_pairs)),
                  pl.BlockSpec((1, S, LANES), col(2 * n_pairs))],
        out_specs=pl.BlockSpec((1, S, LANES), lambda b, hp: (b, 0, hp)),
        out_shape=jax.ShapeDtypeStruct((B, S, attn_width), BF16),
        scratch_shapes=[f32_buf, f32_buf, f32_buf,
                        f32_buf, f32_buf, f32_buf,
                        bf16_buf, bf16_buf,
                        pltpu.VMEM((S, 2 * LANES), BF16),
                        f32_buf, f32_buf, f32_buf,
                        tmp_buf, tmp_buf, tmp_buf,
                        pltpu.VMEM((2, ATTN_BLK, 2 * ATTN_BLK), F32)],
        compiler_params=pltpu.CompilerParams(
            dimension_semantics=("arbitrary", "arbitrary"),
            vmem_limit_bytes=VMEM_LIMIT_BYTES),
        name="dil_attn",
    )(trips, proj, proj, proj)


def _mix_mlp_kernel(x_ref, ya_ref, yc_ref, yx_ref, g_attn_ref, w_out_ref, g_post_mix_ref,
                    g_pre_mlp_ref, w_up_ref, w_down_ref, g_post_mlp_ref, o_ref, *, ff_chunk):
    tm = x_ref.shape[1]

    halves = ((0, tm // 2), (tm // 2, tm))

    x1_h, h2_h = [], []
    for r0, r1 in halves:
        y = jnp.concatenate([
            _rms(ya_ref[0, r0:r1, :].astype(F32), g_attn_ref[...]).astype(BF16),
            yc_ref[0, r0:r1, :], yx_ref[0, r0:r1, :]], axis=-1)
        y = jnp.dot(y, w_out_ref[...], preferred_element_type=F32)
        x1 = x_ref[0, r0:r1, :] + _rms(y, g_post_mix_ref[...])
        x1_h.append(x1)
        h2_h.append(_rms(x1, g_pre_mlp_ref[...]).astype(BF16))

    def act_fn(up):
        return jnp.square(jnp.maximum(up, 0.0)).astype(BF16)

    chunks = list(range(0, w_up_ref.shape[1], ff_chunk))
    h2 = jnp.concatenate(h2_h, axis=0)
    acc = None
    for f0 in chunks:
        w_up_c = w_up_ref[:, f0:f0 + ff_chunk]
        if f0 == chunks[0]:
            act = jnp.concatenate(
                [act_fn(jnp.dot(h, w_up_c, preferred_element_type=F32)) for h in h2_h],
                axis=0)
        else:
            act = act_fn(jnp.dot(h2, w_up_c, preferred_element_type=F32))
        w_down_c = w_down_ref[f0:f0 + ff_chunk, :]
        if f0 != chunks[-1]:
            part = jnp.dot(act, w_down_c, preferred_element_type=F32)
            acc = part if acc is None else acc + part
        else:
            for (r0, r1), x1 in zip(halves, x1_h):
                f = acc[r0:r1] + jnp.dot(act[r0:r1], w_down_c, preferred_element_type=F32)
                o_ref[0, r0:r1, :] = x1 + _rms(f, g_post_mlp_ref[...])


def _mix_mlp(x, y_attn, y_conv, y_x, g_attn, w_out, g_post_mix, g_pre_mlp, w_up, w_down,
             g_post_mlp, *, tm=512, ff_chunk=1024):
    B, S, D = x.shape
    d_ff = w_up.shape[1]
    const = lambda shape: pl.BlockSpec(shape, lambda b, t: (0,) * len(shape),
                                       pipeline_mode=pl.Buffered(1))
    rows = lambda a: pl.BlockSpec((1, tm, a.shape[-1]), lambda b, t: (b, t, 0))
    return pl.pallas_call(
        functools.partial(_mix_mlp_kernel, ff_chunk=ff_chunk),
        grid=(B, S // tm),
        in_specs=[
            rows(x), rows(y_attn), rows(y_conv), rows(y_x),
            const((1, y_attn.shape[-1])), const((D, D)), const((1, D)),
            const((1, D)), const((D, d_ff)), const((d_ff, D)), const((1, D)),
        ],
        out_specs=rows(x),
        out_shape=jax.ShapeDtypeStruct((B, S, D), x.dtype),
        compiler_params=pltpu.CompilerParams(
            dimension_semantics=("parallel", "parallel"),
            vmem_limit_bytes=VMEM_LIMIT_BYTES),
        name="mix_mlp",
    )(x, y_attn, y_conv, y_x, g_attn, w_out, g_post_mix, g_pre_mlp, w_up, w_down, g_post_mlp)


def kernel(x, mem, positions, g_pre_mix, g_mem, w_in, w_mem_kv, conv_w, g_attn_out,
           g_conv_out, g_xattn_out, w_out, g_post_mix, g_pre_mlp, w_up, w_down,
           g_post_mlp):
    depth = w_in.shape[0]
    attn_width = g_attn_out.shape[1]
    conv_width = g_conv_out.shape[1]
    xattn_width = g_xattn_out.shape[1]
    pos3 = positions[:, None, :]
    row = lambda g: g[None, :]
    for l in range(depth):
        proj, y_conv, y_x, (w_out_b, w_up_b, w_down_b) = _in_proj(
            x, pos3, row(g_pre_mix[l]), w_in[l], conv_w[l], row(g_conv_out[l]),
            mem, row(g_mem[l]), w_mem_kv[l], row(g_xattn_out[l]),
            (w_out[l], w_up[l], w_down[l]),
            attn_width=attn_width, conv_width=conv_width, xattn_width=xattn_width)
        y_attn = _dil_attn(proj, attn_width=attn_width)
        x = _mix_mlp(x, y_attn, y_conv, y_x, row(g_attn_out[l]), w_out_b,
                     row(g_post_mix[l]), row(g_pre_mlp[l]), w_up_b, w_down_b,
                     row(g_post_mlp[l]))
    return x
```

```python
import functools

import jax
import jax.numpy as jnp
from jax import lax
from jax.experimental import pallas as pl
from jax.experimental.pallas import tpu as pltpu

F32 = jnp.float32
BF16 = jnp.bfloat16

HEAD_DIM = 64
DILATED_PATTERNS = ((128, 1), (512, 4), (2048, 16))
CONV_K = 3
ROPE_THETA = 10000.0
EPS = 1e-6
NEG_INF = -1e30
LOG2_E = 1.4426950408889634

LANES = 128
BF16_SUBLANES = 16
ATTN_BLK = 128
VMEM_LIMIT_BYTES = 56 * 1024 * 1024


def _rms(x, g):
    return x * lax.rsqrt(jnp.mean(x * x, axis=-1, keepdims=True) + EPS) * g


CONV_HALO = 8


def _in_proj_kernel(x_ref, pos_ref, g_ref, w_f32_ref, conv_w_ref, g_conv_ref,
                    mem_ref, g_mem_ref, w_mem_ref, g_xattn_ref, *rest,
                    attn_width, conv_width, xattn_width, chunk):
    n_cast = (len(rest) - 7) // 2
    cast_in, (o_ref, oc_ref, ox_ref) = rest[:n_cast], rest[n_cast:n_cast + 3]
    cast_out = rest[n_cast + 3:-4]
    w_ref, ztail_ref, km_ref, vm_ref = rest[-4:]
    t = pl.program_id(1)
    tm = x_ref.shape[1]
    xw = xattn_width

    @pl.when((pl.program_id(0) == 0) & (pl.program_id(1) == 0))
    def _():
        w_ref[...] = w_f32_ref[...].astype(BF16)
        ztail_ref[...] = jnp.zeros(ztail_ref.shape, F32)

    @pl.when(t == 0)
    def _():
        hm = _rms(mem_ref[0], g_mem_ref[...]).astype(BF16)
        kv = jnp.dot(hm, w_mem_ref[...].astype(BF16), preferred_element_type=F32)
        km_ref[...] = kv[:, :xw].astype(BF16)
        vm_ref[...] = kv[:, xw:].astype(BF16)

    h = _rms(x_ref[0], g_ref[...]).astype(BF16)

    half = HEAD_DIM // 2
    freq = lax.broadcasted_iota(jnp.int32, (half, 1), 0).astype(F32)
    inv_freq = jnp.float32(ROPE_THETA) ** (-(freq * 2.0 / HEAD_DIM))
    ang = inv_freq * pos_ref[0].astype(F32)
    cos_t, sin_t = jnp.cos(ang), jnp.sin(ang)
    reps = LANES // HEAD_DIM
    cos_k = jnp.concatenate([cos_t, cos_t] * reps, axis=0).T
    sin_k = jnp.concatenate([-sin_t, sin_t] * reps, axis=0).T
    q_scale = HEAD_DIM ** -0.5 * LOG2_E
    cos_q, sin_q = cos_k * q_scale, sin_k * q_scale
    lane = lax.broadcasted_iota(jnp.int32, (1, LANES), 1)
    first_half = (lane % HEAD_DIM) < half

    qkv = 3 * attn_width
    assert chunk == 2 * conv_width == conv_width + xattn_width and qkv % chunk == 0

    p_bc = jnp.dot(h, w_ref[:, qkv:qkv + chunk], preferred_element_type=F32)
    p_uq = jnp.dot(h, w_ref[:, qkv + chunk:qkv + 2 * chunk], preferred_element_type=F32)
    qx = (p_uq[:, conv_width:] * q_scale).astype(BF16)
    z = p_bc[:, conv_width:] * p_uq[:, :conv_width]
    tail = jnp.where(t > 0, ztail_ref[...], jnp.zeros(ztail_ref.shape, F32))
    z_ext = jnp.concatenate([tail, z], axis=0)
    cw = conv_w_ref[...]
    y_conv = z * cw[CONV_K - 1:CONV_K, :]
    for back in range(1, CONV_K):
        lo = CONV_HALO - back
        y_conv = y_conv + z_ext[lo:lo + tm, :] * cw[CONV_K - 1 - back:CONV_K - back, :]
    oc_ref[0] = _rms(p_bc[:, :conv_width] * y_conv, g_conv_ref[...]).astype(BF16)
    ztail_ref[...] = z[tm - CONV_HALO:, :]

    km = km_ref[...]
    vm = vm_ref[...]
    xlane = lax.broadcasted_iota(jnp.int32, (1, xw), 1)
    n_heads = xw // HEAD_DIM
    sels = [(xlane >= hd * HEAD_DIM) & (xlane < (hd + 1) * HEAD_DIM) for hd in range(n_heads)]
    assert xw == 2 * LANES
    heads_per_group = LANES // HEAD_DIM
    xa = {"sc": {}, "p": {}, "y": [jnp.zeros((tm, LANES), F32)] * 2}

    def xattn_stage(hd):
        if hd < n_heads:
            qh = jnp.where(sels[hd], qx, jnp.zeros_like(qx))
            xa["sc"][hd] = lax.dot_general(qh, km, (((1,), (1,)), ((), ())),
                                           preferred_element_type=F32)
        if 0 <= hd - 1 < n_heads:
            sc = xa["sc"].pop(hd - 1)
            xa["p"][hd - 1] = jnp.exp2(sc - jnp.max(sc, axis=-1, keepdims=True)).astype(BF16)
        if 0 <= hd - 2 < n_heads:
            hv = hd - 2
            vm1 = jnp.where(sels[hv], vm, jnp.ones_like(vm))
            o = jnp.dot(xa["p"].pop(hv), vm1, preferred_element_type=F32)
            grp = hv // heads_per_group
            mine = o[:, grp * LANES:(grp + 1) * LANES]
            sums = o[:, (1 - grp) * LANES:(2 - grp) * LANES]
            xa["y"][grp] = jnp.where(sels[hv][:, grp * LANES:(grp + 1) * LANES],
                                     mine / sums, xa["y"][grp])

    stages = list(range(n_heads + 2))
    n_chunks = qkv // chunk
    per_chunk = -(-len(stages) // n_chunks)

    for ci, c0 in enumerate(range(0, qkv, chunk)):
        p = jnp.dot(h, w_ref[:, c0:c0 + chunk], preferred_element_type=F32)
        for hd in stages[ci * per_chunk:(ci + 1) * per_chunk]:
            xattn_stage(hd)
        if c0 < 2 * attn_width:
            cos, sin = (cos_q, sin_q) if c0 < attn_width else (cos_k, sin_k)
            for g0 in range(0, chunk, LANES):
                v = p[:, g0:g0 + LANES]
                rot = jnp.where(first_half,
                                pltpu.roll(v, LANES - half, 1),
                                pltpu.roll(v, half, 1))
                r = v * cos + rot * sin
                o_ref[0, :, c0 + g0:c0 + g0 + LANES] = r.astype(BF16)
        else:
            o_ref[0, :, c0:c0 + chunk] = p.astype(BF16)
        if c0 == 2 * attn_width:
            for src, dst in zip(cast_in, cast_out):
                dst[...] = src[...].astype(BF16)
    ox_ref[0] = _rms(jnp.concatenate(xa["y"], axis=-1), g_xattn_ref[...]).astype(BF16)


def _in_proj(x, pos3, g, w_in, conv_w, g_conv, mem, g_mem, w_mem, g_xattn, later_weights,
             *, attn_width, conv_width, xattn_width, tm=1024, chunk=512):
    B, S, D = x.shape
    n_in = w_in.shape[1]
    n_out = 3 * attn_width
    n_mem = mem.shape[1]
    assert n_in == n_out + 3 * conv_width + xattn_width
    n_steps = B * (S // tm)
    slices = []
    for w in later_weights:
        rows = w.shape[0] // n_steps
        assert rows * n_steps == w.shape[0] and rows % BF16_SUBLANES == 0
        slices.append(pl.BlockSpec((rows, w.shape[1]), lambda b, t: (b * (S // tm) + t, 0)))
    out = pl.pallas_call(
        functools.partial(_in_proj_kernel, attn_width=attn_width, conv_width=conv_width,
                          xattn_width=xattn_width, chunk=chunk),
        grid=(B, S // tm),
        in_specs=[
            pl.BlockSpec((1, tm, D), lambda b, t: (b, t, 0)),
            pl.BlockSpec((1, 1, tm), lambda b, t: (b, 0, t)),
            pl.BlockSpec((1, D), lambda b, t: (0, 0)),
            pl.BlockSpec((D, n_in), lambda b, t: (0, 0), pipeline_mode=pl.Buffered(1)),
            pl.BlockSpec(conv_w.shape, lambda b, t: (0, 0)),
            pl.BlockSpec((1, conv_width), lambda b, t: (0, 0)),
            pl.BlockSpec((1, n_mem, D), lambda b, t: (b, 0, 0)),
            pl.BlockSpec((1, D), lambda b, t: (0, 0)),
            pl.BlockSpec(w_mem.shape, lambda b, t: (0, 0), pipeline_mode=pl.Buffered(1)),
            pl.BlockSpec((1, xattn_width), lambda b, t: (0, 0)),
        ] + slices,
        out_specs=[pl.BlockSpec((1, tm, n_out), lambda b, t: (b, t, 0)),
                   pl.BlockSpec((1, tm, conv_width), lambda b, t: (b, t, 0)),
                   pl.BlockSpec((1, tm, xattn_width), lambda b, t: (b, t, 0))] + slices,
        out_shape=[jax.ShapeDtypeStruct((B, S, n_out), BF16),
                   jax.ShapeDtypeStruct((B, S, conv_width), BF16),
                   jax.ShapeDtypeStruct((B, S, xattn_width), BF16)]
        + [jax.ShapeDtypeStruct(w.shape, BF16) for w in later_weights],
        scratch_shapes=[pltpu.VMEM(w_in.shape, BF16),
                        pltpu.VMEM((CONV_HALO, conv_width), F32),
                        pltpu.VMEM((n_mem, xattn_width), BF16),
                        pltpu.VMEM((n_mem, xattn_width), BF16)],
        compiler_params=pltpu.CompilerParams(
            dimension_semantics=("arbitrary", "arbitrary"),
            vmem_limit_bytes=VMEM_LIMIT_BYTES),
        name="in_proj",
    )(x, pos3, g, w_in, conv_w, g_conv, mem, g_mem, w_mem, g_xattn, *later_weights)
    return out[0], out[1], out[2], out[3:]


def _band_block(qb, kb, vb1, bias, lane_lo):
    ms, ls, accs = [], [], []
    for head_lo in (True, False):
        sel = lane_lo if head_lo else jnp.logical_not(lane_lo)
        qh = jnp.where(sel, qb, jnp.zeros_like(qb))
        s = lax.dot_general(qh, kb, (((1,), (1,)), ((), ())),
                            preferred_element_type=F32) + bias
        m = jnp.max(s, axis=-1, keepdims=True)
        p = jnp.exp2(s - m).astype(BF16)
        r = jnp.dot(p, vb1, preferred_element_type=F32)
        accs.append(r[:, :LANES])
        ls.append(r[:, LANES:])
        ms.append(jnp.broadcast_to(m, (m.shape[0], LANES)))
    return (jnp.where(lane_lo, ms[0], ms[1]), jnp.where(lane_lo, ls[0], ls[1]),
            jnp.where(lane_lo, accs[0], accs[1]))


COPY_ROWS = 64
DEINTERLEAVE = 4


def _for_each_split(seq, region, body):
    run = region // DEINTERLEAVE
    span = DEINTERLEAVE * COPY_ROWS
    steps_per_region = region // span

    def step(t, carry):
        strided0 = pl.multiple_of(t * span, span)
        dense0 = pl.multiple_of((t // steps_per_region) * region
                                + (t % steps_per_region) * COPY_ROWS, COPY_ROWS)
        for j in range(DEINTERLEAVE):
            body(pl.ds(strided0 + j, COPY_ROWS, stride=DEINTERLEAVE),
                 pl.ds(dense0 + j * run, COPY_ROWS))
        return carry

    lax.fori_loop(0, seq // span, step, 0)


def _merge_softmax(a, b):
    (m_a, l_a, a_a), (m_b, l_b, a_b) = a, b
    m = jnp.maximum(m_a, m_b)
    w_a = jnp.exp2(m_a - m)
    w_b = jnp.exp2(m_b - m)
    return m, l_a * w_a + l_b * w_b, a_a * w_a + a_b * w_b


def _dil_attn_kernel(trips_ref, q_ref, k_ref, v_ref, o_ref,
                     qf, kf, vf, q4f, k4f, v4f, qp, kp, vp1,
                     m_p4, l_p4, a_p4, m_tmp, l_tmp, a_tmp, bias_ref,
                     *, seq, patterns, unroll):
    blk = ATTN_BLK
    n_blocks = seq // blk
    lane_lo = lax.broadcasted_iota(jnp.int32, (1, LANES), 1) < HEAD_DIM
    assert [d for _, d in patterns] == [1, DEINTERLEAVE, DEINTERLEAVE ** 2]
    assert all(w // d == blk for w, d in patterns)
    run4 = seq // DEINTERLEAVE
    run16 = run4 // DEINTERLEAVE
    blocks_per_run4 = run4 // blk
    blocks_per_run16 = run16 // blk
    assert n_blocks % unroll == 0 and unroll % blocks_per_run4 == 0
    n_trips = trips_ref[0]

    @pl.when((pl.program_id(0) == 0) & (pl.program_id(1) == 0))
    def _():
        qi = lax.broadcasted_iota(jnp.int32, (blk, 2 * blk), 0)
        kj = lax.broadcasted_iota(jnp.int32, (blk, 2 * blk), 1)
        bias_ref[0] = jnp.where(kj <= qi, 0.0, NEG_INF).astype(F32)
        bias_ref[1] = jnp.where((kj >= qi) & (kj <= qi + blk), 0.0, NEG_INF).astype(F32)
        vp1[:, LANES:] = jnp.ones((seq, LANES), BF16)

    pat4 = (m_p4, l_p4, a_p4)
    tmp = (m_tmp, l_tmp, a_tmp)

    def attend(sub_len, q_src, k_src, finish):
        nb = sub_len // blk
        assert nb >= 2

        def trip_body(trip, carry):
            for u in range(unroll):
                g = trip * unroll + u
                n = g % nb
                q0 = pl.multiple_of(g * blk, blk)
                k0 = pl.multiple_of(jnp.where(n > 0, q0 - blk, q0), blk)
                triple = _band_block(q_src[pl.ds(q0, blk), :],
                                     k_src[pl.ds(k0, 2 * blk), :],
                                     vp1[pl.ds(k0, 2 * blk), :],
                                     bias_ref[jnp.minimum(n, 1)], lane_lo)
                finish(trip, u, q0, triple)
            return carry

        lax.fori_loop(0, n_trips, trip_body, 0)

    qf[...] = q_ref[0].astype(F32)
    kf[...] = k_ref[0].astype(F32)
    vf[...] = v_ref[0].astype(F32)

    def gather4(strided, dense):
        for src, dst_f, dst_b in ((qf, q4f, qp), (kf, k4f, kp)):
            rows = src[strided, :]
            dst_f[dense, :] = rows
            dst_b[dense, :] = rows.astype(BF16)
        rows = vf[strided, :]
        v4f[dense, :] = rows
        vp1[dense, :LANES] = rows.astype(BF16)

    def store_pat4(trip, u, q0, triple):
        for ref, val in zip(pat4, triple):
            ref[pl.ds(q0, blk), :] = val

    _for_each_split(seq, seq, gather4)
    attend(run4, qp, kp, store_pat4)

    def gather16(strided, dense):
        qp[dense, :] = q4f[strided, :].astype(BF16)
        kp[dense, :] = k4f[strided, :].astype(BF16)
        vp1[dense, :LANES] = v4f[strided, :].astype(BF16)

    def fold_into_pat4(trip, u, q0, triple):
        r_static, idx = divmod(u, blocks_per_run4)
        j, a0 = idx // blocks_per_run16, (idx % blocks_per_run16) * blk
        r_dyn = pl.multiple_of(trip * (unroll // blocks_per_run4) * run4, run4)
        rows = pl.ds(r_dyn + r_static * run4 + j + DEINTERLEAVE * a0, blk,
                     stride=DEINTERLEAVE)
        merged = _merge_softmax(tuple(ref[rows, :] for ref in pat4), triple)
        for ref, val in zip(pat4, merged):
            ref[rows, :] = val

    _for_each_split(seq, run4, gather16)
    attend(run16, qp, kp, fold_into_pat4)

    def finish_natural(trip, u, q0, triple):
        sub = blk // DEINTERLEAVE
        i_dyn = pl.multiple_of(trip * (unroll * sub), unroll * sub)
        for j in range(DEINTERLEAVE):
            src = pl.ds(i_dyn + j * run4 + u * sub, sub)
            for t_ref, p_ref in zip(tmp, pat4):
                t_ref[u, pl.ds(j, sub, stride=DEINTERLEAVE), :] = p_ref[src, :]
        _, l, acc = _merge_softmax(tuple(t_ref[u] for t_ref in tmp), triple)
        o_ref[0, pl.ds(q0, blk), :] = (acc / l).astype(o_ref.dtype)

    vp1[:, :LANES] = v_ref[0]
    attend(seq, q_ref.at[0], k_ref.at[0], finish_natural)


def _dil_attn(proj, *, attn_width, unroll=32):
    B, S, _ = proj.shape
    n_pairs = attn_width // LANES
    trips = jnp.full((1,), S // ATTN_BLK // unroll, jnp.int32)
    kern = functools.partial(_dil_attn_kernel, seq=S, patterns=DILATED_PATTERNS,
                             unroll=unroll)
    col = lambda off: (lambda b, hp: (b, 0, off + hp))
    f32_buf = pltpu.VMEM((S, LANES), F32)
    bf16_buf = pltpu.VMEM((S, LANES), BF16)
    tmp_buf = pltpu.VMEM((unroll, ATTN_BLK, LANES), F32)
    return pl.pallas_call(
        kern,
        grid=(B, n_pairs),
        in_specs=[pl.BlockSpec(memory_space=pltpu.SMEM),
                  pl.BlockSpec((1, S, LANES), col(0)),
                  pl.BlockSpec((1, S, LANES), col(n_pairs)),
                  pl.BlockSpec((1, S, LANES), col(2 * n_pairs))],
        out_specs=pl.BlockSpec((1, S, LANES), lambda b, hp: (b, 0, hp)),
        out_shape=jax.ShapeDtypeStruct((B, S, attn_width), BF16),
        scratch_shapes=[f32_buf, f32_buf, f32_buf,
                        f32_buf, f32_buf, f32_buf,
                        bf16_buf, bf16_buf,
                        pltpu.VMEM((S, 2 * LANES), BF16),
                        f32_buf, f32_buf, f32_buf,
                        tmp_buf, tmp_buf, tmp_buf,
                        pltpu.VMEM((2, ATTN_BLK, 2 * ATTN_BLK), F32)],
        compiler_params=pltpu.CompilerParams(
            dimension_semantics=("arbitrary", "arbitrary"),
            vmem_limit_bytes=VMEM_LIMIT_BYTES),
        name="dil_attn",
    )(trips, proj, proj, proj)


def _mix_mlp_kernel(x_ref, ya_ref, yc_ref, yx_ref, g_attn_ref, w_out_ref, g_post_mix_ref,
                    g_pre_mlp_ref, w_up_ref, w_down_ref, g_post_mlp_ref, o_ref, *, ff_chunk):
    tm = x_ref.shape[1]

    halves = ((0, tm // 2), (tm // 2, tm))

    x1_h, h2_h = [], []
    for r0, r1 in halves:
        y = jnp.concatenate([
            _rms(ya_ref[0, r0:r1, :].astype(F32), g_attn_ref[...]).astype(BF16),
            yc_ref[0, r0:r1, :], yx_ref[0, r0:r1, :]], axis=-1)
        y = jnp.dot(y, w_out_ref[...], preferred_element_type=F32)
        x1 = x_ref[0, r0:r1, :] + _rms(y, g_post_mix_ref[...])
        x1_h.append(x1)
        h2_h.append(_rms(x1, g_pre_mlp_ref[...]).astype(BF16))

    def act_fn(up):
        return jnp.square(jnp.maximum(up, 0.0)).astype(BF16)

    chunks = list(range(0, w_up_ref.shape[1], ff_chunk))
    h2 = jnp.concatenate(h2_h, axis=0)
    acc = None
    for f0 in chunks:
        w_up_c = w_up_ref[:, f0:f0 + ff_chunk]
        if f0 == chunks[0]:
            act = jnp.concatenate(
                [act_fn(jnp.dot(h, w_up_c, preferred_element_type=F32)) for h in h2_h],
                axis=0)
        else:
            act = act_fn(jnp.dot(h2, w_up_c, preferred_element_type=F32))
        w_down_c = w_down_ref[f0:f0 + ff_chunk, :]
        if f0 != chunks[-1]:
            part = jnp.dot(act, w_down_c, preferred_element_type=F32)
            acc = part if acc is None else acc + part
        else:
            for (r0, r1), x1 in zip(halves, x1_h):
                f = acc[r0:r1] + jnp.dot(act[r0:r1], w_down_c, preferred_element_type=F32)
                o_ref[0, r0:r1, :] = x1 + _rms(f, g_post_mlp_ref[...])


def _mix_mlp(x, y_attn, y_conv, y_x, g_attn, w_out, g_post_mix, g_pre_mlp, w_up, w_down,
             g_post_mlp, *, tm=512, ff_chunk=1024):
    B, S, D = x.shape
    d_ff = w_up.shape[1]
    const = lambda shape: pl.BlockSpec(shape, lambda b, t: (0,) * len(shape),
                                       pipeline_mode=pl.Buffered(1))
    rows = lambda a: pl.BlockSpec((1, tm, a.shape[-1]), lambda b, t: (b, t, 0))
    return pl.pallas_call(
        functools.partial(_mix_mlp_kernel, ff_chunk=ff_chunk),
        grid=(B, S // tm),
        in_specs=[
            rows(x), rows(y_attn), rows(y_conv), rows(y_x),
            const((1, y_attn.shape[-1])), const((D, D)), const((1, D)),
            const((1, D)), const((D, d_ff)), const((d_ff, D)), const((1, D)),
        ],
        out_specs=rows(x),
        out_shape=jax.ShapeDtypeStruct((B, S, D), x.dtype),
        compiler_params=pltpu.CompilerParams(
            dimension_semantics=("parallel", "parallel"),
            vmem_limit_bytes=VMEM_LIMIT_BYTES),
        name="mix_mlp",
    )(x, y_attn, y_conv, y_x, g_attn, w_out, g_post_mix, g_pre_mlp, w_up, w_down, g_post_mlp)


def kernel(x, mem, positions, g_pre_mix, g_mem, w_in, w_mem_kv, conv_w, g_attn_out,
           g_conv_out, g_xattn_out, w_out, g_post_mix, g_pre_mlp, w_up, w_down,
           g_post_mlp):
    depth = w_in.shape[0]
    attn_width = g_attn_out.shape[1]
    conv_width = g_conv_out.shape[1]
    xattn_width = g_xattn_out.shape[1]
    pos3 = positions[:, None, :]
    row = lambda g: g[None, :]
    for l in range(depth):
        proj, y_conv, y_x, (w_out_b, w_up_b, w_down_b) = _in_proj(
            x, pos3, row(g_pre_mix[l]), w_in[l], conv_w[l], row(g_conv_out[l]),
            mem, row(g_mem[l]), w_mem_kv[l], row(g_xattn_out[l]),
            (w_out[l], w_up[l], w_down[l]),
            attn_width=attn_width, conv_width=conv_width, xattn_width=xattn_width)
        y_attn = _dil_attn(proj, attn_width=attn_width)
        x = _mix_mlp(x, y_attn, y_conv, y_x, row(g_attn_out[l]), w_out_b,
                     row(g_post_mix[l]), row(g_pre_mlp[l]), w_up_b, w_down_b,
                     row(g_post_mlp[l]))
    return x
```

```python
import functools

import jax
import jax.numpy as jnp
from jax import lax
from jax.experimental import pallas as pl
from jax.experimental.pallas import tpu as pltpu

F32 = jnp.float32
BF16 = jnp.bfloat16

HEAD_DIM = 64
DILATED_PATTERNS = ((128, 1), (512, 4), (2048, 16))
CONV_K = 3
ROPE_THETA = 10000.0
EPS = 1e-6
NEG_INF = -1e30
LOG2_E = 1.4426950408889634

LANES = 128
BF16_SUBLANES = 16
ATTN_BLK = 128
VMEM_LIMIT_BYTES = 56 * 1024 * 1024


def _rms(x, g):
    return x * lax.rsqrt(jnp.mean(x * x, axis=-1, keepdims=True) + EPS) * g


CONV_HALO = 8


def _in_proj_kernel(x_ref, pos_ref, g_ref, w_f32_ref, conv_w_ref, g_conv_ref,
                    mem_ref, g_mem_ref, w_mem_ref, g_xattn_ref, *rest,
                    attn_width, conv_width, xattn_width, chunk):
    n_cast = (len(rest) - 7) // 2
    cast_in, (o_ref, oc_ref, ox_ref) = rest[:n_cast], rest[n_cast:n_cast + 3]
    cast_out = rest[n_cast + 3:-4]
    w_ref, ztail_ref, km_ref, vm_ref = rest[-4:]
    t = pl.program_id(1)
    tm = x_ref.shape[1]
    xw = xattn_width

    @pl.when((pl.program_id(0) == 0) & (pl.program_id(1) == 0))
    def _():
        w_ref[...] = w_f32_ref[...].astype(BF16)
        ztail_ref[...] = jnp.zeros(ztail_ref.shape, F32)

    @pl.when(t == 0)
    def _():
        hm = _rms(mem_ref[0], g_mem_ref[...]).astype(BF16)
        kv = jnp.dot(hm, w_mem_ref[...].astype(BF16), preferred_element_type=F32)
        km_ref[...] = kv[:, :xw].T.astype(BF16)
        vm_ref[...] = kv[:, xw:].astype(BF16)

    h = _rms(x_ref[0], g_ref[...]).astype(BF16)

    half = HEAD_DIM // 2
    freq = lax.broadcasted_iota(jnp.int32, (half, 1), 0).astype(F32)
    inv_freq = jnp.float32(ROPE_THETA) ** (-(freq * 2.0 / HEAD_DIM))
    ang = inv_freq * pos_ref[0].astype(F32)
    cos_t, sin_t = jnp.cos(ang), jnp.sin(ang)
    reps = LANES // HEAD_DIM
    cos_k = jnp.concatenate([cos_t, cos_t] * reps, axis=0).T
    sin_k = jnp.concatenate([-sin_t, sin_t] * reps, axis=0).T
    q_scale = HEAD_DIM ** -0.5 * LOG2_E
    cos_q, sin_q = cos_k * q_scale, sin_k * q_scale
    lane = lax.broadcasted_iota(jnp.int32, (1, LANES), 1)
    first_half = (lane % HEAD_DIM) < half

    qkv = 3 * attn_width
    assert chunk == 2 * conv_width == conv_width + xattn_width and qkv % chunk == 0

    p_bc = jnp.dot(h, w_ref[:, qkv:qkv + chunk], preferred_element_type=F32)
    p_uq = jnp.dot(h, w_ref[:, qkv + chunk:qkv + 2 * chunk], preferred_element_type=F32)
    qx = (p_uq[:, conv_width:] * q_scale).astype(BF16)
    z = p_bc[:, conv_width:] * p_uq[:, :conv_width]
    tail = jnp.where(t > 0, ztail_ref[...], jnp.zeros(ztail_ref.shape, F32))
    z_ext = jnp.concatenate([tail, z], axis=0)
    cw = conv_w_ref[...]
    y_conv = z * cw[CONV_K - 1:CONV_K, :]
    for back in range(1, CONV_K):
        lo = CONV_HALO - back
        y_conv = y_conv + z_ext[lo:lo + tm, :] * cw[CONV_K - 1 - back:CONV_K - back, :]
    oc_ref[0] = _rms(p_bc[:, :conv_width] * y_conv, g_conv_ref[...]).astype(BF16)
    ztail_ref[...] = z[tm - CONV_HALO:, :]

    km = km_ref[...]
    vm = vm_ref[...]
    xlane = lax.broadcasted_iota(jnp.int32, (1, xw), 1)
    n_heads = xw // HEAD_DIM
    sels = [(xlane >= hd * HEAD_DIM) & (xlane < (hd + 1) * HEAD_DIM) for hd in range(n_heads)]
    assert xw == 2 * LANES
    heads_per_group = LANES // HEAD_DIM
    xa = {"sc": {}, "p": {}, "y": [jnp.zeros((tm, LANES), F32)] * 2}

    def xattn_stage(hd):
        if hd < n_heads:
            qh = jnp.where(sels[hd], qx, jnp.zeros_like(qx))
            xa["sc"][hd] = jnp.dot(qh, km, preferred_element_type=F32)
        if 0 <= hd - 1 < n_heads:
            sc = xa["sc"].pop(hd - 1)
            xa["p"][hd - 1] = jnp.exp2(sc - jnp.max(sc, axis=-1, keepdims=True)).astype(BF16)
        if 0 <= hd - 2 < n_heads:
            hv = hd - 2
            vm1 = jnp.where(sels[hv], vm, jnp.ones_like(vm))
            o = jnp.dot(xa["p"].pop(hv), vm1, preferred_element_type=F32)
            grp = hv // heads_per_group
            mine = o[:, grp * LANES:(grp + 1) * LANES]
            sums = o[:, (1 - grp) * LANES:(2 - grp) * LANES]
            xa["y"][grp] = jnp.where(sels[hv][:, grp * LANES:(grp + 1) * LANES],
                                     mine / sums, xa["y"][grp])

    stages = list(range(n_heads + 2))
    n_chunks = qkv // chunk
    per_chunk = -(-len(stages) // n_chunks)

    for ci, c0 in enumerate(range(0, qkv, chunk)):
        p = jnp.dot(h, w_ref[:, c0:c0 + chunk], preferred_element_type=F32)
        for hd in stages[ci * per_chunk:(ci + 1) * per_chunk]:
            xattn_stage(hd)
        if c0 < 2 * attn_width:
            cos, sin = (cos_q, sin_q) if c0 < attn_width else (cos_k, sin_k)
            for g0 in range(0, chunk, LANES):
                v = p[:, g0:g0 + LANES]
                rot = jnp.where(first_half,
                                pltpu.roll(v, LANES - half, 1),
                                pltpu.roll(v, half, 1))
                r = v * cos + rot * sin
                o_ref[0, :, c0 + g0:c0 + g0 + LANES] = r.astype(BF16)
        else:
            o_ref[0, :, c0:c0 + chunk] = p.astype(BF16)
        if c0 == 2 * attn_width:
            for src, dst in zip(cast_in, cast_out):
                dst[...] = src[...].astype(BF16)
    ox_ref[0] = _rms(jnp.concatenate(xa["y"], axis=-1), g_xattn_ref[...]).astype(BF16)


def _in_proj(x, pos3, g, w_in, conv_w, g_conv, mem, g_mem, w_mem, g_xattn, later_weights,
             *, attn_width, conv_width, xattn_width, tm=1024, chunk=512):
    B, S, D = x.shape
    n_in = w_in.shape[1]
    n_out = 3 * attn_width
    n_mem = mem.shape[1]
    assert n_in == n_out + 3 * conv_width + xattn_width
    n_steps = B * (S // tm)
    slices = []
    for w in later_weights:
        rows = w.shape[0] // n_steps
        assert rows * n_steps == w.shape[0] and rows % BF16_SUBLANES == 0
        slices.append(pl.BlockSpec((rows, w.shape[1]), lambda b, t: (b * (S // tm) + t, 0)))
    out = pl.pallas_call(
        functools.partial(_in_proj_kernel, attn_width=attn_width, conv_width=conv_width,
                          xattn_width=xattn_width, chunk=chunk),
        grid=(B, S // tm),
        in_specs=[
            pl.BlockSpec((1, tm, D), lambda b, t: (b, t, 0)),
            pl.BlockSpec((1, 1, tm), lambda b, t: (b, 0, t)),
            pl.BlockSpec((1, D), lambda b, t: (0, 0)),
            pl.BlockSpec((D, n_in), lambda b, t: (0, 0), pipeline_mode=pl.Buffered(1)),
            pl.BlockSpec(conv_w.shape, lambda b, t: (0, 0)),
            pl.BlockSpec((1, conv_width), lambda b, t: (0, 0)),
            pl.BlockSpec((1, n_mem, D), lambda b, t: (b, 0, 0)),
            pl.BlockSpec((1, D), lambda b, t: (0, 0)),
            pl.BlockSpec(w_mem.shape, lambda b, t: (0, 0), pipeline_mode=pl.Buffered(1)),
            pl.BlockSpec((1, xattn_width), lambda b, t: (0, 0)),
        ] + slices,
        out_specs=[pl.BlockSpec((1, tm, n_out), lambda b, t: (b, t, 0)),
                   pl.BlockSpec((1, tm, conv_width), lambda b, t: (b, t, 0)),
                   pl.BlockSpec((1, tm, xattn_width), lambda b, t: (b, t, 0))] + slices,
        out_shape=[jax.ShapeDtypeStruct((B, S, n_out), BF16),
                   jax.ShapeDtypeStruct((B, S, conv_width), BF16),
                   jax.ShapeDtypeStruct((B, S, xattn_width), BF16)]
        + [jax.ShapeDtypeStruct(w.shape, BF16) for w in later_weights],
        scratch_shapes=[pltpu.VMEM(w_in.shape, BF16),
                        pltpu.VMEM((CONV_HALO, conv_width), F32),
                        pltpu.VMEM((xattn_width, n_mem), BF16),
                        pltpu.VMEM((n_mem, xattn_width), BF16)],
        compiler_params=pltpu.CompilerParams(
            dimension_semantics=("arbitrary", "arbitrary"),
            vmem_limit_bytes=VMEM_LIMIT_BYTES),
        name="in_proj",
    )(x, pos3, g, w_in, conv_w, g_conv, mem, g_mem, w_mem, g_xattn, *later_weights)
    return out[0], out[1], out[2], out[3:]


def _band_block(qb, kb, vb1, bias, lane_lo):
    ms, ls, accs = [], [], []
    for head_lo in (True, False):
        sel = lane_lo if head_lo else jnp.logical_not(lane_lo)
        qh = jnp.where(sel, qb, jnp.zeros_like(qb))
        s = lax.dot_general(qh, kb, (((1,), (1,)), ((), ())),
                            preferred_element_type=F32) + bias
        m = jnp.max(s, axis=-1, keepdims=True)
        p = jnp.exp2(s - m).astype(BF16)
        r = jnp.dot(p, vb1, preferred_element_type=F32)
        accs.append(r[:, :LANES])
        ls.append(r[:, LANES:])
        ms.append(jnp.broadcast_to(m, (m.shape[0], LANES)))
    return (jnp.where(lane_lo, ms[0], ms[1]), jnp.where(lane_lo, ls[0], ls[1]),
            jnp.where(lane_lo, accs[0], accs[1]))


COPY_ROWS = 64
DEINTERLEAVE = 4


def _for_each_split(seq, region, body):
    run = region // DEINTERLEAVE
    span = DEINTERLEAVE * COPY_ROWS
    steps_per_region = region // span

    def step(t, carry):
        strided0 = pl.multiple_of(t * span, span)
        dense0 = pl.multiple_of((t // steps_per_region) * region
                                + (t % steps_per_region) * COPY_ROWS, COPY_ROWS)
        for j in range(DEINTERLEAVE):
            body(pl.ds(strided0 + j, COPY_ROWS, stride=DEINTERLEAVE),
                 pl.ds(dense0 + j * run, COPY_ROWS))
        return carry

    lax.fori_loop(0, seq // span, step, 0)


def _merge_softmax(a, b):
    (m_a, l_a, a_a), (m_b, l_b, a_b) = a, b
    m = jnp.maximum(m_a, m_b)
    w_a = jnp.exp2(m_a - m)
    w_b = jnp.exp2(m_b - m)
    return m, l_a * w_a + l_b * w_b, a_a * w_a + a_b * w_b


def _dil_attn_kernel(trips_ref, q_ref, k_ref, v_ref, o_ref,
                     qf, kf, vf, q4f, k4f, v4f, qp, kp, vp1,
                     m_p4, l_p4, a_p4, m_tmp, l_tmp, a_tmp, bias_ref,
                     *, seq, patterns, unroll):
    blk = ATTN_BLK
    n_blocks = seq // blk
    lane_lo = lax.broadcasted_iota(jnp.int32, (1, LANES), 1) < HEAD_DIM
    assert [d for _, d in patterns] == [1, DEINTERLEAVE, DEINTERLEAVE ** 2]
    assert all(w // d == blk for w, d in patterns)
    run4 = seq // DEINTERLEAVE
    run16 = run4 // DEINTERLEAVE
    blocks_per_run4 = run4 // blk
    blocks_per_run16 = run16 // blk
    assert n_blocks % unroll == 0 and unroll % blocks_per_run4 == 0
    n_trips = trips_ref[0]

    @pl.when((pl.program_id(0) == 0) & (pl.program_id(1) == 0))
    def _():
        qi = lax.broadcasted_iota(jnp.int32, (blk, 2 * blk), 0)
        kj = lax.broadcasted_iota(jnp.int32, (blk, 2 * blk), 1)
        bias_ref[0] = jnp.where(kj <= qi, 0.0, NEG_INF).astype(F32)
        bias_ref[1] = jnp.where((kj >= qi) & (kj <= qi + blk), 0.0, NEG_INF).astype(F32)
        vp1[:, LANES:] = jnp.ones((seq, LANES), BF16)

    pat4 = (m_p4, l_p4, a_p4)
    tmp = (m_tmp, l_tmp, a_tmp)

    def attend(sub_len, q_src, k_src, finish):
        nb = sub_len // blk
        assert nb >= 2

        def trip_body(trip, carry):
            for u in range(unroll):
                g = trip * unroll + u
                n = g % nb
                q0 = pl.multiple_of(g * blk, blk)
                k0 = pl.multiple_of(jnp.where(n > 0, q0 - blk, q0), blk)
                triple = _band_block(q_src[pl.ds(q0, blk), :],
                                     k_src[pl.ds(k0, 2 * blk), :],
                                     vp1[pl.ds(k0, 2 * blk), :],
                                     bias_ref[jnp.minimum(n, 1)], lane_lo)
                finish(trip, u, q0, triple)
            return carry

        lax.fori_loop(0, n_trips, trip_body, 0)

    qf[...] = q_ref[0].astype(F32)
    kf[...] = k_ref[0].astype(F32)
    vf[...] = v_ref[0].astype(F32)

    def gather4(strided, dense):
        for src, dst_f, dst_b in ((qf, q4f, qp), (kf, k4f, kp)):
            rows = src[strided, :]
            dst_f[dense, :] = rows
            dst_b[dense, :] = rows.astype(BF16)
        rows = vf[strided, :]
        v4f[dense, :] = rows
        vp1[dense, :LANES] = rows.astype(BF16)

    def store_pat4(trip, u, q0, triple):
        for ref, val in zip(pat4, triple):
            ref[pl.ds(q0, blk), :] = val

    _for_each_split(seq, seq, gather4)
    attend(run4, qp, kp, store_pat4)

    def gather16(strided, dense):
        qp[dense, :] = q4f[strided, :].astype(BF16)
        kp[dense, :] = k4f[strided, :].astype(BF16)
        vp1[dense, :LANES] = v4f[strided, :].astype(BF16)

    def fold_into_pat4(trip, u, q0, triple):
        r_static, idx = divmod(u, blocks_per_run4)
        j, a0 = idx // blocks_per_run16, (idx % blocks_per_run16) * blk
        r_dyn = pl.multiple_of(trip * (unroll // blocks_per_run4) * run4, run4)
        rows = pl.ds(r_dyn + r_static * run4 + j + DEINTERLEAVE * a0, blk,
                     stride=DEINTERLEAVE)
        merged = _merge_softmax(tuple(ref[rows, :] for ref in pat4), triple)
        for ref, val in zip(pat4, merged):
            ref[rows, :] = val

    _for_each_split(seq, run4, gather16)
    attend(run16, qp, kp, fold_into_pat4)

    def finish_natural(trip, u, q0, triple):
        sub = blk // DEINTERLEAVE
        i_dyn = pl.multiple_of(trip * (unroll * sub), unroll * sub)
        for j in range(DEINTERLEAVE):
            src = pl.ds(i_dyn + j * run4 + u * sub, sub)
            for t_ref, p_ref in zip(tmp, pat4):
                t_ref[u, pl.ds(j, sub, stride=DEINTERLEAVE), :] = p_ref[src, :]
        _, l, acc = _merge_softmax(tuple(t_ref[u] for t_ref in tmp), triple)
        o_ref[0, pl.ds(q0, blk), :] = (acc / l).astype(o_ref.dtype)

    vp1[:, :LANES] = v_ref[0]
    attend(seq, q_ref.at[0], k_ref.at[0], finish_natural)


def _dil_attn(proj, *, attn_width, unroll=32):
    B, S, _ = proj.shape
    n_pairs = attn_width // LANES
    trips = jnp.full((1,), S // ATTN_BLK // unroll, jnp.int32)
    kern = functools.partial(_dil_attn_kernel, seq=S, patterns=DILATED_PATTERNS,
                             unroll=unroll)
    col = lambda off: (lambda b, hp: (b, 0, off + hp))
    f32_buf = pltpu.VMEM((S, LANES), F32)
    bf16_buf = pltpu.VMEM((S, LANES), BF16)
    tmp_buf = pltpu.VMEM((unroll, ATTN_BLK, LANES), F32)
    return pl.pallas_call(
        kern,
        grid=(B, n_pairs),
        in_specs=[pl.BlockSpec(memory_space=pltpu.SMEM),
                  pl.BlockSpec((1, S, LANES), col(0)),
                  pl.BlockSpec((1, S, LANES), col(n_pairs)),
                  pl.BlockSpec((1, S, LANES), col(2 * n_pairs))],
        out_specs=pl.BlockSpec((1, S, LANES), lambda b, hp: (b, 0, hp)),
        out_shape=jax.ShapeDtypeStruct((B, S, attn_width), BF16),
        scratch_shapes=[f32_buf, f32_buf, f32_buf,
                        f32_buf, f32_buf, f32_buf,
                        bf16_buf, bf16_buf,
                        pltpu.VMEM((S, 2 * LANES), BF16),
                        f32_buf, f32_buf, f32_buf,
                        tmp_buf, tmp_buf, tmp_buf,
                        pltpu.VMEM((2, ATTN_BLK, 2 * ATTN_BLK), F32)],
        compiler_params=pltpu.CompilerParams(
            dimension_semantics=("arbitrary", "arbitrary"),
            vmem_limit_bytes=VMEM_LIMIT_BYTES),
        name="dil_attn",
    )(trips, proj, proj, proj)


def _mix_mlp_kernel(x_ref, ya_ref, yc_ref, yx_ref, g_attn_ref, w_out_ref, g_post_mix_ref,
                    g_pre_mlp_ref, w_up_ref, w_down_ref, g_post_mlp_ref, o_ref, *, ff_chunk):
    tm = x_ref.shape[1]

    halves = ((0, tm // 2), (tm // 2, tm))

    x1_h, h2_h = [], []
    for r0, r1 in halves:
        y = jnp.concatenate([
            _rms(ya_ref[0, r0:r1, :].astype(F32), g_attn_ref[...]).astype(BF16),
            yc_ref[0, r0:r1, :], yx_ref[0, r0:r1, :]], axis=-1)
        y = jnp.dot(y, w_out_ref[...], preferred_element_type=F32)
        x1 = x_ref[0, r0:r1, :] + _rms(y, g_post_mix_ref[...])
        x1_h.append(x1)
        h2_h.append(_rms(x1, g_pre_mlp_ref[...]).astype(BF16))

    def act_fn(up):
        return jnp.square(jnp.maximum(up, 0.0)).astype(BF16)

    chunks = list(range(0, w_up_ref.shape[1], ff_chunk))
    h2 = jnp.concatenate(h2_h, axis=0)
    acc = None
    for f0 in chunks:
        w_up_c = w_up_ref[:, f0:f0 + ff_chunk]
        if f0 == chunks[0]:
            act = jnp.concatenate(
                [act_fn(jnp.dot(h, w_up_c, preferred_element_type=F32)) for h in h2_h],
                axis=0)
        else:
            act = act_fn(jnp.dot(h2, w_up_c, preferred_element_type=F32))
        w_down_c = w_down_ref[f0:f0 + ff_chunk, :]
        if f0 != chunks[-1]:
            part = jnp.dot(act, w_down_c, preferred_element_type=F32)
            acc = part if acc is None else acc + part
        else:
            for (r0, r1), x1 in zip(halves, x1_h):
                f = acc[r0:r1] + jnp.dot(act[r0:r1], w_down_c, preferred_element_type=F32)
                o_ref[0, r0:r1, :] = x1 + _rms(f, g_post_mlp_ref[...])


def _mix_mlp(x, y_attn, y_conv, y_x, g_attn, w_out, g_post_mix, g_pre_mlp, w_up, w_down,
             g_post_mlp, *, tm=512, ff_chunk=1024):
    B, S, D = x.shape
    d_ff = w_up.shape[1]
    const = lambda shape: pl.BlockSpec(shape, lambda b, t: (0,) * len(shape),
                                       pipeline_mode=pl.Buffered(1))
    rows = lambda a: pl.BlockSpec((1, tm, a.shape[-1]), lambda b, t: (b, t, 0))
    return pl.pallas_call(
        functools.partial(_mix_mlp_kernel, ff_chunk=ff_chunk),
        grid=(B, S // tm),
        in_specs=[
            rows(x), rows(y_attn), rows(y_conv), rows(y_x),
            const((1, y_attn.shape[-1])), const((D, D)), const((1, D)),
            const((1, D)), const((D, d_ff)), const((d_ff, D)), const((1, D)),
        ],
        out_specs=rows(x),
        out_shape=jax.ShapeDtypeStruct((B, S, D), x.dtype),
        compiler_params=pltpu.CompilerParams(
            dimension_semantics=("parallel", "parallel"),
            vmem_limit_bytes=VMEM_LIMIT_BYTES),
        name="mix_mlp",
    )(x, y_attn, y_conv, y_x, g_attn, w_out, g_post_mix, g_pre_mlp, w_up, w_down, g_post_mlp)


def kernel(x, mem, positions, g_pre_mix, g_mem, w_in, w_mem_kv, conv_w, g_attn_out,
           g_conv_out, g_xattn_out, w_out, g_post_mix, g_pre_mlp, w_up, w_down,
           g_post_mlp):
    depth = w_in.shape[0]
    attn_width = g_attn_out.shape[1]
    conv_width = g_conv_out.shape[1]
    xattn_width = g_xattn_out.shape[1]
    pos3 = positions[:, None, :]
    row = lambda g: g[None, :]
    for l in range(depth):
        proj, y_conv, y_x, (w_out_b, w_up_b, w_down_b) = _in_proj(
            x, pos3, row(g_pre_mix[l]), w_in[l], conv_w[l], row(g_conv_out[l]),
            mem, row(g_mem[l]), w_mem_kv[l], row(g_xattn_out[l]),
            (w_out[l], w_up[l], w_down[l]),
            attn_width=attn_width, conv_width=conv_width, xattn_width=xattn_width)
        y_attn = _dil_attn(proj, attn_width=attn_width)
        x = _mix_mlp(x, y_attn, y_conv, y_x, row(g_attn_out[l]), w_out_b,
                     row(g_post_mix[l]), row(g_pre_mlp[l]), w_up_b, w_down_b,
                     row(g_post_mlp[l]))
    return x
```

```python
import functools

import jax
import jax.numpy as jnp
from jax import lax
from jax.experimental import pallas as pl
from jax.experimental.pallas import tpu as pltpu

F32 = jnp.float32
BF16 = jnp.bfloat16

HEAD_DIM = 64
DILATED_PATTERNS = ((128, 1), (512, 4), (2048, 16))
CONV_K = 3
ROPE_THETA = 10000.0
EPS = 1e-6
NEG_INF = -1e30
LOG2_E = 1.4426950408889634

LANES = 128
BF16_SUBLANES = 16
ATTN_BLK = 128
VMEM_LIMIT_BYTES = 56 * 1024 * 1024


def _rms(x, g):
    return x * lax.rsqrt(jnp.mean(x * x, axis=-1, keepdims=True) + EPS) * g


CONV_HALO = 8


def _in_proj_kernel(x_ref, pos_ref, g_ref, w_f32_ref, conv_w_ref, g_conv_ref,
                    mem_ref, g_mem_ref, w_mem_ref, g_xattn_ref, *rest,
                    attn_width, conv_width, xattn_width, chunk):
    n_cast = (len(rest) - 7) // 2
    cast_in, (o_ref, oc_ref, ox_ref) = rest[:n_cast], rest[n_cast:n_cast + 3]
    cast_out = rest[n_cast + 3:-4]
    w_ref, ztail_ref, km_ref, vm_ref = rest[-4:]
    t = pl.program_id(1)
    tm = x_ref.shape[1]
    xw = xattn_width

    @pl.when((pl.program_id(0) == 0) & (pl.program_id(1) == 0))
    def _():
        w_ref[...] = w_f32_ref[...].astype(BF16)
        ztail_ref[...] = jnp.zeros(ztail_ref.shape, F32)

    @pl.when(t == 0)
    def _():
        hm = _rms(mem_ref[0], g_mem_ref[...]).astype(BF16)
        kv = jnp.dot(hm, w_mem_ref[...].astype(BF16), preferred_element_type=F32)
        km_ref[...] = kv[:, :xw].T.astype(BF16)
        vm_ref[...] = kv[:, xw:].astype(BF16)

    h = _rms(x_ref[0], g_ref[...]).astype(BF16)

    half = HEAD_DIM // 2
    freq = lax.broadcasted_iota(jnp.int32, (half, 1), 0).astype(F32)
    inv_freq = jnp.float32(ROPE_THETA) ** (-(freq * 2.0 / HEAD_DIM))
    ang = inv_freq * pos_ref[0].astype(F32)
    cos_t, sin_t = jnp.cos(ang), jnp.sin(ang)
    reps = LANES // HEAD_DIM
    cos_k = jnp.concatenate([cos_t, cos_t] * reps, axis=0).T
    sin_k = jnp.concatenate([-sin_t, sin_t] * reps, axis=0).T
    q_scale = HEAD_DIM ** -0.5 * LOG2_E
    cos_q, sin_q = cos_k * q_scale, sin_k * q_scale
    lane = lax.broadcasted_iota(jnp.int32, (1, LANES), 1)
    first_half = (lane % HEAD_DIM) < half

    qkv = 3 * attn_width
    assert chunk == 2 * conv_width == conv_width + xattn_width and qkv % chunk == 0

    p_bc = jnp.dot(h, w_ref[:, qkv:qkv + chunk], preferred_element_type=F32)
    p_uq = jnp.dot(h, w_ref[:, qkv + chunk:qkv + 2 * chunk], preferred_element_type=F32)
    qx = (p_uq[:, conv_width:] * q_scale).astype(BF16)
    z = p_bc[:, conv_width:] * p_uq[:, :conv_width]
    tail = jnp.where(t > 0, ztail_ref[...], jnp.zeros(ztail_ref.shape, F32))
    z_ext = jnp.concatenate([tail, z], axis=0)
    cw = conv_w_ref[...]
    y_conv = z * cw[CONV_K - 1:CONV_K, :]
    for back in range(1, CONV_K):
        lo = CONV_HALO - back
        y_conv = y_conv + z_ext[lo:lo + tm, :] * cw[CONV_K - 1 - back:CONV_K - back, :]
    oc_ref[0] = _rms(p_bc[:, :conv_width] * y_conv, g_conv_ref[...]).astype(BF16)
    ztail_ref[...] = z[tm - CONV_HALO:, :]

    km = km_ref[...]
    vm = vm_ref[...]
    xlane = lax.broadcasted_iota(jnp.int32, (1, xw), 1)
    n_heads = xw // HEAD_DIM
    sels = [(xlane >= hd * HEAD_DIM) & (xlane < (hd + 1) * HEAD_DIM) for hd in range(n_heads)]
    assert xw == 2 * LANES
    heads_per_group = LANES // HEAD_DIM
    xa = {"sc": {}, "p": {}, "y": [jnp.zeros((tm, LANES), F32)] * 2}

    def xattn_stage(hd):
        if hd < n_heads:
            qh = jnp.where(sels[hd], qx, jnp.zeros_like(qx))
            xa["sc"][hd] = jnp.dot(qh, km, preferred_element_type=F32)
        if 0 <= hd - 1 < n_heads:
            sc = xa["sc"].pop(hd - 1)
            xa["p"][hd - 1] = jnp.exp2(sc - jnp.max(sc, axis=-1, keepdims=True)).astype(BF16)
        if 0 <= hd - 2 < n_heads:
            hv = hd - 2
            vm1 = jnp.where(sels[hv], vm, jnp.ones_like(vm))
            o = jnp.dot(xa["p"].pop(hv), vm1, preferred_element_type=F32)
            grp = hv // heads_per_group
            mine = o[:, grp * LANES:(grp + 1) * LANES]
            sums = o[:, (1 - grp) * LANES:(2 - grp) * LANES]
            xa["y"][grp] = jnp.where(sels[hv][:, grp * LANES:(grp + 1) * LANES],
                                     mine / sums, xa["y"][grp])

    stages = list(range(n_heads + 2))
    n_chunks = qkv // chunk
    per_chunk = -(-len(stages) // n_chunks)

    for ci, c0 in enumerate(range(0, qkv, chunk)):
        p = jnp.dot(h, w_ref[:, c0:c0 + chunk], preferred_element_type=F32)
        for hd in (stages[:3], stages[3:5], stages[5:])[ci]:
            xattn_stage(hd)
        if c0 < 2 * attn_width:
            cos, sin = (cos_q, sin_q) if c0 < attn_width else (cos_k, sin_k)
            for g0 in range(0, chunk, LANES):
                v = p[:, g0:g0 + LANES]
                rot = jnp.where(first_half,
                                pltpu.roll(v, LANES - half, 1),
                                pltpu.roll(v, half, 1))
                r = v * cos + rot * sin
                o_ref[0, :, c0 + g0:c0 + g0 + LANES] = r.astype(BF16)
        else:
            o_ref[0, :, c0:c0 + chunk] = p.astype(BF16)
        if c0 == 2 * attn_width:
            for src, dst in zip(cast_in, cast_out):
                dst[...] = src[...].astype(BF16)
    ox_ref[0] = _rms(jnp.concatenate(xa["y"], axis=-1), g_xattn_ref[...]).astype(BF16)


def _in_proj(x, pos3, g, w_in, conv_w, g_conv, mem, g_mem, w_mem, g_xattn, later_weights,
             *, attn_width, conv_width, xattn_width, tm=1024, chunk=512):
    B, S, D = x.shape
    n_in = w_in.shape[1]
    n_out = 3 * attn_width
    n_mem = mem.shape[1]
    assert n_in == n_out + 3 * conv_width + xattn_width
    n_steps = B * (S // tm)
    slices = []
    for w in later_weights:
        rows = w.shape[0] // n_steps
        assert rows * n_steps == w.shape[0] and rows % BF16_SUBLANES == 0
        slices.append(pl.BlockSpec((rows, w.shape[1]), lambda b, t: (b * (S // tm) + t, 0)))
    out = pl.pallas_call(
        functools.partial(_in_proj_kernel, attn_width=attn_width, conv_width=conv_width,
                          xattn_width=xattn_width, chunk=chunk),
        grid=(B, S // tm),
        in_specs=[
            pl.BlockSpec((1, tm, D), lambda b, t: (b, t, 0)),
            pl.BlockSpec((1, 1, tm), lambda b, t: (b, 0, t)),
            pl.BlockSpec((1, D), lambda b, t: (0, 0)),
            pl.BlockSpec((D, n_in), lambda b, t: (0, 0), pipeline_mode=pl.Buffered(1)),
            pl.BlockSpec(conv_w.shape, lambda b, t: (0, 0)),
            pl.BlockSpec((1, conv_width), lambda b, t: (0, 0)),
            pl.BlockSpec((1, n_mem, D), lambda b, t: (b, 0, 0)),
            pl.BlockSpec((1, D), lambda b, t: (0, 0)),
            pl.BlockSpec(w_mem.shape, lambda b, t: (0, 0), pipeline_mode=pl.Buffered(1)),
            pl.BlockSpec((1, xattn_width), lambda b, t: (0, 0)),
        ] + slices,
        out_specs=[pl.BlockSpec((1, tm, n_out), lambda b, t: (b, t, 0)),
                   pl.BlockSpec((1, tm, conv_width), lambda b, t: (b, t, 0)),
                   pl.BlockSpec((1, tm, xattn_width), lambda b, t: (b, t, 0))] + slices,
        out_shape=[jax.ShapeDtypeStruct((B, S, n_out), BF16),
                   jax.ShapeDtypeStruct((B, S, conv_width), BF16),
                   jax.ShapeDtypeStruct((B, S, xattn_width), BF16)]
        + [jax.ShapeDtypeStruct(w.shape, BF16) for w in later_weights],
        scratch_shapes=[pltpu.VMEM(w_in.shape, BF16),
                        pltpu.VMEM((CONV_HALO, conv_width), F32),
                        pltpu.VMEM((xattn_width, n_mem), BF16),
                        pltpu.VMEM((n_mem, xattn_width), BF16)],
        compiler_params=pltpu.CompilerParams(
            dimension_semantics=("arbitrary", "arbitrary"),
            vmem_limit_bytes=VMEM_LIMIT_BYTES),
        name="in_proj",
    )(x, pos3, g, w_in, conv_w, g_conv, mem, g_mem, w_mem, g_xattn, *later_weights)
    return out[0], out[1], out[2], out[3:]


def _band_block(qb, kb, vb1, bias, lane_lo):
    ms, ls, accs = [], [], []
    for head_lo in (True, False):
        sel = lane_lo if head_lo else jnp.logical_not(lane_lo)
        qh = jnp.where(sel, qb, jnp.zeros_like(qb))
        s = lax.dot_general(qh, kb, (((1,), (1,)), ((), ())),
                            preferred_element_type=F32) + bias
        m = jnp.max(s, axis=-1, keepdims=True)
        p = jnp.exp2(s - m).astype(BF16)
        r = jnp.dot(p, vb1, preferred_element_type=F32)
        accs.append(r[:, :LANES])
        ls.append(r[:, LANES:])
        ms.append(jnp.broadcast_to(m, (m.shape[0], LANES)))
    return (jnp.where(lane_lo, ms[0], ms[1]), jnp.where(lane_lo, ls[0], ls[1]),
            jnp.where(lane_lo, accs[0], accs[1]))


COPY_ROWS = 64
DEINTERLEAVE = 4


def _for_each_split(seq, region, body):
    run = region // DEINTERLEAVE
    span = DEINTERLEAVE * COPY_ROWS
    steps_per_region = region // span

    def step(t, carry):
        strided0 = pl.multiple_of(t * span, span)
        dense0 = pl.multiple_of((t // steps_per_region) * region
                                + (t % steps_per_region) * COPY_ROWS, COPY_ROWS)
        for j in range(DEINTERLEAVE):
            body(pl.ds(strided0 + j, COPY_ROWS, stride=DEINTERLEAVE),
                 pl.ds(dense0 + j * run, COPY_ROWS))
        return carry

    lax.fori_loop(0, seq // span, step, 0)


def _merge_softmax(a, b):
    (m_a, l_a, a_a), (m_b, l_b, a_b) = a, b
    m = jnp.maximum(m_a, m_b)
    w_a = jnp.exp2(m_a - m)
    w_b = jnp.exp2(m_b - m)
    return m, l_a * w_a + l_b * w_b, a_a * w_a + a_b * w_b


def _dil_attn_kernel(trips_ref, q_ref, k_ref, v_ref, o_ref,
                     qf, kf, vf, q4f, k4f, v4f, qp, kp, vp1,
                     m_p4, l_p4, a_p4, m_tmp, l_tmp, a_tmp, bias_ref,
                     *, seq, patterns, unroll):
    blk = ATTN_BLK
    n_blocks = seq // blk
    lane_lo = lax.broadcasted_iota(jnp.int32, (1, LANES), 1) < HEAD_DIM
    assert [d for _, d in patterns] == [1, DEINTERLEAVE, DEINTERLEAVE ** 2]
    assert all(w // d == blk for w, d in patterns)
    run4 = seq // DEINTERLEAVE
    run16 = run4 // DEINTERLEAVE
    blocks_per_run4 = run4 // blk
    blocks_per_run16 = run16 // blk
    assert n_blocks % unroll == 0 and unroll % blocks_per_run4 == 0
    n_trips = trips_ref[0]

    @pl.when((pl.program_id(0) == 0) & (pl.program_id(1) == 0))
    def _():
        qi = lax.broadcasted_iota(jnp.int32, (blk, 2 * blk), 0)
        kj = lax.broadcasted_iota(jnp.int32, (blk, 2 * blk), 1)
        bias_ref[0] = jnp.where(kj <= qi, 0.0, NEG_INF).astype(F32)
        bias_ref[1] = jnp.where((kj >= qi) & (kj <= qi + blk), 0.0, NEG_INF).astype(F32)
        vp1[:, LANES:] = jnp.ones((seq, LANES), BF16)

    pat4 = (m_p4, l_p4, a_p4)
    tmp = (m_tmp, l_tmp, a_tmp)

    def attend(sub_len, q_src, k_src, finish):
        nb = sub_len // blk
        assert nb >= 2

        def trip_body(trip, carry):
            for u in range(unroll):
                g = trip * unroll + u
                n = g % nb
                q0 = pl.multiple_of(g * blk, blk)
                k0 = pl.multiple_of(jnp.where(n > 0, q0 - blk, q0), blk)
                triple = _band_block(q_src[pl.ds(q0, blk), :],
                                     k_src[pl.ds(k0, 2 * blk), :],
                                     vp1[pl.ds(k0, 2 * blk), :],
                                     bias_ref[jnp.minimum(n, 1)], lane_lo)
                finish(trip, u, q0, triple)
            return carry

        lax.fori_loop(0, n_trips, trip_body, 0)

    qf[...] = q_ref[0].astype(F32)
    kf[...] = k_ref[0].astype(F32)
    vf[...] = v_ref[0].astype(F32)

    def gather4(strided, dense):
        for src, dst_f, dst_b in ((qf, q4f, qp), (kf, k4f, kp)):
            rows = src[strided, :]
            dst_f[dense, :] = rows
            dst_b[dense, :] = rows.astype(BF16)
        rows = vf[strided, :]
        v4f[dense, :] = rows
        vp1[dense, :LANES] = rows.astype(BF16)

    def store_pat4(trip, u, q0, triple):
        for ref, val in zip(pat4, triple):
            ref[pl.ds(q0, blk), :] = val

    _for_each_split(seq, seq, gather4)
    attend(run4, qp, kp, store_pat4)

    def gather16(strided, dense):
        qp[dense, :] = q4f[strided, :].astype(BF16)
        kp[dense, :] = k4f[strided, :].astype(BF16)
        vp1[dense, :LANES] = v4f[strided, :].astype(BF16)

    def fold_into_pat4(trip, u, q0, triple):
        r_static, idx = divmod(u, blocks_per_run4)
        j, a0 = idx // blocks_per_run16, (idx % blocks_per_run16) * blk
        r_dyn = pl.multiple_of(trip * (unroll // blocks_per_run4) * run4, run4)
        rows = pl.ds(r_dyn + r_static * run4 + j + DEINTERLEAVE * a0, blk,
                     stride=DEINTERLEAVE)
        merged = _merge_softmax(tuple(ref[rows, :] for ref in pat4), triple)
        for ref, val in zip(pat4, merged):
            ref[rows, :] = val

    _for_each_split(seq, run4, gather16)
    attend(run16, qp, kp, fold_into_pat4)

    def finish_natural(trip, u, q0, triple):
        sub = blk // DEINTERLEAVE
        i_dyn = pl.multiple_of(trip * (unroll * sub), unroll * sub)
        for j in range(DEINTERLEAVE):
            src = pl.ds(i_dyn + j * run4 + u * sub, sub)
            for t_ref, p_ref in zip(tmp, pat4):
                t_ref[u, pl.ds(j, sub, stride=DEINTERLEAVE), :] = p_ref[src, :]
        _, l, acc = _merge_softmax(tuple(t_ref[u] for t_ref in tmp), triple)
        o_ref[0, pl.ds(q0, blk), :] = (acc / l).astype(o_ref.dtype)

    vp1[:, :LANES] = v_ref[0]
    attend(seq, q_ref.at[0], k_ref.at[0], finish_natural)


def _dil_attn(proj, *, attn_width, unroll=32):
    B, S, _ = proj.shape
    n_pairs = attn_width // LANES
    trips = jnp.full((1,), S // ATTN_BLK // unroll, jnp.int32)
    kern = functools.partial(_dil_attn_kernel, seq=S, patterns=DILATED_PATTERNS,
                             unroll=unroll)
    col = lambda off: (lambda b, hp: (b, 0, off + hp))
    f32_buf = pltpu.VMEM((S, LANES), F32)
    bf16_buf = pltpu.VMEM((S, LANES), BF16)
    tmp_buf = pltpu.VMEM((unroll, ATTN_BLK, LANES), F32)
    return pl.pallas_call(
        kern,
        grid=(B, n_pairs),
        in_specs=[pl.BlockSpec(memory_space=pltpu.SMEM),
                  pl.BlockSpec((1, S, LANES), col(0)),
                  pl.BlockSpec((1, S, LANES), col(n_pairs)),
                  pl.BlockSpec((1, S, LANES), col(2 * n_pairs))],
        out_specs=pl.BlockSpec((1, S, LANES), lambda b, hp: (b, 0, hp)),
        out_shape=jax.ShapeDtypeStruct((B, S, attn_width), BF16),
        scratch_shapes=[f32_buf, f32_buf, f32_buf,
                        f32_buf, f32_buf, f32_buf,
                        bf16_buf, bf16_buf,
                        pltpu.VMEM((S, 2 * LANES), BF16),
                        f32_buf, f32_buf, f32_buf,
                        tmp_buf, tmp_buf, tmp_buf,
                        pltpu.VMEM((2, ATTN_BLK, 2 * ATTN_BLK), F32)],
        compiler_params=pltpu.CompilerParams(
            dimension_semantics=("arbitrary", "arbitrary"),
            vmem_limit_bytes=VMEM_LIMIT_BYTES),
        name="dil_attn",
    )(trips, proj, proj, proj)


def _mix_mlp_kernel(x_ref, ya_ref, yc_ref, yx_ref, g_attn_ref, w_out_ref, g_post_mix_ref,
                    g_pre_mlp_ref, w_up_ref, w_down_ref, g_post_mlp_ref, o_ref, *, ff_chunk):
    tm = x_ref.shape[1]

    halves = ((0, tm // 2), (tm // 2, tm))

    x1_h, h2_h = [], []
    for r0, r1 in halves:
        y = jnp.concatenate([
            _rms(ya_ref[0, r0:r1, :].astype(F32), g_attn_ref[...]).astype(BF16),
            yc_ref[0, r0:r1, :], yx_ref[0, r0:r1, :]], axis=-1)
        y = jnp.dot(y, w_out_ref[...], preferred_element_type=F32)
        x1 = x_ref[0, r0:r1, :] + _rms(y, g_post_mix_ref[...])
        x1_h.append(x1)
        h2_h.append(_rms(x1, g_pre_mlp_ref[...]).astype(BF16))

    def act_fn(up):
        return jnp.square(jnp.maximum(up, 0.0)).astype(BF16)

    chunks = list(range(0, w_up_ref.shape[1], ff_chunk))
    h2 = jnp.concatenate(h2_h, axis=0)
    acc = None
    for f0 in chunks:
        w_up_c = w_up_ref[:, f0:f0 + ff_chunk]
        if f0 == chunks[0]:
            act = jnp.concatenate(
                [act_fn(jnp.dot(h, w_up_c, preferred_element_type=F32)) for h in h2_h],
                axis=0)
        else:
            act = act_fn(jnp.dot(h2, w_up_c, preferred_element_type=F32))
        w_down_c = w_down_ref[f0:f0 + ff_chunk, :]
        if f0 != chunks[-1]:
            part = jnp.dot(act, w_down_c, preferred_element_type=F32)
            acc = part if acc is None else acc + part
        else:
            for (r0, r1), x1 in zip(halves, x1_h):
                f = acc[r0:r1] + jnp.dot(act[r0:r1], w_down_c, preferred_element_type=F32)
                o_ref[0, r0:r1, :] = x1 + _rms(f, g_post_mlp_ref[...])


def _mix_mlp(x, y_attn, y_conv, y_x, g_attn, w_out, g_post_mix, g_pre_mlp, w_up, w_down,
             g_post_mlp, *, tm=512, ff_chunk=1024):
    B, S, D = x.shape
    d_ff = w_up.shape[1]
    const = lambda shape: pl.BlockSpec(shape, lambda b, t: (0,) * len(shape),
                                       pipeline_mode=pl.Buffered(1))
    rows = lambda a: pl.BlockSpec((1, tm, a.shape[-1]), lambda b, t: (b, t, 0))
    return pl.pallas_call(
        functools.partial(_mix_mlp_kernel, ff_chunk=ff_chunk),
        grid=(B, S // tm),
        in_specs=[
            rows(x), rows(y_attn), rows(y_conv), rows(y_x),
            const((1, y_attn.shape[-1])), const((D, D)), const((1, D)),
            const((1, D)), const((D, d_ff)), const((d_ff, D)), const((1, D)),
        ],
        out_specs=rows(x),
        out_shape=jax.ShapeDtypeStruct((B, S, D), x.dtype),
        compiler_params=pltpu.CompilerParams(
            dimension_semantics=("parallel", "parallel"),
            vmem_limit_bytes=VMEM_LIMIT_BYTES),
        name="mix_mlp",
    )(x, y_attn, y_conv, y_x, g_attn, w_out, g_post_mix, g_pre_mlp, w_up, w_down, g_post_mlp)


def kernel(x, mem, positions, g_pre_mix, g_mem, w_in, w_mem_kv, conv_w, g_attn_out,
           g_conv_out, g_xattn_out, w_out, g_post_mix, g_pre_mlp, w_up, w_down,
           g_post_mlp):
    depth = w_in.shape[0]
    attn_width = g_attn_out.shape[1]
    conv_width = g_conv_out.shape[1]
    xattn_width = g_xattn_out.shape[1]
    pos3 = positions[:, None, :]
    row = lambda g: g[None, :]
    for l in range(depth):
        proj, y_conv, y_x, (w_out_b, w_up_b, w_down_b) = _in_proj(
            x, pos3, row(g_pre_mix[l]), w_in[l], conv_w[l], row(g_conv_out[l]),
            mem, row(g_mem[l]), w_mem_kv[l], row(g_xattn_out[l]),
            (w_out[l], w_up[l], w_down[l]),
            attn_width=attn_width, conv_width=conv_width, xattn_width=xattn_width)
        y_attn = _dil_attn(proj, attn_width=attn_width)
        x = _mix_mlp(x, y_attn, y_conv, y_x, row(g_attn_out[l]), w_out_b,
                     row(g_post_mix[l]), row(g_pre_mlp[l]), w_up_b, w_down_b,
                     row(g_post_mlp[l]))
    return x
```
